```python
import jax, jax.numpy as jnp
from jax import lax
import numpy as np

D_MODEL = 2048
BATCH = 2
SEQ = 16384
DEPTH = 2

HEAD_DIM = 128
FOX_HEADS = 8
DSA_HEADS = 8
MLA_HEADS = 8
BRANCH_WIDTH = FOX_HEADS * HEAD_DIM
N_BRANCH = 3
IDX_HEADS = 16
IDX_DIM = 64
TOPK_MAX = 256
Q_LORA = 512
KV_LORA = 512
QK_NOPE = 128
QK_ROPE = 64
V_DIM = 128
ROPE_THETA = 10000.0
BLOCK = 128
EPS = 1e-6

IN_SPLITS = (
    BRANCH_WIDTH, BRANCH_WIDTH, BRANCH_WIDTH, FOX_HEADS,
    BRANCH_WIDTH, BRANCH_WIDTH, BRANCH_WIDTH,
    IDX_HEADS * IDX_DIM, IDX_DIM, IDX_HEADS,
    Q_LORA, KV_LORA, QK_ROPE,
    N_BRANCH * BRANCH_WIDTH,
    N_BRANCH * D_MODEL,
)
D_IN = sum(IN_SPLITS)

kernel_name = 'hybrid_gated_fox_dsa_mla'


def _rmsnorm(x, g):
    xf = x.astype(jnp.float32)
    y = xf * lax.rsqrt(jnp.mean(xf * xf, axis=-1, keepdims=True) + EPS)
    return (y * g.astype(jnp.float32)).astype(x.dtype)


def _rope(x, pos):
    half = x.shape[-1] // 2
    inv = ROPE_THETA ** (-jnp.arange(half, dtype=jnp.float32) / half)
    ang = pos[:, None] * inv[None, :]
    cos = jnp.cos(ang)[:, None, :]
    sin = jnp.sin(ang)[:, None, :]
    xf = x.astype(jnp.float32)
    x1, x2 = xf[..., :half], xf[..., half:]
    return jnp.concatenate([x1 * cos - x2 * sin, x2 * cos + x1 * sin], axis=-1).astype(x.dtype)


def _to_blocks(a):
    b, s = a.shape[:2]
    return jnp.moveaxis(a.reshape((b, s // BLOCK, BLOCK) + a.shape[2:]), 1, 0)


def _from_blocks(a):
    a = jnp.moveaxis(a, 0, 1)
    return a.reshape((a.shape[0], a.shape[1] * a.shape[2]) + a.shape[3:])


def _causal_block_attention(q, k, v, scale, log_cum=None):
    s = q.shape[1]
    key_pos = jnp.arange(s)
    starts = jnp.arange(s // BLOCK) * BLOCK
    cum_k = None if log_cum is None else jnp.moveaxis(log_cum, 1, 2)[:, :, None, :]
    cum_q = None if log_cum is None else _to_blocks(log_cum)

    def one_block(args):
        q_blk, t0, c_blk = args
        logits = jnp.einsum('bthd,bshd->bhts', q_blk, k, preferred_element_type=jnp.float32) * scale
        if c_blk is not None:
            logits = logits + jnp.moveaxis(c_blk, 1, 2)[..., None] - cum_k
        q_pos = t0 + jnp.arange(BLOCK)
        logits = jnp.where(q_pos[:, None] >= key_pos[None, :], logits, -jnp.inf)
        p = jax.nn.softmax(logits, axis=-1)
        return jnp.einsum('bhts,bshd->bthd', p.astype(v.dtype), v)

    return _from_blocks(lax.map(one_block, (_to_blocks(q), starts, cum_q)))


def _dsa_attention(q, k, v, q_idx, k_idx, w_idx, n_sel):
    s = q.shape[1]
    key_pos = jnp.arange(s)
    starts = jnp.arange(s // BLOCK) * BLOCK
    idx_scale = (IDX_HEADS * IDX_DIM) ** -0.5
    attn_scale = HEAD_DIM ** -0.5
    gather = jax.vmap(lambda table, rows: table[rows])

    def one_block(args):
        q_blk, qi_blk, wi_blk, t0 = args
        q_pos = t0 + jnp.arange(BLOCK)
        rel = jax.nn.relu(jnp.einsum('bthe,bse->bths', qi_blk, k_idx, preferred_element_type=jnp.float32))
        score = jnp.einsum('bths,bth->bts', rel, wi_blk.astype(jnp.float32)) * idx_scale
        score = jnp.where(q_pos[:, None] >= key_pos[None, :], score, -jnp.inf)
        _, sel = lax.top_k(score, n_sel)
        k_sel = gather(k, sel)
        v_sel = gather(v, sel)
        logits = jnp.einsum('bthd,btkhd->bhtk', q_blk, k_sel, preferred_element_type=jnp.float32) * attn_scale
        valid = (sel <= q_pos[None, :, None])[:, None]
        p = jax.nn.softmax(jnp.where(valid, logits, -jnp.inf), axis=-1)
        return jnp.einsum('bhtk,btkhd->bthd', p.astype(v.dtype), v_sel)

    return _from_blocks(lax.map(one_block, (_to_blocks(q), _to_blocks(q_idx), _to_blocks(w_idx), starts)))


def _layer(x, pos, n_sel, norm_g, w_in, forget_b, q_norm_g, w_q_up, kv_norm_g, w_kv_up, w_branch, w_out):
    b, s, _ = x.shape
    h = _rmsnorm(x, norm_g)
    z = jnp.einsum('bsd,de->bse', h, w_in)
    points = np.cumsum(IN_SPLITS)[:-1].tolist()
    (f_q, f_k, f_v, f_gate, d_q, d_k, d_v, i_q, i_k, i_w,
     c_q, c_kv, k_pe, silu_g, merge_g) = jnp.split(z, points, axis=-1)

    def heads(t, n):
        return t.reshape(b, s, n, -1)

    log_f = jax.nn.log_sigmoid(f_gate.astype(jnp.float32) + forget_b.astype(jnp.float32))
    cum = jnp.cumsum(log_f, axis=1)
    o_fox = _causal_block_attention(heads(f_q, FOX_HEADS), heads(f_k, FOX_HEADS), heads(f_v, FOX_HEADS),
                                    HEAD_DIM ** -0.5, cum)

    o_dsa = _dsa_attention(_rope(heads(d_q, DSA_HEADS), pos), _rope(heads(d_k, DSA_HEADS), pos),
                           heads(d_v, DSA_HEADS), _rope(heads(i_q, IDX_HEADS), pos),
                           _rope(i_k[:, :, None, :], pos)[:, :, 0], i_w, n_sel)

    q = heads(_rmsnorm(c_q, q_norm_g) @ w_q_up, MLA_HEADS)
    q = jnp.concatenate([q[..., :QK_NOPE], _rope(q[..., QK_NOPE:], pos)], axis=-1)
    kv = heads(_rmsnorm(c_kv, kv_norm_g) @ w_kv_up, MLA_HEADS)
    k_rot = jnp.broadcast_to(_rope(k_pe[:, :, None, :], pos), (b, s, MLA_HEADS, QK_ROPE))
    k = jnp.concatenate([kv[..., :QK_NOPE], k_rot], axis=-1)
    o_mla = _causal_block_attention(q, k, kv[..., QK_NOPE:], (QK_NOPE + QK_ROPE) ** -0.5)

    outs = (o_fox.reshape(b, s, -1), o_dsa.reshape(b, s, -1), o_mla.reshape(b, s, -1))
    gates = jnp.split(silu_g, N_BRANCH, axis=-1)
    merges = jnp.split(merge_g, N_BRANCH, axis=-1)
    terms = [jax.nn.sigmoid(m) * ((o * jax.nn.silu(g)) @ w_branch[i])
             for i, (o, g, m) in enumerate(zip(outs, gates, merges))]
    y = terms[0] + terms[1] + terms[2]
    return x + y @ w_out


def setup_inputs(seed: int = 0) -> dict:
    key = jax.random.key(seed)
    ks = jax.random.split(key, 12)
    nrm = jax.random.normal
    f32 = jnp.float32
    return {
        'x': nrm(ks[0], (BATCH, SEQ, D_MODEL), f32),
        'norm_g': 1.0 + 0.02 * nrm(ks[1], (DEPTH, D_MODEL), f32),
        'w_in': nrm(ks[2], (DEPTH, D_MODEL, D_IN), f32) * D_MODEL ** -0.5,
        'forget_b': 3.0 + 0.5 * nrm(ks[3], (DEPTH, FOX_HEADS), f32),
        'q_norm_g': 1.0 + 0.02 * nrm(ks[4], (DEPTH, Q_LORA), f32),
        'w_q_up': nrm(ks[5], (DEPTH, Q_LORA, MLA_HEADS * (QK_NOPE + QK_ROPE)), f32) * Q_LORA ** -0.5,
        'kv_norm_g': 1.0 + 0.02 * nrm(ks[6], (DEPTH, KV_LORA), f32),
        'w_kv_up': nrm(ks[7], (DEPTH, KV_LORA, MLA_HEADS * (QK_NOPE + V_DIM)), f32) * KV_LORA ** -0.5,
        'w_branch': nrm(ks[8], (DEPTH, N_BRANCH, BRANCH_WIDTH, D_MODEL), f32) * BRANCH_WIDTH ** -0.5,
        'w_out': nrm(ks[9], (DEPTH, D_MODEL, D_MODEL), f32) * D_MODEL ** -0.5,
        'final_norm_g': 1.0 + 0.02 * nrm(ks[10], (D_MODEL,), f32),
    }


def reference(x, norm_g, w_in, forget_b, q_norm_g, w_q_up, kv_norm_g, w_kv_up, w_branch, w_out, final_norm_g):
    s = x.shape[1]
    pos = jnp.arange(s, dtype=jnp.float32)
    n_sel = min(TOPK_MAX, s // 4)
    for l in range(DEPTH):
        x = _layer(x, pos, n_sel, norm_g[l], w_in[l], forget_b[l], q_norm_g[l], w_q_up[l],
                   kv_norm_g[l], w_kv_up[l], w_branch[l], w_out[l])
    return _rmsnorm(x, final_norm_g)
```

```python
import functools

import jax
import jax.numpy as jnp
import numpy as np
from jax import lax
from jax.experimental import pallas as pl
from jax.experimental.pallas import tpu as pltpu

F32 = jnp.float32
BF16 = jnp.bfloat16

HEAD_DIM = 128
N_HEADS = 8
BRANCH_WIDTH = N_HEADS * HEAD_DIM
N_BRANCH = 3
IDX_HEADS = 16
IDX_DIM = 64
TOPK_MAX = 256
Q_LORA = 512
KV_LORA = 512
QK_NOPE = 128
QK_ROPE = 64
V_DIM = 128
MLA_QK_PAD = 256
ROPE_THETA = 10000.0
EPS = 1e-6

LANES = 128
VMEM_LIMIT = 56 * 1024 * 1024
NEG_BIG = -1e30
INT_MIN = -(2 ** 31)

COL_FQ, COL_FK, COL_FV = 0, 1024, 2048
COL_DQ, COL_DK, COL_DV = 3072, 4096, 5120
COL_IQ = 6144
COL_CQ = 7168
COL_MERGE = 8192


def _params(*sem):
    return pltpu.CompilerParams(dimension_semantics=sem, vmem_limit_bytes=VMEM_LIMIT)


def _rope_tables(s):
    pos = jnp.arange(s, dtype=F32)

    def cs(half):
        inv = ROPE_THETA ** (-jnp.arange(half, dtype=F32) / half)
        ang = pos[:, None] * inv[None, :]
        return jnp.cos(ang), jnp.sin(ang)

    c64, s64 = cs(64)
    c32, s32 = cs(32)
    z32 = jnp.zeros_like(s32)
    cos128 = jnp.concatenate([c64, c64], axis=-1)
    sin128 = jnp.concatenate([-s64, s64], axis=-1)
    cos64 = jnp.concatenate([c32, c32, c32, c32], axis=-1)
    sin64_lo = jnp.concatenate([-s32, z32, -s32, z32], axis=-1)
    sin64_hi = jnp.concatenate([z32, s32, z32, s32], axis=-1)
    return cos128, sin128, cos64, sin64_lo, sin64_hi


def _rope128(x, cos, sin):
    return x * cos + pltpu.roll(x, 64, 1) * sin


def _rope64(x, cos, sin_lo, sin_hi):
    return x * cos + pltpu.roll(x, 96, 1) * sin_lo + pltpu.roll(x, 32, 1) * sin_hi


def _in_proj_kernel(x_ref, g_ref, w_ref, ws_ref, c128_ref, s128_ref, c64_ref, s64l_ref, s64h_ref,
                    zb_ref, zs_ref, h_scr, *, tn, q_scale):
    j = pl.program_id(1)

    @pl.when(j == 0)
    def _():
        xf = x_ref[...]
        y = xf * lax.rsqrt(jnp.mean(xf * xf, axis=-1, keepdims=True) + EPS)
        h = (y * g_ref[...]).astype(BF16)
        h_scr[...] = h
        zs = jnp.dot(h, ws_ref[...], preferred_element_type=F32)
        zs_ref[:, :LANES] = zs[:, :LANES]
        zs_ref[:, LANES:] = _rope64(zs[:, LANES:], c64_ref[...], s64l_ref[...], s64h_ref[...])

    acc = jnp.dot(h_scr[...], w_ref[...], preferred_element_type=F32)
    col = j * tn
    is_q = ((col >= COL_FQ) & (col < COL_FK)) | ((col >= COL_DQ) & (col < COL_DK))
    acc = acc * jnp.where(is_q, q_scale, 1.0).astype(F32)
    is_r128 = (col >= COL_DQ) & (col < COL_DV)
    is_r64 = (col >= COL_IQ) & (col < COL_CQ)

    @pl.when(is_r128)
    def _():
        for g in range(tn // LANES):
            sl = slice(g * LANES, (g + 1) * LANES)
            zb_ref[:, sl] = _rope128(acc[:, sl], c128_ref[...], s128_ref[...]).astype(BF16)

    @pl.when(is_r64)
    def _():
        for g in range(tn // LANES):
            sl = slice(g * LANES, (g + 1) * LANES)
            zb_ref[:, sl] = _rope64(acc[:, sl], c64_ref[...], s64l_ref[...], s64h_ref[...]).astype(BF16)

    @pl.when(jnp.logical_not(is_r128 | is_r64))
    def _():
        zb_ref[...] = acc.astype(BF16)


def _in_proj(x2, g, w_big, w_small, tabs, tm, tn):
    m, d = x2.shape
    n = w_big.shape[1]
    c128, s128, c64, s64l, s64h = tabs
    s = c128.shape[0]
    nsb = s // tm
    tab_spec = pl.BlockSpec((tm, LANES), lambda i, j: (i % nsb, 0))
    return pl.pallas_call(
        functools.partial(_in_proj_kernel, tn=tn, q_scale=HEAD_DIM ** -0.5),
        grid=(m // tm, n // tn),
        in_specs=[
            pl.BlockSpec((tm, d), lambda i, j: (i, 0)),
            pl.BlockSpec((1, d), lambda i, j: (0, 0)),
            pl.BlockSpec((d, tn), lambda i, j: (0, j)),
            pl.BlockSpec((d, 2 * LANES), lambda i, j: (0, 0)),
            tab_spec, tab_spec, tab_spec, tab_spec, tab_spec,
        ],
        out_specs=[
            pl.BlockSpec((tm, tn), lambda i, j: (i, j)),
            pl.BlockSpec((tm, 2 * LANES), lambda i, j: (i, 0)),
        ],
        out_shape=[
            jax.ShapeDtypeStruct((m, n), BF16),
            jax.ShapeDtypeStruct((m, 2 * LANES), F32),
        ],
        scratch_shapes=[pltpu.VMEM((tm, d), BF16)],
        compiler_params=_params("parallel", "arbitrary"),
        name="in_proj",
    )(x2, g, w_big, w_small, c128, s128, c64, s64l, s64h)


def _cum_kernel(zs_ref, fb_ref, cum_ref, carry_scr, *, t):
    @pl.when(pl.program_id(1) == 0)
    def _():
        carry_scr[...] = jnp.zeros_like(carry_scr)

    x = zs_ref[:, :N_HEADS] + fb_ref[...]
    logf = -(jnp.maximum(-x, 0.0) + jnp.log1p(jnp.exp(-jnp.abs(x))))
    r = lax.broadcasted_iota(jnp.int32, (t, t), 0)
    c = lax.broadcasted_iota(jnp.int32, (t, t), 1)
    tri = jnp.where(c <= r, 1.0, 0.0).astype(F32)
    cum = jnp.dot(tri, logf, preferred_element_type=F32, precision=lax.Precision.HIGHEST) + carry_scr[...]
    cum_ref[...] = cum
    carry_scr[...] = cum[t - 1:t, :]


def _forget_cumsum(zs, fb, b, s, t):
    return pl.pallas_call(
        functools.partial(_cum_kernel, t=t),
        grid=(b, s // t),
        in_specs=[
            pl.BlockSpec((t, 2 * LANES), lambda bi, i: (bi * (s // t) + i, 0)),
            pl.BlockSpec((1, N_HEADS), lambda bi, i: (0, 0)),
        ],
        out_specs=pl.BlockSpec((t, N_HEADS), lambda bi, i: (bi * (s // t) + i, 0)),
        out_shape=jax.ShapeDtypeStruct((b * s, N_HEADS), F32),
        scratch_shapes=[pltpu.VMEM((1, N_HEADS), F32)],
        compiler_params=_params("parallel", "arbitrary"),
        name="forget_cumsum",
    )(zs, fb)


def _flash_kernel(qt_ref, kt_ref, q_ref, k_ref, v_ref, *rest, mode, dk, t):
    if mode == "plain":
        o_ref, m_scr, l_scr, acc_scr = rest
        extra_ref = None
    else:
        extra_ref, o_ref, m_scr, l_scr, acc_scr = rest
    p = pl.program_id(1)
    qi = qt_ref[p]
    ki = kt_ref[p]

    @pl.when(ki == 0)
    def _():
        m_scr[...] = jnp.full_like(m_scr, NEG_BIG)
        l_scr[...] = jnp.zeros_like(l_scr)
        acc_scr[...] = jnp.zeros_like(acc_scr)

    def step(diag):
        if mode == "mask":
            keep = extra_ref[...].astype(jnp.int32) != 0
        elif diag:
            kr = lax.broadcasted_iota(jnp.int32, (t, t), 0)
            qc = lax.broadcasted_iota(jnp.int32, (t, t), 1)
            keep = kr <= qc
        else:
            keep = None
        for h in range(N_HEADS):
            k_h = k_ref[:, h * dk:(h + 1) * dk]
            q_h = q_ref[:, h * dk:(h + 1) * dk]
            st = lax.dot_general(k_h, q_h, (((1,), (1,)), ((), ())), preferred_element_type=F32)
            if mode == "fox":
                st = st - extra_ref[:, h:h + 1]
            if keep is not None:
                st = jnp.where(keep, st, NEG_BIG)
            m_prev = m_scr[h:h + 1, :]
            m_new = jnp.maximum(m_prev, jnp.max(st, axis=0, keepdims=True))
            alpha = jnp.exp(m_prev - m_new)
            pt = jnp.exp(st - m_new)
            l_scr[h:h + 1, :] = alpha * l_scr[h:h + 1, :] + jnp.sum(pt, axis=0, keepdims=True)
            v_h = v_ref[:, h * HEAD_DIM:(h + 1) * HEAD_DIM]
            pv = lax.dot_general(v_h, pt.astype(BF16), (((0,), (0,)), ((), ())), preferred_element_type=F32)
            sl = slice(h * HEAD_DIM, (h + 1) * HEAD_DIM)
            acc_scr[sl, :] = alpha * acc_scr[sl, :] + pv
            m_scr[h:h + 1, :] = m_new

    if mode == "mask":
        step(False)
    else:
        @pl.when(ki == qi)
        def _():
            step(True)

        @pl.when(ki != qi)
        def _():
            step(False)

    @pl.when(ki == qi)
    def _():
        for h in range(N_HEADS):
            sl = slice(h * HEAD_DIM, (h + 1) * HEAD_DIM)
            o = acc_scr[sl, :] / l_scr[h:h + 1, :]
            o_ref[:, sl] = o.T.astype(BF16)


def _pair_tables(n):
    qt = np.concatenate([np.full(i + 1, i, np.int32) for i in range(n)])
    kt = np.concatenate([np.arange(i + 1, dtype=np.int32) for i in range(n)])
    return jnp.asarray(qt), jnp.asarray(kt)


def _flash(q_arr, q_col, k_arr, k_col, v_arr, v_col, b, s, t, dk, mode, extra=None):
    n = s // t
    qt, kt = _pair_tables(n)
    wq = N_HEADS * dk
    wv = N_HEADS * HEAD_DIM
    in_specs = [
        pl.BlockSpec((t, wq), lambda bi, p, qt, kt: (bi * n + qt[p], q_col)),
        pl.BlockSpec((t, wq), lambda bi, p, qt, kt: (bi * n + kt[p], k_col)),
        pl.BlockSpec((t, wv), lambda bi, p, qt, kt: (bi * n + kt[p], v_col)),
    ]
    args = [q_arr, k_arr, v_arr]
    if mode == "fox":
        in_specs.append(pl.BlockSpec((t, N_HEADS), lambda bi, p, qt, kt: (bi * n + kt[p], 0)))
        args.append(extra)
    elif mode == "mask":
        in_specs.append(pl.BlockSpec((None, t, t), lambda bi, p, qt, kt: (bi, kt[p], qt[p])))
        args.append(extra)
    grid_spec = pltpu.PrefetchScalarGridSpec(
        num_scalar_prefetch=2,
        grid=(b, n * (n + 1) // 2),
        in_specs=in_specs,
        out_specs=pl.BlockSpec((t, wv), lambda bi, p, qt, kt: (bi * n + qt[p], 0)),
        scratch_shapes=[
            pltpu.VMEM((N_HEADS, t), F32),
            pltpu.VMEM((N_HEADS, t), F32),
            pltpu.VMEM((wv, t), F32),
        ],
    )
    return pl.pallas_call(
        functools.partial(_flash_kernel, mode=mode, dk=dk, t=t),
        grid_spec=grid_spec,
        out_shape=jax.ShapeDtypeStruct((b * s, wv), BF16),
        compiler_params=_params("parallel", "arbitrary"),
        name="flash_" + mode,
    )(qt, kt, *args)


def _count(sc_ref, nvalid, cand, tk, tq, ind_fn):
    def body(c, a):
        ind = ind_fn(sc_ref[c], cand, c)
        return a + jnp.sum(ind.reshape(tk // 8, 8, tq), axis=0)

    part = lax.fori_loop(0, nvalid, body, jnp.zeros((8, tq), jnp.int32))
    return jnp.sum(part, axis=0, keepdims=True)


def _indexer_kernel(q_ref, wt_ref, k2_ref, mask_ref, sc_ref, qm_scr, j_scr, *, tq, tk, n_sel, idx_bits):
    qi = pl.program_id(1)
    one, zero = jnp.int32(1), jnp.int32(0)
    nchunks = sc_ref.shape[0]
    nvalid = ((qi + 1) * tq + tk - 1) // tk
    qcol = qi * tq + lax.broadcasted_iota(jnp.int32, (1, tq), 1)
    krow0 = lax.broadcasted_iota(jnp.int32, (tk, 1), 0)

    half = lax.broadcasted_iota(jnp.int32, (tq, LANES), 1) // IDX_DIM
    for g in range(IDX_HEADS // 2):
        qp = q_ref[:, g * LANES:(g + 1) * LANES]
        for e in range(2):
            qm_scr[2 * g + e] = jnp.where(half == e, qp, jnp.zeros_like(qp))

    def score_chunk(c, carry):
        kc = k2_ref[pl.ds(pl.multiple_of(c * tk, tk), tk), :]
        acc = jnp.zeros((tk, tq), F32)
        for h in range(IDX_HEADS):
            x = lax.dot_general(kc, qm_scr[h], (((1,), (1,)), ((), ())), preferred_element_type=F32)
            acc = acc + wt_ref[h:h + 1, :] * jnp.maximum(x, 0.0)
        score = acc * (IDX_HEADS * IDX_DIM) ** -0.5
        bits = pltpu.bitcast(score, jnp.int32)
        key = bits ^ ((bits >> 31) & 0x7FFFFFFF)
        causal = (c * tk + krow0) <= qcol
        sc_ref[c] = jnp.where(causal, key, INT_MIN)
        return carry

    lax.fori_loop(0, nvalid, score_chunk, 0)

    def bit_body(i, carry):
        u, cge = carry
        bit = lax.shift_left(jnp.int32(1), 31 - i)
        cand = (u | bit) ^ INT_MIN
        cnt = _count(sc_ref, nvalid, cand, tk, tq, lambda k, cd, c: jnp.where(k >= cd, one, zero))
        ok = cnt >= n_sel
        return jnp.where(ok, u | bit, u), jnp.where(ok, cnt, cge)

    u0 = jnp.zeros((1, tq), jnp.int32)
    u, cge = lax.fori_loop(0, 32, bit_body, (u0, jnp.full((1, tq), n_sel, jnp.int32)))
    found = u != 0
    thr = jnp.where(found, u ^ INT_MIN, INT_MIN + 1)

    j_scr[...] = jnp.full_like(j_scr, 2 ** idx_bits - 1)
    has_tie = jnp.max(jnp.where(found & (cge > n_sel), 1, 0)) > 0

    @pl.when(has_tie)
    def _():
        cgt = _count(sc_ref, nvalid, thr, tk, tq, lambda k, cd, c: jnp.where(k > cd, one, zero))
        need = n_sel - cgt

        def jbit(i, jv):
            cand = jv | lax.shift_left(jnp.int32(1), idx_bits - 1 - i)
            cnt = _count(sc_ref, nvalid, cand, tk, tq,
                         lambda k, cd, c: jnp.where(k == thr, jnp.where((c * tk + krow0) < cd, one, zero), zero))
            return jnp.where(cnt < need, cand, jv)

        j_scr[...] = lax.fori_loop(0, idx_bits, jbit, jnp.zeros((1, tq), jnp.int32))

    jlast = j_scr[...]

    def write_chunk(c, carry):
        key = sc_ref[c]
        sel = jnp.where(key > thr, one, jnp.where(key == thr, jnp.where((c * tk + krow0) <= jlast, one, zero), zero))
        mask_ref[pl.ds(pl.multiple_of(c * tk, tk), tk), :] = sel.astype(jnp.int8)
        return carry

    lax.fori_loop(0, nvalid, write_chunk, 0)

    def zero_chunk(c, carry):
        mask_ref[pl.ds(pl.multiple_of(c * tk, tk), tk), :] = jnp.zeros((tk, tq), jnp.int8)
        return carry

    lax.fori_loop(nvalid, nchunks, zero_chunk, 0)


def _indexer_mask(zb, wt, k2, b, s, tq, tk, n_sel):
    nq = s // tq
    idx_bits = max(1, int(np.ceil(np.log2(s))))
    return pl.pallas_call(
        functools.partial(_indexer_kernel, tq=tq, tk=tk, n_sel=n_sel, idx_bits=idx_bits),
        grid=(b, nq),
        in_specs=[
            pl.BlockSpec((tq, IDX_HEADS * IDX_DIM), lambda bi, i: (bi * nq + i, COL_IQ // (IDX_HEADS * IDX_DIM))),
            pl.BlockSpec((None, IDX_HEADS, tq), lambda bi, i: (bi, 0, i)),
            pl.BlockSpec((None, s, LANES), lambda bi, i: (bi, 0, 0)),
        ],
        out_specs=pl.BlockSpec((None, s, tq), lambda bi, i: (bi, 0, i)),
        out_shape=jax.ShapeDtypeStruct((b, s, s), jnp.int8),
        scratch_shapes=[
            pltpu.VMEM((s // tk, tk, tq), jnp.int32),
            pltpu.VMEM((IDX_HEADS, tq, LANES), BF16),
            pltpu.VMEM((1, tq), jnp.int32),
        ],
        compiler_params=_params("parallel", "arbitrary"),
        name="indexer_mask",
    )(zb, wt, k2)


def _mla_up_kernel(c_ref, zs_ref, qg_ref, kvg_ref, wq_ref, wk_ref, wv_ref, c64_ref, s64l_ref, s64h_ref,
                   q_ref, k_ref, v_ref, *, q_scale):
    def norm(xf, g):
        y = xf * lax.rsqrt(jnp.mean(xf * xf, axis=-1, keepdims=True) + EPS)
        return (y * g).astype(BF16)

    cq = norm(c_ref[:, :Q_LORA].astype(F32), qg_ref[...])
    ckv = norm(c_ref[:, Q_LORA:].astype(F32), kvg_ref[...])
    q = jnp.dot(cq, wq_ref[...], preferred_element_type=F32) * q_scale
    k = jnp.dot(ckv, wk_ref[...], preferred_element_type=F32)
    v_ref[...] = jnp.dot(ckv, wv_ref[...], preferred_element_type=F32).astype(BF16)
    lane = lax.broadcasted_iota(jnp.int32, (1, LANES), 1)
    k_rot = jnp.where(lane < QK_ROPE, zs_ref[:, LANES:], 0.0)
    cos, sin_lo, sin_hi = c64_ref[...], s64l_ref[...], s64h_ref[...]
    for h in range(N_HEADS):
        nope = slice(h * MLA_QK_PAD, h * MLA_QK_PAD + QK_NOPE)
        rope = slice(h * MLA_QK_PAD + QK_NOPE, (h + 1) * MLA_QK_PAD)
        q_ref[:, nope] = q[:, nope].astype(BF16)
        q_ref[:, rope] = _rope64(q[:, rope], cos, sin_lo, sin_hi).astype(BF16)
        k_ref[:, nope] = k[:, nope].astype(BF16)
        k_ref[:, rope] = k_rot.astype(BF16)


def _mla_up(zb, zs, qg, kvg, wq, wk, wv, tabs, tm):
    m = zb.shape[0]
    _, _, c64, s64l, s64h = tabs
    nsb = c64.shape[0] // tm
    tab_spec = pl.BlockSpec((tm, LANES), lambda i: (i % nsb, 0))
    wqk = N_HEADS * MLA_QK_PAD
    full = lambda a: pl.BlockSpec(a.shape, lambda i: (0,) * a.ndim)
    return pl.pallas_call(
        functools.partial(_mla_up_kernel, q_scale=(QK_NOPE + QK_ROPE) ** -0.5),
        grid=(m // tm,),
        in_specs=[
            pl.BlockSpec((tm, Q_LORA + KV_LORA), lambda i: (i, COL_CQ // (Q_LORA + KV_LORA))),
            pl.BlockSpec((tm, 2 * LANES), lambda i: (i, 0)),
            full(qg), full(kvg), full(wq), full(wk), full(wv),
            tab_spec, tab_spec, tab_spec,
        ],
        out_specs=[
            pl.BlockSpec((tm, wqk), lambda i: (i, 0)),
            pl.BlockSpec((tm, wqk), lambda i: (i, 0)),
            pl.BlockSpec((tm, BRANCH_WIDTH), lambda i: (i, 0)),
        ],
        out_shape=[
            jax.ShapeDtypeStruct((m, wqk), BF16),
            jax.ShapeDtypeStruct((m, wqk), BF16),
            jax.ShapeDtypeStruct((m, BRANCH_WIDTH), BF16),
        ],
        compiler_params=_params("parallel"),
        name="mla_up",
    )(zb, zs, qg, kvg, wq, wk, wv, c64, s64l, s64h)


def _branch_kernel(o0_ref, o1_ref, o2_ref, g0_ref, g1_ref, g2_ref, m0_ref, m1_ref, m2_ref, w_ref, y_ref):
    y = None
    for i, (o_ref, g_ref, m_ref) in enumerate(((o0_ref, g0_ref, m0_ref), (o1_ref, g1_ref, m1_ref),
                                               (o2_ref, g2_ref, m2_ref))):
        g = g_ref[...].astype(F32)
        a = (o_ref[...].astype(F32) * (g * jax.nn.sigmoid(g))).astype(BF16)
        term = jax.nn.sigmoid(m_ref[...].astype(F32)) * jnp.dot(a, w_ref[i], preferred_element_type=F32)
        y = term if y is None else y + term
    y_ref[...] = y.astype(BF16)


def _branch_merge(o_fox, o_dsa, o_mla, zb, wb, d, tm, tn):
    m = zb.shape[0]
    o_spec = pl.BlockSpec((tm, BRANCH_WIDTH), lambda i, j: (i, 0))
    silu0 = (COL_MERGE + N_BRANCH * d) // BRANCH_WIDTH

    def g_spec(i_br):
        return pl.BlockSpec((tm, BRANCH_WIDTH), lambda i, j: (i, silu0 + i_br))

    def m_spec(i_br):
        base = (COL_MERGE + i_br * d) // tn
        return pl.BlockSpec((tm, tn), lambda i, j: (i, base + j))

    return pl.pallas_call(
        _branch_kernel,
        grid=(m // tm, d // tn),
        in_specs=[o_spec, o_spec, o_spec, g_spec(0), g_spec(1), g_spec(2), m_spec(0), m_spec(1), m_spec(2),
                  pl.BlockSpec((N_BRANCH, BRANCH_WIDTH, tn), lambda i, j: (0, 0, j))],
        out_specs=pl.BlockSpec((tm, tn), lambda i, j: (i, j)),
        out_shape=jax.ShapeDtypeStruct((m, d), BF16),
        compiler_params=_params("parallel", "arbitrary"),
        name="branch_merge",
    )(o_fox, o_dsa, o_mla, zb, zb, zb, zb, zb, zb, wb)


def _out_kernel(x_ref, y_ref, w_ref, g_ref, o_ref, *, final_norm):
    r = x_ref[...] + jnp.dot(y_ref[...], w_ref[...], preferred_element_type=F32)
    if final_norm:
        r = r * lax.rsqrt(jnp.mean(r * r, axis=-1, keepdims=True) + EPS) * g_ref[...]
    o_ref[...] = r


def _out_proj(x2, y, w, g, tm, final_norm):
    m, d = x2.shape
    return pl.pallas_call(
        functools.partial(_out_kernel, final_norm=final_norm),
        grid=(m // tm,),
        in_specs=[
            pl.BlockSpec((tm, d), lambda i: (i, 0)),
            pl.BlockSpec((tm, d), lambda i: (i, 0)),
            pl.BlockSpec((d, d), lambda i: (0, 0)),
            pl.BlockSpec((1, d), lambda i: (0, 0)),
        ],
        out_specs=pl.BlockSpec((tm, d), lambda i: (i, 0)),
        out_shape=jax.ShapeDtypeStruct((m, d), F32),
        compiler_params=_params("parallel"),
        name="out_proj",
    )(x2, y, w, g)


def _split_w_in(w_in, d):
    sizes = (BRANCH_WIDTH, BRANCH_WIDTH, BRANCH_WIDTH, N_HEADS, BRANCH_WIDTH, BRANCH_WIDTH, BRANCH_WIDTH,
             IDX_HEADS * IDX_DIM, IDX_DIM, IDX_HEADS, Q_LORA, KV_LORA, QK_ROPE,
             N_BRANCH * BRANCH_WIDTH, N_BRANCH * d)
    pts = np.cumsum(sizes)[:-1].tolist()
    (f_q, f_k, f_v, f_gate, d_q, d_k, d_v, i_q, i_k, i_w, c_q, c_kv, k_pe, silu_g, merge_g) = jnp.split(
        w_in, pts, axis=-1)
    big = jnp.concatenate([f_q, f_k, f_v, d_q, d_k, d_v, i_q, c_q, c_kv, merge_g, silu_g], axis=-1).astype(BF16)
    pad = jnp.zeros((w_in.shape[0], LANES - N_HEADS - IDX_HEADS), w_in.dtype)
    small = jnp.concatenate([f_gate, i_w, pad, k_pe, i_k], axis=-1).astype(BF16)
    return big, small


def _pad_heads(w, take, width_in):
    r = w.shape[0]
    w3 = w.reshape(r, N_HEADS, width_in)[:, :, take]
    w3 = jnp.pad(w3, ((0, 0), (0, 0), (0, MLA_QK_PAD - w3.shape[-1])))
    return w3.reshape(r, N_HEADS * MLA_QK_PAD)


def _tile(s, pref):
    return pref if s % pref == 0 else s


def kernel(x, norm_g, w_in, forget_b, q_norm_g, w_q_up, kv_norm_g, w_kv_up, w_branch, w_out, final_norm_g):
    b, s, d = x.shape
    depth = w_in.shape[0]
    n_sel = min(TOPK_MAX, s // 4)
    m = b * s
    tabs = _rope_tables(s)
    t_att = _tile(s, 512)
    tm_in = _tile(s, 1024)
    x2 = x.reshape(m, d)
    for l in range(depth):
        w_big, w_small = _split_w_in(w_in[l], d)
        zb, zs = _in_proj(x2, norm_g[l].reshape(1, d), w_big, w_small, tabs, tm_in, 512)

        cum = _forget_cumsum(zs, forget_b[l].reshape(1, N_HEADS), b, s, _tile(s, 1024))
        o_fox = _flash(zb, COL_FQ // BRANCH_WIDTH, zb, COL_FK // BRANCH_WIDTH, zb, COL_FV // BRANCH_WIDTH,
                       b, s, t_att, HEAD_DIM, "fox", cum)

        zs3 = zs.reshape(b, s, 2 * LANES)
        wt = jnp.swapaxes(zs3[:, :, N_HEADS:N_HEADS + IDX_HEADS], 1, 2)
        ik = zs3[:, :, LANES + QK_ROPE:].astype(BF16)
        k2 = jnp.concatenate([ik, ik], axis=-1)
        mask = _indexer_mask(zb, wt, k2, b, s, _tile(s, 256), _tile(s, 512), n_sel)
        o_dsa = _flash(zb, COL_DQ // BRANCH_WIDTH, zb, COL_DK // BRANCH_WIDTH, zb, COL_DV // BRANCH_WIDTH,
                       b, s, t_att, HEAD_DIM, "mask", mask)

        wq = _pad_heads(w_q_up[l], slice(0, QK_NOPE + QK_ROPE), QK_NOPE + QK_ROPE).astype(BF16)
        wk = _pad_heads(w_kv_up[l], slice(0, QK_NOPE), QK_NOPE + V_DIM).astype(BF16)
        wv = w_kv_up[l].reshape(KV_LORA, N_HEADS, QK_NOPE + V_DIM)[:, :, QK_NOPE:].reshape(
            KV_LORA, BRANCH_WIDTH).astype(BF16)
        q_mla, k_mla, v_mla = _mla_up(zb, zs, q_norm_g[l].reshape(1, Q_LORA), kv_norm_g[l].reshape(1, KV_LORA),
                                      wq, wk, wv, tabs, _tile(s, 512))
        o_mla = _flash(q_mla, 0, k_mla, 0, v_mla, 0, b, s, t_att, MLA_QK_PAD, "plain")

        y = _branch_merge(o_fox, o_dsa, o_mla, zb, w_branch[l].astype(BF16), d, _tile(s, 512), 1024)
        last = l == depth - 1
        x2 = _out_proj(x2, y, w_out[l].astype(BF16), final_norm_g.reshape(1, d), _tile(s, 512), last)
    return x2.reshape(b, s, d)
```

```python
import functools

import jax
import jax.numpy as jnp
import numpy as np
from jax import lax
from jax.experimental import pallas as pl
from jax.experimental.pallas import tpu as pltpu

F32 = jnp.float32
BF16 = jnp.bfloat16

HEAD_DIM = 128
N_HEADS = 8
BRANCH_WIDTH = N_HEADS * HEAD_DIM
N_BRANCH = 3
IDX_HEADS = 16
IDX_DIM = 64
TOPK_MAX = 256
Q_LORA = 512
KV_LORA = 512
QK_NOPE = 128
QK_ROPE = 64
V_DIM = 128
MLA_QK_PAD = 256
ROPE_THETA = 10000.0
EPS = 1e-6

LANES = 128
VMEM_LIMIT = 56 * 1024 * 1024
NEG_BIG = -1e30
INT_MIN = -(2 ** 31)
LOG2E = 1.4426950408889634
Q_STRIP = 256
PIPE_LEAD = 2
PIPE_LAG = 2

COL_FQ, COL_FK, COL_FV = 0, 1024, 2048
COL_DQ, COL_DK, COL_DV = 3072, 4096, 5120
COL_IQ = 6144
COL_CQ = 7168
COL_MERGE = 8192


def _params(*sem):
    return pltpu.CompilerParams(dimension_semantics=sem, vmem_limit_bytes=VMEM_LIMIT)


def _rope_tables(s):
    pos = jnp.arange(s, dtype=F32)

    def cs(half):
        inv = ROPE_THETA ** (-jnp.arange(half, dtype=F32) / half)
        ang = pos[:, None] * inv[None, :]
        return jnp.cos(ang), jnp.sin(ang)

    c64, s64 = cs(64)
    c32, s32 = cs(32)
    z32 = jnp.zeros_like(s32)
    cos128 = jnp.concatenate([c64, c64], axis=-1)
    sin128 = jnp.concatenate([-s64, s64], axis=-1)
    cos64 = jnp.concatenate([c32, c32, c32, c32], axis=-1)
    sin64_lo = jnp.concatenate([-s32, z32, -s32, z32], axis=-1)
    sin64_hi = jnp.concatenate([z32, s32, z32, s32], axis=-1)
    return cos128, sin128, cos64, sin64_lo, sin64_hi


def _rope128(x, cos, sin):
    return x * cos + pltpu.roll(x, 64, 1) * sin


def _rope64(x, cos, sin_lo, sin_hi):
    return x * cos + pltpu.roll(x, 96, 1) * sin_lo + pltpu.roll(x, 32, 1) * sin_hi


def _in_proj_kernel(x_ref, g_ref, w_ref, ws_ref, c128_ref, s128_ref, c64_ref, s64l_ref, s64h_ref,
                    zb_ref, zs_ref, h_scr, *, tn, q_scale):
    j = pl.program_id(1)

    @pl.when(j == 0)
    def _():
        xf = x_ref[...]
        y = xf * lax.rsqrt(jnp.mean(xf * xf, axis=-1, keepdims=True) + EPS)
        h = (y * g_ref[...]).astype(BF16)
        h_scr[...] = h
        zs = jnp.dot(h, ws_ref[...], preferred_element_type=F32)
        zs_ref[:, :LANES] = zs[:, :LANES]
        zs_ref[:, LANES:] = _rope64(zs[:, LANES:], c64_ref[...], s64l_ref[...], s64h_ref[...])

    acc = jnp.dot(h_scr[...], w_ref[...], preferred_element_type=F32)
    col = j * tn
    is_q = ((col >= COL_FQ) & (col < COL_FK)) | ((col >= COL_DQ) & (col < COL_DK))
    acc = acc * jnp.where(is_q, q_scale, 1.0).astype(F32)
    is_r128 = (col >= COL_DQ) & (col < COL_DV)
    is_r64 = (col >= COL_IQ) & (col < COL_CQ)

    @pl.when(is_r128)
    def _():
        for g in range(tn // LANES):
            sl = slice(g * LANES, (g + 1) * LANES)
            zb_ref[:, sl] = _rope128(acc[:, sl], c128_ref[...], s128_ref[...]).astype(BF16)

    @pl.when(is_r64)
    def _():
        for g in range(tn // LANES):
            sl = slice(g * LANES, (g + 1) * LANES)
            zb_ref[:, sl] = _rope64(acc[:, sl], c64_ref[...], s64l_ref[...], s64h_ref[...]).astype(BF16)

    @pl.when(jnp.logical_not(is_r128 | is_r64))
    def _():
        zb_ref[...] = acc.astype(BF16)


def _in_proj(x2, g, w_big, w_small, tabs, tm, tn):
    m, d = x2.shape
    n = w_big.shape[1]
    c128, s128, c64, s64l, s64h = tabs
    s = c128.shape[0]
    nsb = s // tm
    tab_spec = pl.BlockSpec((tm, LANES), lambda i, j: (i % nsb, 0))
    return pl.pallas_call(
        functools.partial(_in_proj_kernel, tn=tn, q_scale=HEAD_DIM ** -0.5 * LOG2E),
        grid=(m // tm, n // tn),
        in_specs=[
            pl.BlockSpec((tm, d), lambda i, j: (i, 0)),
            pl.BlockSpec((1, d), lambda i, j: (0, 0)),
            pl.BlockSpec((d, tn), lambda i, j: (0, j)),
            pl.BlockSpec((d, 2 * LANES), lambda i, j: (0, 0)),
            tab_spec, tab_spec, tab_spec, tab_spec, tab_spec,
        ],
        out_specs=[
            pl.BlockSpec((tm, tn), lambda i, j: (i, j)),
            pl.BlockSpec((tm, 2 * LANES), lambda i, j: (i, 0)),
        ],
        out_shape=[
            jax.ShapeDtypeStruct((m, n), BF16),
            jax.ShapeDtypeStruct((m, 2 * LANES), F32),
        ],
        scratch_shapes=[pltpu.VMEM((tm, d), BF16)],
        compiler_params=_params("parallel", "arbitrary"),
        name="in_proj",
    )(x2, g, w_big, w_small, c128, s128, c64, s64l, s64h)


def _cum_kernel(zs_ref, fb_ref, cum_ref, carry_scr, *, t):
    @pl.when(pl.program_id(1) == 0)
    def _():
        carry_scr[...] = jnp.zeros_like(carry_scr)

    x = zs_ref[:, :N_HEADS] + fb_ref[...]
    logf = -(jnp.maximum(-x, 0.0) + jnp.log1p(jnp.exp(-jnp.abs(x))))
    r = lax.broadcasted_iota(jnp.int32, (t, t), 0)
    c = lax.broadcasted_iota(jnp.int32, (t, t), 1)
    tri = jnp.where(c <= r, 1.0, 0.0).astype(F32)
    cum = jnp.dot(tri, logf, preferred_element_type=F32, precision=lax.Precision.HIGHEST) + carry_scr[...]
    cum_ref[...] = cum * LOG2E
    carry_scr[...] = cum[t - 1:t, :]


def _forget_cumsum(zs, fb, b, s, t):
    return pl.pallas_call(
        functools.partial(_cum_kernel, t=t),
        grid=(b, s // t),
        in_specs=[
            pl.BlockSpec((t, 2 * LANES), lambda bi, i: (bi * (s // t) + i, 0)),
            pl.BlockSpec((1, N_HEADS), lambda bi, i: (0, 0)),
        ],
        out_specs=pl.BlockSpec((t, N_HEADS), lambda bi, i: (bi * (s // t) + i, 0)),
        out_shape=jax.ShapeDtypeStruct((b * s, N_HEADS), F32),
        scratch_shapes=[pltpu.VMEM((1, N_HEADS), F32)],
        compiler_params=_params("parallel", "arbitrary"),
        name="forget_cumsum",
    )(zs, fb)


def _flash_kernel(qt_ref, kt_ref, q_ref, k_ref, v_ref, *rest, mode, dk, t):
    if mode == "plain":
        o_ref, m_scr, l_scr, acc_scr = rest
        extra_ref = None
    else:
        extra_ref, o_ref, m_scr, l_scr, acc_scr = rest
    p = pl.program_id(1)
    qi = qt_ref[p]
    ki = kt_ref[p]
    qs = min(t, Q_STRIP)

    @pl.when(ki == 0)
    def _():
        m_scr[...] = jnp.full_like(m_scr, NEG_BIG)
        l_scr[...] = jnp.zeros_like(l_scr)
        acc_scr[...] = jnp.zeros_like(acc_scr)

    def step(diag):
        units = [(c, h) for c in range(t // qs) for h in range(N_HEADS)]

        def n_keys(c):
            return (c + 1) * qs if diag else t

        def logits(c, h):
            k_h = k_ref[:n_keys(c), h * dk:(h + 1) * dk]
            q_h = q_ref[c * qs:(c + 1) * qs, h * dk:(h + 1) * dk]
            return lax.dot_general(k_h, q_h, (((1,), (1,)), ((), ())), preferred_element_type=F32)

        def keep_mask(c):
            nk = n_keys(c)
            if mode == "mask":
                return extra_ref[:, c * qs:(c + 1) * qs].astype(jnp.int32) != 0
            if diag:
                kr = lax.broadcasted_iota(jnp.int32, (nk, qs), 0)
                qc = lax.broadcasted_iota(jnp.int32, (nk, qs), 1) + c * qs
                return kr <= qc
            return None

        keeps = {}

        def softmax(c, h, st):
            cs = slice(c * qs, (c + 1) * qs)
            if c not in keeps:
                keeps[c] = keep_mask(c)
            if mode == "fox":
                st = st - extra_ref[:n_keys(c), h:h + 1]
            if keeps[c] is not None:
                st = jnp.where(keeps[c], st, NEG_BIG)
            m_prev = m_scr[h:h + 1, cs]
            m_new = jnp.maximum(m_prev, jnp.max(st, axis=0, keepdims=True))
            alpha = jnp.exp2(m_prev - m_new)
            pt = jnp.exp2(st - m_new)
            l_scr[h:h + 1, cs] = alpha * l_scr[h:h + 1, cs] + jnp.sum(pt, axis=0, keepdims=True)
            m_scr[h:h + 1, cs] = m_new
            return alpha, pt.astype(BF16)

        def accumulate(c, h, alpha, pt):
            cs = slice(c * qs, (c + 1) * qs)
            vt_h = v_ref[h * HEAD_DIM:(h + 1) * HEAD_DIM, :n_keys(c)]
            pv = jnp.dot(vt_h, pt, preferred_element_type=F32)
            sl = slice(h * HEAD_DIM, (h + 1) * HEAD_DIM)
            acc_scr[sl, cs] = alpha * acc_scr[sl, cs] + pv

        n_u = len(units)
        st_q = [logits(*units[i]) for i in range(min(PIPE_LEAD, n_u))]
        p_q = []
        for u in range(n_u + PIPE_LAG):
            if u + PIPE_LEAD < n_u:
                st_q.append(logits(*units[u + PIPE_LEAD]))
            if u < n_u:
                p_q.append(softmax(*units[u], st_q.pop(0)))
            if u >= PIPE_LAG:
                accumulate(*units[u - PIPE_LAG], *p_q.pop(0))

    if mode == "mask":
        step(False)
    else:
        @pl.when(ki == qi)
        def _():
            step(True)

        @pl.when(ki != qi)
        def _():
            step(False)

    @pl.when(ki == qi)
    def _():
        for h in range(N_HEADS):
            sl = slice(h * HEAD_DIM, (h + 1) * HEAD_DIM)
            o = acc_scr[sl, :] / l_scr[h:h + 1, :]
            o_ref[:, sl] = o.T.astype(BF16)


def _pair_tables(n):
    qt = np.concatenate([np.full(i + 1, i, np.int32) for i in range(n)])
    kt = np.concatenate([np.arange(i + 1, dtype=np.int32) for i in range(n)])
    return jnp.asarray(qt), jnp.asarray(kt)


def _flash(q_arr, q_col, k_arr, k_col, vt_arr, b, s, t, dk, mode, extra=None):
    n = s // t
    qt, kt = _pair_tables(n)
    wq = N_HEADS * dk
    wv = N_HEADS * HEAD_DIM
    in_specs = [
        pl.BlockSpec((t, wq), lambda bi, p, qt, kt: (bi * n + qt[p], q_col)),
        pl.BlockSpec((t, wq), lambda bi, p, qt, kt: (bi * n + kt[p], k_col)),
        pl.BlockSpec((None, wv, t), lambda bi, p, qt, kt: (bi, 0, kt[p])),
    ]
    args = [q_arr, k_arr, vt_arr]
    if mode == "fox":
        in_specs.append(pl.BlockSpec((t, N_HEADS), lambda bi, p, qt, kt: (bi * n + kt[p], 0)))
        args.append(extra)
    elif mode == "mask":
        in_specs.append(pl.BlockSpec((None, t, t), lambda bi, p, qt, kt: (bi, kt[p], qt[p])))
        args.append(extra)
    grid_spec = pltpu.PrefetchScalarGridSpec(
        num_scalar_prefetch=2,
        grid=(b, n * (n + 1) // 2),
        in_specs=in_specs,
        out_specs=pl.BlockSpec((t, wv), lambda bi, p, qt, kt: (bi * n + qt[p], 0)),
        scratch_shapes=[
            pltpu.VMEM((N_HEADS, t), F32),
            pltpu.VMEM((N_HEADS, t), F32),
            pltpu.VMEM((wv, t), F32),
        ],
    )
    return pl.pallas_call(
        functools.partial(_flash_kernel, mode=mode, dk=dk, t=t),
        grid_spec=grid_spec,
        out_shape=jax.ShapeDtypeStruct((b * s, wv), BF16),
        compiler_params=_params("parallel", "arbitrary"),
        name="flash_" + mode,
    )(qt, kt, *args)


def _count(sc_ref, nvalid, cand, tk, tq, ind_fn):
    def body(c, a):
        ind = ind_fn(sc_ref[c], cand, c)
        return a + jnp.sum(ind.reshape(tk // 8, 8, tq), axis=0)

    part = lax.fori_loop(0, nvalid, body, jnp.zeros((8, tq), jnp.int32))
    return jnp.sum(part, axis=0, keepdims=True)


def _indexer_kernel(q_ref, wt_ref, k2_ref, mask_ref, sc_ref, qm_scr, j_scr, *, tq, tk, n_sel, idx_bits):
    qi = pl.program_id(1)
    one, zero = jnp.int32(1), jnp.int32(0)
    nchunks = sc_ref.shape[0]
    nvalid = ((qi + 1) * tq + tk - 1) // tk
    qcol = qi * tq + lax.broadcasted_iota(jnp.int32, (1, tq), 1)
    krow0 = lax.broadcasted_iota(jnp.int32, (tk, 1), 0)

    half = lax.broadcasted_iota(jnp.int32, (tq, LANES), 1) // IDX_DIM
    for g in range(IDX_HEADS // 2):
        qp = q_ref[:, g * LANES:(g + 1) * LANES]
        for e in range(2):
            qm_scr[2 * g + e] = jnp.where(half == e, qp, jnp.zeros_like(qp))

    def score_chunk(c, carry):
        kc = k2_ref[pl.ds(pl.multiple_of(c * tk, tk), tk), :]
        acc = jnp.zeros((tk, tq), F32)
        for h in range(IDX_HEADS):
            x = lax.dot_general(kc, qm_scr[h], (((1,), (1,)), ((), ())), preferred_element_type=F32)
            acc = acc + wt_ref[h:h + 1, :] * jnp.maximum(x, 0.0)
        score = acc * (IDX_HEADS * IDX_DIM) ** -0.5
        bits = pltpu.bitcast(score, jnp.int32)
        key = bits ^ ((bits >> 31) & 0x7FFFFFFF)
        causal = (c * tk + krow0) <= qcol
        sc_ref[c] = jnp.where(causal, key, INT_MIN)
        return carry

    lax.fori_loop(0, nvalid, score_chunk, 0)

    def bit_body(i, carry):
        u, cge = carry
        bit = lax.shift_left(jnp.int32(1), 31 - i)
        cand = (u | bit) ^ INT_MIN
        cnt = _count(sc_ref, nvalid, cand, tk, tq, lambda k, cd, c: jnp.where(k >= cd, one, zero))
        ok = cnt >= n_sel
        return jnp.where(ok, u | bit, u), jnp.where(ok, cnt, cge)

    u0 = jnp.zeros((1, tq), jnp.int32)
    u, cge = lax.fori_loop(0, 32, bit_body, (u0, jnp.full((1, tq), n_sel, jnp.int32)))
    found = u != 0
    thr = jnp.where(found, u ^ INT_MIN, INT_MIN + 1)

    j_scr[...] = jnp.full_like(j_scr, 2 ** idx_bits - 1)
    has_tie = jnp.max(jnp.where(found & (cge > n_sel), 1, 0)) > 0

    @pl.when(has_tie)
    def _():
        cgt = _count(sc_ref, nvalid, thr, tk, tq, lambda k, cd, c: jnp.where(k > cd, one, zero))
        need = n_sel - cgt

        def jbit(i, jv):
            cand = jv | lax.shift_left(jnp.int32(1), idx_bits - 1 - i)
            cnt = _count(sc_ref, nvalid, cand, tk, tq,
                         lambda k, cd, c: jnp.where(k == thr, jnp.where((c * tk + krow0) < cd, one, zero), zero))
            return jnp.where(cnt < need, cand, jv)

        j_scr[...] = lax.fori_loop(0, idx_bits, jbit, jnp.zeros((1, tq), jnp.int32))

    jlast = j_scr[...]

    def write_chunk(c, carry):
        key = sc_ref[c]
        sel = jnp.where(key > thr, one, jnp.where(key == thr, jnp.where((c * tk + krow0) <= jlast, one, zero), zero))
        mask_ref[pl.ds(pl.multiple_of(c * tk, tk), tk), :] = sel.astype(jnp.int8)
        return carry

    lax.fori_loop(0, nvalid, write_chunk, 0)

    def zero_chunk(c, carry):
        mask_ref[pl.ds(pl.multiple_of(c * tk, tk), tk), :] = jnp.zeros((tk, tq), jnp.int8)
        return carry

    lax.fori_loop(nvalid, nchunks, zero_chunk, 0)


def _indexer_mask(zb, wt, k2, b, s, tq, tk, n_sel):
    nq = s // tq
    idx_bits = max(1, int(np.ceil(np.log2(s))))
    return pl.pallas_call(
        functools.partial(_indexer_kernel, tq=tq, tk=tk, n_sel=n_sel, idx_bits=idx_bits),
        grid=(b, nq),
        in_specs=[
            pl.BlockSpec((tq, IDX_HEADS * IDX_DIM), lambda bi, i: (bi * nq + i, COL_IQ // (IDX_HEADS * IDX_DIM))),
            pl.BlockSpec((None, IDX_HEADS, tq), lambda bi, i: (bi, 0, i)),
            pl.BlockSpec((None, s, LANES), lambda bi, i: (bi, 0, 0)),
        ],
        out_specs=pl.BlockSpec((None, s, tq), lambda bi, i: (bi, 0, i)),
        out_shape=jax.ShapeDtypeStruct((b, s, s), jnp.int8),
        scratch_shapes=[
            pltpu.VMEM((s // tk, tk, tq), jnp.int32),
            pltpu.VMEM((IDX_HEADS, tq, LANES), BF16),
            pltpu.VMEM((1, tq), jnp.int32),
        ],
        compiler_params=_params("parallel", "arbitrary"),
        name="indexer_mask",
    )(zb, wt, k2)


def _mla_up_kernel(c_ref, zs_ref, qg_ref, kvg_ref, wq_ref, wk_ref, wv_ref, c64_ref, s64l_ref, s64h_ref,
                   q_ref, k_ref, v_ref, *, q_scale):
    def norm(xf, g):
        y = xf * lax.rsqrt(jnp.mean(xf * xf, axis=-1, keepdims=True) + EPS)
        return (y * g).astype(BF16)

    cq = norm(c_ref[:, :Q_LORA].astype(F32), qg_ref[...])
    ckv = norm(c_ref[:, Q_LORA:].astype(F32), kvg_ref[...])
    q = jnp.dot(cq, wq_ref[...], preferred_element_type=F32) * q_scale
    k = jnp.dot(ckv, wk_ref[...], preferred_element_type=F32)
    v_ref[...] = jnp.dot(ckv, wv_ref[...], preferred_element_type=F32).astype(BF16)
    lane = lax.broadcasted_iota(jnp.int32, (1, LANES), 1)
    k_rot = jnp.where(lane < QK_ROPE, zs_ref[:, LANES:], 0.0)
    cos, sin_lo, sin_hi = c64_ref[...], s64l_ref[...], s64h_ref[...]
    for h in range(N_HEADS):
        nope = slice(h * MLA_QK_PAD, h * MLA_QK_PAD + QK_NOPE)
        rope = slice(h * MLA_QK_PAD + QK_NOPE, (h + 1) * MLA_QK_PAD)
        q_ref[:, nope] = q[:, nope].astype(BF16)
        q_ref[:, rope] = _rope64(q[:, rope], cos, sin_lo, sin_hi).astype(BF16)
        k_ref[:, nope] = k[:, nope].astype(BF16)
        k_ref[:, rope] = k_rot.astype(BF16)


def _mla_up(zb, zs, qg, kvg, wq, wk, wv, tabs, tm):
    m = zb.shape[0]
    _, _, c64, s64l, s64h = tabs
    nsb = c64.shape[0] // tm
    tab_spec = pl.BlockSpec((tm, LANES), lambda i: (i % nsb, 0))
    wqk = N_HEADS * MLA_QK_PAD
    full = lambda a: pl.BlockSpec(a.shape, lambda i: (0,) * a.ndim)
    return pl.pallas_call(
        functools.partial(_mla_up_kernel, q_scale=(QK_NOPE + QK_ROPE) ** -0.5 * LOG2E),
        grid=(m // tm,),
        in_specs=[
            pl.BlockSpec((tm, Q_LORA + KV_LORA), lambda i: (i, COL_CQ // (Q_LORA + KV_LORA))),
            pl.BlockSpec((tm, 2 * LANES), lambda i: (i, 0)),
            full(qg), full(kvg), full(wq), full(wk), full(wv),
            tab_spec, tab_spec, tab_spec,
        ],
        out_specs=[
            pl.BlockSpec((tm, wqk), lambda i: (i, 0)),
            pl.BlockSpec((tm, wqk), lambda i: (i, 0)),
            pl.BlockSpec((tm, BRANCH_WIDTH), lambda i: (i, 0)),
        ],
        out_shape=[
            jax.ShapeDtypeStruct((m, wqk), BF16),
            jax.ShapeDtypeStruct((m, wqk), BF16),
            jax.ShapeDtypeStruct((m, BRANCH_WIDTH), BF16),
        ],
        compiler_params=_params("parallel"),
        name="mla_up",
    )(zb, zs, qg, kvg, wq, wk, wv, c64, s64l, s64h)


def _branch_kernel(o0_ref, o1_ref, o2_ref, g0_ref, g1_ref, g2_ref, m0_ref, m1_ref, m2_ref, w_ref, y_ref):
    y = None
    for i, (o_ref, g_ref, m_ref) in enumerate(((o0_ref, g0_ref, m0_ref), (o1_ref, g1_ref, m1_ref),
                                               (o2_ref, g2_ref, m2_ref))):
        g = g_ref[...].astype(F32)
        a = (o_ref[...].astype(F32) * (g * jax.nn.sigmoid(g))).astype(BF16)
        term = jax.nn.sigmoid(m_ref[...].astype(F32)) * jnp.dot(a, w_ref[i], preferred_element_type=F32)
        y = term if y is None else y + term
    y_ref[...] = y.astype(BF16)


def _branch_merge(o_fox, o_dsa, o_mla, zb, wb, d, tm, tn):
    m = zb.shape[0]
    o_spec = pl.BlockSpec((tm, BRANCH_WIDTH), lambda i, j: (i, 0))
    silu0 = (COL_MERGE + N_BRANCH * d) // BRANCH_WIDTH

    def g_spec(i_br):
        return pl.BlockSpec((tm, BRANCH_WIDTH), lambda i, j: (i, silu0 + i_br))

    def m_spec(i_br):
        base = (COL_MERGE + i_br * d) // tn
        return pl.BlockSpec((tm, tn), lambda i, j: (i, base + j))

    return pl.pallas_call(
        _branch_kernel,
        grid=(m // tm, d // tn),
        in_specs=[o_spec, o_spec, o_spec, g_spec(0), g_spec(1), g_spec(2), m_spec(0), m_spec(1), m_spec(2),
                  pl.BlockSpec((N_BRANCH, BRANCH_WIDTH, tn), lambda i, j: (0, 0, j))],
        out_specs=pl.BlockSpec((tm, tn), lambda i, j: (i, j)),
        out_shape=jax.ShapeDtypeStruct((m, d), BF16),
        compiler_params=_params("parallel", "arbitrary"),
        name="branch_merge",
    )(o_fox, o_dsa, o_mla, zb, zb, zb, zb, zb, zb, wb)


def _out_kernel(x_ref, y_ref, w_ref, g_ref, o_ref, *, final_norm):
    r = x_ref[...] + jnp.dot(y_ref[...], w_ref[...], preferred_element_type=F32)
    if final_norm:
        r = r * lax.rsqrt(jnp.mean(r * r, axis=-1, keepdims=True) + EPS) * g_ref[...]
    o_ref[...] = r


def _out_proj(x2, y, w, g, tm, final_norm):
    m, d = x2.shape
    return pl.pallas_call(
        functools.partial(_out_kernel, final_norm=final_norm),
        grid=(m // tm,),
        in_specs=[
            pl.BlockSpec((tm, d), lambda i: (i, 0)),
            pl.BlockSpec((tm, d), lambda i: (i, 0)),
            pl.BlockSpec((d, d), lambda i: (0, 0)),
            pl.BlockSpec((1, d), lambda i: (0, 0)),
        ],
        out_specs=pl.BlockSpec((tm, d), lambda i: (i, 0)),
        out_shape=jax.ShapeDtypeStruct((m, d), F32),
        compiler_params=_params("parallel"),
        name="out_proj",
    )(x2, y, w, g)


def _split_w_in(w_in, d):
    sizes = (BRANCH_WIDTH, BRANCH_WIDTH, BRANCH_WIDTH, N_HEADS, BRANCH_WIDTH, BRANCH_WIDTH, BRANCH_WIDTH,
             IDX_HEADS * IDX_DIM, IDX_DIM, IDX_HEADS, Q_LORA, KV_LORA, QK_ROPE,
             N_BRANCH * BRANCH_WIDTH, N_BRANCH * d)
    pts = np.cumsum(sizes)[:-1].tolist()
    (f_q, f_k, f_v, f_gate, d_q, d_k, d_v, i_q, i_k, i_w, c_q, c_kv, k_pe, silu_g, merge_g) = jnp.split(
        w_in, pts, axis=-1)
    big = jnp.concatenate([f_q, f_k, f_v, d_q, d_k, d_v, i_q, c_q, c_kv, merge_g, silu_g], axis=-1).astype(BF16)
    pad = jnp.zeros((w_in.shape[0], LANES - N_HEADS - IDX_HEADS), w_in.dtype)
    small = jnp.concatenate([f_gate, i_w, pad, k_pe, i_k], axis=-1).astype(BF16)
    return big, small


def _pad_heads(w, take, width_in):
    r = w.shape[0]
    w3 = w.reshape(r, N_HEADS, width_in)[:, :, take]
    w3 = jnp.pad(w3, ((0, 0), (0, 0), (0, MLA_QK_PAD - w3.shape[-1])))
    return w3.reshape(r, N_HEADS * MLA_QK_PAD)


def _tile(s, pref):
    return pref if s % pref == 0 else s


def _feature_major(a, col, b, s):
    return jnp.swapaxes(a[:, col:col + BRANCH_WIDTH].reshape(b, s, BRANCH_WIDTH), 1, 2)


def kernel(x, norm_g, w_in, forget_b, q_norm_g, w_q_up, kv_norm_g, w_kv_up, w_branch, w_out, final_norm_g):
    b, s, d = x.shape
    depth = w_in.shape[0]
    n_sel = min(TOPK_MAX, s // 4)
    m = b * s
    tabs = _rope_tables(s)
    t_att = _tile(s, 512)
    tm_in = _tile(s, 1024)
    x2 = x.reshape(m, d)
    for l in range(depth):
        w_big, w_small = _split_w_in(w_in[l], d)
        zb, zs = _in_proj(x2, norm_g[l].reshape(1, d), w_big, w_small, tabs, tm_in, 512)

        cum = _forget_cumsum(zs, forget_b[l].reshape(1, N_HEADS), b, s, _tile(s, 1024))
        o_fox = _flash(zb, COL_FQ // BRANCH_WIDTH, zb, COL_FK // BRANCH_WIDTH, _feature_major(zb, COL_FV, b, s),
                       b, s, t_att, HEAD_DIM, "fox", cum)

        zs3 = zs.reshape(b, s, 2 * LANES)
        wt = jnp.swapaxes(zs3[:, :, N_HEADS:N_HEADS + IDX_HEADS], 1, 2)
        ik = zs3[:, :, LANES + QK_ROPE:].astype(BF16)
        k2 = jnp.concatenate([ik, ik], axis=-1)
        mask = _indexer_mask(zb, wt, k2, b, s, _tile(s, 256), _tile(s, 512), n_sel)
        o_dsa = _flash(zb, COL_DQ // BRANCH_WIDTH, zb, COL_DK // BRANCH_WIDTH, _feature_major(zb, COL_DV, b, s),
                       b, s, t_att, HEAD_DIM, "mask", mask)

        wq = _pad_heads(w_q_up[l], slice(0, QK_NOPE + QK_ROPE), QK_NOPE + QK_ROPE).astype(BF16)
        wk = _pad_heads(w_kv_up[l], slice(0, QK_NOPE), QK_NOPE + V_DIM).astype(BF16)
        wv = w_kv_up[l].reshape(KV_LORA, N_HEADS, QK_NOPE + V_DIM)[:, :, QK_NOPE:].reshape(
            KV_LORA, BRANCH_WIDTH).astype(BF16)
        q_mla, k_mla, v_mla = _mla_up(zb, zs, q_norm_g[l].reshape(1, Q_LORA), kv_norm_g[l].reshape(1, KV_LORA),
                                      wq, wk, wv, tabs, _tile(s, 512))
        o_mla = _flash(q_mla, 0, k_mla, 0, _feature_major(v_mla, 0, b, s), b, s, t_att, MLA_QK_PAD, "plain")

        y = _branch_merge(o_fox, o_dsa, o_mla, zb, w_branch[l].astype(BF16), d, _tile(s, 512), 1024)
        last = l == depth - 1
        x2 = _out_proj(x2, y, w_out[l].astype(BF16), final_norm_g.reshape(1, d), _tile(s, 512), last)
    return x2.reshape(b, s, d)
```

```python
import functools

import jax
import jax.numpy as jnp
import numpy as np
from jax import lax
from jax.experimental import pallas as pl
from jax.experimental.pallas import tpu as pltpu

F32 = jnp.float32
BF16 = jnp.bfloat16

HEAD_DIM = 128
N_HEADS = 8
BRANCH_WIDTH = N_HEADS * HEAD_DIM
N_BRANCH = 3
IDX_HEADS = 16
IDX_DIM = 64
TOPK_MAX = 256
Q_LORA = 512
KV_LORA = 512
QK_NOPE = 128
QK_ROPE = 64
V_DIM = 128
MLA_QK_PAD = 256
ROPE_THETA = 10000.0
EPS = 1e-6

LANES = 128
VMEM_LIMIT = 56 * 1024 * 1024
NEG_BIG = -1e30
INT_MIN = -(2 ** 31)
LOG2E = 1.4426950408889634
Q_STRIP = 256
COUNT_ACCS = 4
PIPE_LEAD = 2
PIPE_LAG = 2

COL_FQ, COL_FK, COL_FV = 0, 1024, 2048
COL_DQ, COL_DK, COL_DV = 3072, 4096, 5120
COL_IQ = 6144
COL_CQ = 7168
COL_MERGE = 8192


def _params(*sem):
    return pltpu.CompilerParams(dimension_semantics=sem, vmem_limit_bytes=VMEM_LIMIT)


def _rope_tables(s):
    pos = jnp.arange(s, dtype=F32)

    def cs(half):
        inv = ROPE_THETA ** (-jnp.arange(half, dtype=F32) / half)
        ang = pos[:, None] * inv[None, :]
        return jnp.cos(ang), jnp.sin(ang)

    c64, s64 = cs(64)
    c32, s32 = cs(32)
    z32 = jnp.zeros_like(s32)
    cos128 = jnp.concatenate([c64, c64], axis=-1)
    sin128 = jnp.concatenate([-s64, s64], axis=-1)
    cos64 = jnp.concatenate([c32, c32, c32, c32], axis=-1)
    sin64_lo = jnp.concatenate([-s32, z32, -s32, z32], axis=-1)
    sin64_hi = jnp.concatenate([z32, s32, z32, s32], axis=-1)
    return cos128, sin128, cos64, sin64_lo, sin64_hi


def _rope128(x, cos, sin):
    return x * cos + pltpu.roll(x, 64, 1) * sin


def _rope64(x, cos, sin_lo, sin_hi):
    return x * cos + pltpu.roll(x, 96, 1) * sin_lo + pltpu.roll(x, 32, 1) * sin_hi


def _in_proj_kernel(x_ref, g_ref, w_ref, ws_ref, c128_ref, s128_ref, c64_ref, s64l_ref, s64h_ref,
                    zb_ref, zs_ref, h_scr, *, tn, q_scale):
    j = pl.program_id(1)

    @pl.when(j == 0)
    def _():
        xf = x_ref[...]
        y = xf * lax.rsqrt(jnp.mean(xf * xf, axis=-1, keepdims=True) + EPS)
        h = (y * g_ref[...]).astype(BF16)
        h_scr[...] = h
        zs = jnp.dot(h, ws_ref[...], preferred_element_type=F32)
        zs_ref[:, :LANES] = zs[:, :LANES]
        zs_ref[:, LANES:] = _rope64(zs[:, LANES:], c64_ref[...], s64l_ref[...], s64h_ref[...])

    acc = jnp.dot(h_scr[...], w_ref[...], preferred_element_type=F32)
    col = j * tn
    is_q = ((col >= COL_FQ) & (col < COL_FK)) | ((col >= COL_DQ) & (col < COL_DK))
    acc = acc * jnp.where(is_q, q_scale, 1.0).astype(F32)
    is_r128 = (col >= COL_DQ) & (col < COL_DV)
    is_r64 = (col >= COL_IQ) & (col < COL_CQ)

    @pl.when(is_r128)
    def _():
        for g in range(tn // LANES):
            sl = slice(g * LANES, (g + 1) * LANES)
            zb_ref[:, sl] = _rope128(acc[:, sl], c128_ref[...], s128_ref[...]).astype(BF16)

    @pl.when(is_r64)
    def _():
        for g in range(tn // LANES):
            sl = slice(g * LANES, (g + 1) * LANES)
            zb_ref[:, sl] = _rope64(acc[:, sl], c64_ref[...], s64l_ref[...], s64h_ref[...]).astype(BF16)

    @pl.when(jnp.logical_not(is_r128 | is_r64))
    def _():
        zb_ref[...] = acc.astype(BF16)


def _in_proj(x2, g, w_big, w_small, tabs, tm, tn):
    m, d = x2.shape
    n = w_big.shape[1]
    c128, s128, c64, s64l, s64h = tabs
    s = c128.shape[0]
    nsb = s // tm
    tab_spec = pl.BlockSpec((tm, LANES), lambda i, j: (i % nsb, 0))
    return pl.pallas_call(
        functools.partial(_in_proj_kernel, tn=tn, q_scale=HEAD_DIM ** -0.5 * LOG2E),
        grid=(m // tm, n // tn),
        in_specs=[
            pl.BlockSpec((tm, d), lambda i, j: (i, 0)),
            pl.BlockSpec((1, d), lambda i, j: (0, 0)),
            pl.BlockSpec((d, tn), lambda i, j: (0, j)),
            pl.BlockSpec((d, 2 * LANES), lambda i, j: (0, 0)),
            tab_spec, tab_spec, tab_spec, tab_spec, tab_spec,
        ],
        out_specs=[
            pl.BlockSpec((tm, tn), lambda i, j: (i, j)),
            pl.BlockSpec((tm, 2 * LANES), lambda i, j: (i, 0)),
        ],
        out_shape=[
            jax.ShapeDtypeStruct((m, n), BF16),
            jax.ShapeDtypeStruct((m, 2 * LANES), F32),
        ],
        scratch_shapes=[pltpu.VMEM((tm, d), BF16)],
        compiler_params=_params("parallel", "arbitrary"),
        name="in_proj",
    )(x2, g, w_big, w_small, c128, s128, c64, s64l, s64h)


def _cum_kernel(zs_ref, fb_ref, cum_ref, carry_scr, *, t):
    @pl.when(pl.program_id(1) == 0)
    def _():
        carry_scr[...] = jnp.zeros_like(carry_scr)

    x = zs_ref[:, :N_HEADS] + fb_ref[...]
    logf = -(jnp.maximum(-x, 0.0) + jnp.log1p(jnp.exp(-jnp.abs(x))))
    r = lax.broadcasted_iota(jnp.int32, (t, t), 0)
    c = lax.broadcasted_iota(jnp.int32, (t, t), 1)
    tri = jnp.where(c <= r, 1.0, 0.0).astype(F32)
    cum = jnp.dot(tri, logf, preferred_element_type=F32, precision=lax.Precision.HIGHEST) + carry_scr[...]
    cum_ref[...] = cum * LOG2E
    carry_scr[...] = cum[t - 1:t, :]


def _forget_cumsum(zs, fb, b, s, t):
    return pl.pallas_call(
        functools.partial(_cum_kernel, t=t),
        grid=(b, s // t),
        in_specs=[
            pl.BlockSpec((t, 2 * LANES), lambda bi, i: (bi * (s // t) + i, 0)),
            pl.BlockSpec((1, N_HEADS), lambda bi, i: (0, 0)),
        ],
        out_specs=pl.BlockSpec((t, N_HEADS), lambda bi, i: (bi * (s // t) + i, 0)),
        out_shape=jax.ShapeDtypeStruct((b * s, N_HEADS), F32),
        scratch_shapes=[pltpu.VMEM((1, N_HEADS), F32)],
        compiler_params=_params("parallel", "arbitrary"),
        name="forget_cumsum",
    )(zs, fb)


def _flash_kernel(qt_ref, kt_ref, q_ref, k_ref, v_ref, *rest, mode, dk, t):
    if mode == "plain":
        o_ref, m_scr, l_scr, acc_scr = rest
        extra_ref = None
    else:
        extra_ref, o_ref, m_scr, l_scr, acc_scr = rest
    p = pl.program_id(1)
    qi = qt_ref[p]
    ki = kt_ref[p]
    qs = min(t, Q_STRIP)

    @pl.when(ki == 0)
    def _():
        m_scr[...] = jnp.full_like(m_scr, NEG_BIG)
        l_scr[...] = jnp.zeros_like(l_scr)
        acc_scr[...] = jnp.zeros_like(acc_scr)

    def step(diag):
        units = [(c, h) for c in range(t // qs) for h in range(N_HEADS)]

        def n_keys(c):
            return (c + 1) * qs if diag else t

        def logits(c, h):
            k_h = k_ref[:n_keys(c), h * dk:(h + 1) * dk]
            q_h = q_ref[c * qs:(c + 1) * qs, h * dk:(h + 1) * dk]
            return lax.dot_general(k_h, q_h, (((1,), (1,)), ((), ())), preferred_element_type=F32)

        def keep_mask(c):
            nk = n_keys(c)
            if mode == "mask":
                return extra_ref[:, c * qs:(c + 1) * qs].astype(jnp.int32) != 0
            if diag:
                kr = lax.broadcasted_iota(jnp.int32, (nk, qs), 0)
                qc = lax.broadcasted_iota(jnp.int32, (nk, qs), 1) + c * qs
                return kr <= qc
            return None

        keeps = {}

        def softmax(c, h, st):
            cs = slice(c * qs, (c + 1) * qs)
            if c not in keeps:
                keeps[c] = keep_mask(c)
            if mode == "fox":
                st = st - extra_ref[:n_keys(c), h:h + 1]
            if keeps[c] is not None:
                st = jnp.where(keeps[c], st, NEG_BIG)
            m_prev = m_scr[h:h + 1, cs]
            m_new = jnp.maximum(m_prev, jnp.max(st, axis=0, keepdims=True))
            alpha = jnp.exp2(m_prev - m_new)
            pt = jnp.exp2(st - m_new)
            l_scr[h:h + 1, cs] = alpha * l_scr[h:h + 1, cs] + jnp.sum(pt, axis=0, keepdims=True)
            m_scr[h:h + 1, cs] = m_new
            return alpha, pt.astype(BF16)

        def accumulate(c, h, alpha, pt):
            cs = slice(c * qs, (c + 1) * qs)
            vt_h = v_ref[h * HEAD_DIM:(h + 1) * HEAD_DIM, :n_keys(c)]
            pv = jnp.dot(vt_h, pt, preferred_element_type=F32)
            sl = slice(h * HEAD_DIM, (h + 1) * HEAD_DIM)
            acc_scr[sl, cs] = alpha * acc_scr[sl, cs] + pv

        n_u = len(units)
        st_q = [logits(*units[i]) for i in range(min(PIPE_LEAD, n_u))]
        p_q = []
        for u in range(n_u + PIPE_LAG):
            if u + PIPE_LEAD < n_u:
                st_q.append(logits(*units[u + PIPE_LEAD]))
            if u < n_u:
                p_q.append(softmax(*units[u], st_q.pop(0)))
            if u >= PIPE_LAG:
                accumulate(*units[u - PIPE_LAG], *p_q.pop(0))

    if mode == "mask":
        step(False)
    else:
        @pl.when(ki == qi)
        def _():
            step(True)

        @pl.when(ki != qi)
        def _():
            step(False)

    @pl.when(ki == qi)
    def _():
        for h in range(N_HEADS):
            sl = slice(h * HEAD_DIM, (h + 1) * HEAD_DIM)
            o = acc_scr[sl, :] / l_scr[h:h + 1, :]
            o_ref[:, sl] = o.T.astype(BF16)


def _pair_tables(n):
    qt = np.concatenate([np.full(i + 1, i, np.int32) for i in range(n)])
    kt = np.concatenate([np.arange(i + 1, dtype=np.int32) for i in range(n)])
    return jnp.asarray(qt), jnp.asarray(kt)


def _flash(q_arr, q_col, k_arr, k_col, vt_arr, b, s, t, dk, mode, extra=None):
    n = s // t
    qt, kt = _pair_tables(n)
    wq = N_HEADS * dk
    wv = N_HEADS * HEAD_DIM
    in_specs = [
        pl.BlockSpec((t, wq), lambda bi, p, qt, kt: (bi * n + qt[p], q_col)),
        pl.BlockSpec((t, wq), lambda bi, p, qt, kt: (bi * n + kt[p], k_col)),
        pl.BlockSpec((None, wv, t), lambda bi, p, qt, kt: (bi, 0, kt[p])),
    ]
    args = [q_arr, k_arr, vt_arr]
    if mode == "fox":
        in_specs.append(pl.BlockSpec((t, N_HEADS), lambda bi, p, qt, kt: (bi * n + kt[p], 0)))
        args.append(extra)
    elif mode == "mask":
        in_specs.append(pl.BlockSpec((None, t, t), lambda bi, p, qt, kt: (bi, kt[p], qt[p])))
        args.append(extra)
    grid_spec = pltpu.PrefetchScalarGridSpec(
        num_scalar_prefetch=2,
        grid=(b, n * (n + 1) // 2),
        in_specs=in_specs,
        out_specs=pl.BlockSpec((t, wv), lambda bi, p, qt, kt: (bi * n + qt[p], 0)),
        scratch_shapes=[
            pltpu.VMEM((N_HEADS, t), F32),
            pltpu.VMEM((N_HEADS, t), F32),
            pltpu.VMEM((wv, t), F32),
        ],
    )
    return pl.pallas_call(
        functools.partial(_flash_kernel, mode=mode, dk=dk, t=t),
        grid_spec=grid_spec,
        out_shape=jax.ShapeDtypeStruct((b * s, wv), BF16),
        compiler_params=_params("parallel", "arbitrary"),
        name="flash_" + mode,
    )(qt, kt, *args)


def _count(sc_ref, nvalid, cand, tk, tq, ind_fn):
    group = _count_group(sc_ref.shape[0])

    lanes = COUNT_ACCS * 8

    def body(g, a):
        for e in range(group):
            c = g * group + e
            ind = ind_fn(sc_ref[c], cand, c)
            a = a + jnp.sum(ind.reshape(tk // lanes, lanes, tq), axis=0)
        return a

    part = lax.fori_loop(0, (nvalid + group - 1) // group, body, jnp.zeros((lanes, tq), jnp.int32))
    return jnp.sum(part, axis=0, keepdims=True)


def _count_group(nchunks):
    return 2 if nchunks % 2 == 0 else 1


def _indexer_kernel(q_ref, wt_ref, k2_ref, mask_ref, sc_ref, qm_scr, j_scr, *, tq, tk, n_sel, idx_bits):
    qi = pl.program_id(1)
    one, zero = jnp.int32(1), jnp.int32(0)
    nchunks = sc_ref.shape[0]
    nvalid = ((qi + 1) * tq + tk - 1) // tk
    qcol = qi * tq + lax.broadcasted_iota(jnp.int32, (1, tq), 1)
    krow0 = lax.broadcasted_iota(jnp.int32, (tk, 1), 0)

    half = lax.broadcasted_iota(jnp.int32, (tq, LANES), 1) // IDX_DIM
    for g in range(IDX_HEADS // 2):
        qp = q_ref[:, g * LANES:(g + 1) * LANES]
        for e in range(2):
            qm_scr[2 * g + e] = jnp.where(half == e, qp, jnp.zeros_like(qp))

    def score_chunk(c, kmax):
        kc = k2_ref[pl.ds(pl.multiple_of(c * tk, tk), tk), :]
        acc = jnp.zeros((tk, tq), F32)
        for h in range(IDX_HEADS):
            x = lax.dot_general(kc, qm_scr[h], (((1,), (1,)), ((), ())), preferred_element_type=F32)
            acc = acc + wt_ref[h:h + 1, :] * jnp.maximum(x, 0.0)
        score = acc * (IDX_HEADS * IDX_DIM) ** -0.5
        bits = pltpu.bitcast(score, jnp.int32)
        key = bits ^ ((bits >> 31) & 0x7FFFFFFF)
        causal = (c * tk + krow0) <= qcol
        key = jnp.where(causal, key, INT_MIN)
        sc_ref[c] = key
        return jnp.maximum(kmax, jnp.max(key.reshape(tk // 8, 8, tq), axis=0))

    kmax = lax.fori_loop(0, nvalid, score_chunk, jnp.full((8, tq), INT_MIN, jnp.int32))
    kmax = jnp.max(kmax, axis=0, keepdims=True)

    group = _count_group(nchunks)
    if group > 1:
        def pad_chunk(c, carry):
            sc_ref[c] = jnp.full((tk, tq), INT_MIN, jnp.int32)
            return carry

        lax.fori_loop(nvalid, (nvalid + group - 1) // group * group, pad_chunk, 0)

    def count_ge(cand):
        return _count(sc_ref, nvalid, cand, tk, tq, lambda k, cd, c: jnp.where(k >= cd, one, zero))

    n_keys = qcol + 1
    f0 = count_ge(jnp.zeros((1, tq), jnp.int32))
    pos = f0 >= n_sel
    lo0 = jnp.where(pos, 0, INT_MIN + 1)
    hi0 = jnp.where(pos, kmax, jnp.minimum(kmax, -1)) + 1
    flo0 = jnp.where(pos, f0, n_keys)

    def searching(flo, w):
        return jnp.where(flo > n_sel, jnp.where(w > 1, 1.0, 0.0), 0.0)

    def bisect_cond(carry):
        it, lo, w, flo = carry
        return (jnp.max(searching(flo, w)) > 0.0) & (it < 32)

    def bisect_body(carry):
        it, lo, w, flo = carry
        active = searching(flo, w) > 0
        off = w >> 1
        cand = lo + off
        cnt = count_ge(cand)
        ok = cnt >= n_sel
        lo = jnp.where(active, jnp.where(ok, cand, lo), lo)
        flo = jnp.where(active, jnp.where(ok, cnt, flo), flo)
        w = jnp.where(active, jnp.where(ok, w - off, off), w)
        return it + 1, lo, w, flo

    _, lo, _, flo = lax.while_loop(bisect_cond, bisect_body, (jnp.int32(0), lo0, hi0 - lo0, flo0))
    found = n_keys >= n_sel
    tie = jnp.where(found, jnp.where(flo > n_sel, one, zero), zero) > 0
    gt_thr = jnp.where(found, jnp.where(tie, lo, lo - 1), INT_MIN)
    thr = jnp.where(tie, lo, INT_MIN + 1)

    j_scr[...] = jnp.full_like(j_scr, 2 ** idx_bits - 1)
    has_tie = jnp.max(jnp.where(tie, 1.0, 0.0)) > 0.0

    @pl.when(has_tie)
    def _():
        cgt = _count(sc_ref, nvalid, thr, tk, tq, lambda k, cd, c: jnp.where(k > cd, one, zero))
        need = n_sel - cgt

        def jbit(i, jv):
            cand = jv | lax.shift_left(jnp.int32(1), idx_bits - 1 - i)
            cnt = _count(sc_ref, nvalid, cand, tk, tq,
                         lambda k, cd, c: jnp.where(k == thr, jnp.where((c * tk + krow0) < cd, one, zero), zero))
            return jnp.where(cnt < need, cand, jv)

        j_scr[...] = lax.fori_loop(0, idx_bits, jbit, jnp.zeros((1, tq), jnp.int32))

    jlast = j_scr[...]

    def write_chunk(c, carry):
        key = sc_ref[c]
        sel = jnp.where(key > gt_thr, one,
                        jnp.where(key == thr, jnp.where((c * tk + krow0) <= jlast, one, zero), zero))
        mask_ref[pl.ds(pl.multiple_of(c * tk, tk), tk), :] = sel.astype(jnp.int8)
        return carry

    lax.fori_loop(0, nvalid, write_chunk, 0)

    def zero_chunk(c, carry):
        mask_ref[pl.ds(pl.multiple_of(c * tk, tk), tk), :] = jnp.zeros((tk, tq), jnp.int8)
        return carry

    lax.fori_loop(nvalid, nchunks, zero_chunk, 0)


def _indexer_mask(zb, wt, k2, b, s, tq, tk, n_sel):
    nq = s // tq
    idx_bits = max(1, int(np.ceil(np.log2(s))))
    return pl.pallas_call(
        functools.partial(_indexer_kernel, tq=tq, tk=tk, n_sel=n_sel, idx_bits=idx_bits),
        grid=(b, nq),
        in_specs=[
            pl.BlockSpec((tq, IDX_HEADS * IDX_DIM), lambda bi, i: (bi * nq + i, COL_IQ // (IDX_HEADS * IDX_DIM))),
            pl.BlockSpec((None, IDX_HEADS, tq), lambda bi, i: (bi, 0, i)),
            pl.BlockSpec((None, s, LANES), lambda bi, i: (bi, 0, 0)),
        ],
        out_specs=pl.BlockSpec((None, s, tq), lambda bi, i: (bi, 0, i)),
        out_shape=jax.ShapeDtypeStruct((b, s, s), jnp.int8),
        scratch_shapes=[
            pltpu.VMEM((s // tk, tk, tq), jnp.int32),
            pltpu.VMEM((IDX_HEADS, tq, LANES), BF16),
            pltpu.VMEM((1, tq), jnp.int32),
        ],
        compiler_params=_params("parallel", "arbitrary"),
        name="indexer_mask",
    )(zb, wt, k2)


def _mla_up_kernel(c_ref, zs_ref, qg_ref, kvg_ref, wq_ref, wk_ref, wv_ref, c64_ref, s64l_ref, s64h_ref,
                   q_ref, k_ref, v_ref, *, q_scale):
    def norm(xf, g):
        y = xf * lax.rsqrt(jnp.mean(xf * xf, axis=-1, keepdims=True) + EPS)
        return (y * g).astype(BF16)

    cq = norm(c_ref[:, :Q_LORA].astype(F32), qg_ref[...])
    ckv = norm(c_ref[:, Q_LORA:].astype(F32), kvg_ref[...])
    q = jnp.dot(cq, wq_ref[...], preferred_element_type=F32) * q_scale
    k = jnp.dot(ckv, wk_ref[...], preferred_element_type=F32)
    v_ref[...] = jnp.dot(ckv, wv_ref[...], preferred_element_type=F32).astype(BF16)
    lane = lax.broadcasted_iota(jnp.int32, (1, LANES), 1)
    k_rot = jnp.where(lane < QK_ROPE, zs_ref[:, LANES:], 0.0)
    cos, sin_lo, sin_hi = c64_ref[...], s64l_ref[...], s64h_ref[...]
    for h in range(N_HEADS):
        nope = slice(h * MLA_QK_PAD, h * MLA_QK_PAD + QK_NOPE)
        rope = slice(h * MLA_QK_PAD + QK_NOPE, (h + 1) * MLA_QK_PAD)
        q_ref[:, nope] = q[:, nope].astype(BF16)
        q_ref[:, rope] = _rope64(q[:, rope], cos, sin_lo, sin_hi).astype(BF16)
        k_ref[:, nope] = k[:, nope].astype(BF16)
        k_ref[:, rope] = k_rot.astype(BF16)


def _mla_up(zb, zs, qg, kvg, wq, wk, wv, tabs, tm):
    m = zb.shape[0]
    _, _, c64, s64l, s64h = tabs
    nsb = c64.shape[0] // tm
    tab_spec = pl.BlockSpec((tm, LANES), lambda i: (i % nsb, 0))
    wqk = N_HEADS * MLA_QK_PAD
    full = lambda a: pl.BlockSpec(a.shape, lambda i: (0,) * a.ndim)
    return pl.pallas_call(
        functools.partial(_mla_up_kernel, q_scale=(QK_NOPE + QK_ROPE) ** -0.5 * LOG2E),
        grid=(m // tm,),
        in_specs=[
            pl.BlockSpec((tm, Q_LORA + KV_LORA), lambda i: (i, COL_CQ // (Q_LORA + KV_LORA))),
            pl.BlockSpec((tm, 2 * LANES), lambda i: (i, 0)),
            full(qg), full(kvg), full(wq), full(wk), full(wv),
            tab_spec, tab_spec, tab_spec,
        ],
        out_specs=[
            pl.BlockSpec((tm, wqk), lambda i: (i, 0)),
            pl.BlockSpec((tm, wqk), lambda i: (i, 0)),
            pl.BlockSpec((tm, BRANCH_WIDTH), lambda i: (i, 0)),
        ],
        out_shape=[
            jax.ShapeDtypeStruct((m, wqk), BF16),
            jax.ShapeDtypeStruct((m, wqk), BF16),
            jax.ShapeDtypeStruct((m, BRANCH_WIDTH), BF16),
        ],
        compiler_params=_params("parallel"),
        name="mla_up",
    )(zb, zs, qg, kvg, wq, wk, wv, c64, s64l, s64h)


def _branch_kernel(o0_ref, o1_ref, o2_ref, g0_ref, g1_ref, g2_ref, m0_ref, m1_ref, m2_ref, w_ref, y_ref):
    y = None
    for i, (o_ref, g_ref, m_ref) in enumerate(((o0_ref, g0_ref, m0_ref), (o1_ref, g1_ref, m1_ref),
                                               (o2_ref, g2_ref, m2_ref))):
        g = g_ref[...].astype(F32)
        a = (o_ref[...].astype(F32) * (g * jax.nn.sigmoid(g))).astype(BF16)
        term = jax.nn.sigmoid(m_ref[...].astype(F32)) * jnp.dot(a, w_ref[i], preferred_element_type=F32)
        y = term if y is None else y + term
    y_ref[...] = y.astype(BF16)


def _branch_merge(o_fox, o_dsa, o_mla, zb, wb, d, tm, tn):
    m = zb.shape[0]
    o_spec = pl.BlockSpec((tm, BRANCH_WIDTH), lambda i, j: (i, 0))
    silu0 = (COL_MERGE + N_BRANCH * d) // BRANCH_WIDTH

    def g_spec(i_br):
        return pl.BlockSpec((tm, BRANCH_WIDTH), lambda i, j: (i, silu0 + i_br))

    def m_spec(i_br):
        base = (COL_MERGE + i_br * d) // tn
        return pl.BlockSpec((tm, tn), lambda i, j: (i, base + j))

    return pl.pallas_call(
        _branch_kernel,
        grid=(m // tm, d // tn),
        in_specs=[o_spec, o_spec, o_spec, g_spec(0), g_spec(1), g_spec(2), m_spec(0), m_spec(1), m_spec(2),
                  pl.BlockSpec((N_BRANCH, BRANCH_WIDTH, tn), lambda i, j: (0, 0, j))],
        out_specs=pl.BlockSpec((tm, tn), lambda i, j: (i, j)),
        out_shape=jax.ShapeDtypeStruct((m, d), BF16),
        compiler_params=_params("parallel", "arbitrary"),
        name="branch_merge",
    )(o_fox, o_dsa, o_mla, zb, zb, zb, zb, zb, zb, wb)


def _out_kernel(x_ref, y_ref, w_ref, g_ref, o_ref, *, final_norm):
    r = x_ref[...] + jnp.dot(y_ref[...], w_ref[...], preferred_element_type=F32)
    if final_norm:
        r = r * lax.rsqrt(jnp.mean(r * r, axis=-1, keepdims=True) + EPS) * g_ref[...]
    o_ref[...] = r


def _out_proj(x2, y, w, g, tm, final_norm):
    m, d = x2.shape
    return pl.pallas_call(
        functools.partial(_out_kernel, final_norm=final_norm),
        grid=(m // tm,),
        in_specs=[
            pl.BlockSpec((tm, d), lambda i: (i, 0)),
            pl.BlockSpec((tm, d), lambda i: (i, 0)),
            pl.BlockSpec((d, d), lambda i: (0, 0)),
            pl.BlockSpec((1, d), lambda i: (0, 0)),
        ],
        out_specs=pl.BlockSpec((tm, d), lambda i: (i, 0)),
        out_shape=jax.ShapeDtypeStruct((m, d), F32),
        compiler_params=_params("parallel"),
        name="out_proj",
    )(x2, y, w, g)


def _split_w_in(w_in, d):
    sizes = (BRANCH_WIDTH, BRANCH_WIDTH, BRANCH_WIDTH, N_HEADS, BRANCH_WIDTH, BRANCH_WIDTH, BRANCH_WIDTH,
             IDX_HEADS * IDX_DIM, IDX_DIM, IDX_HEADS, Q_LORA, KV_LORA, QK_ROPE,
             N_BRANCH * BRANCH_WIDTH, N_BRANCH * d)
    pts = np.cumsum(sizes)[:-1].tolist()
    (f_q, f_k, f_v, f_gate, d_q, d_k, d_v, i_q, i_k, i_w, c_q, c_kv, k_pe, silu_g, merge_g) = jnp.split(
        w_in, pts, axis=-1)
    big = jnp.concatenate([f_q, f_k, f_v, d_q, d_k, d_v, i_q, c_q, c_kv, merge_g, silu_g], axis=-1).astype(BF16)
    pad = jnp.zeros((w_in.shape[0], LANES - N_HEADS - IDX_HEADS), w_in.dtype)
    small = jnp.concatenate([f_gate, i_w, pad, k_pe, i_k], axis=-1).astype(BF16)
    return big, small


def _pad_heads(w, take, width_in):
    r = w.shape[0]
    w3 = w.reshape(r, N_HEADS, width_in)[:, :, take]
    w3 = jnp.pad(w3, ((0, 0), (0, 0), (0, MLA_QK_PAD - w3.shape[-1])))
    return w3.reshape(r, N_HEADS * MLA_QK_PAD)


def _tile(s, pref):
    return pref if s % pref == 0 else s


def _feature_major(a, col, b, s):
    return jnp.swapaxes(a[:, col:col + BRANCH_WIDTH].reshape(b, s, BRANCH_WIDTH), 1, 2)


def kernel(x, norm_g, w_in, forget_b, q_norm_g, w_q_up, kv_norm_g, w_kv_up, w_branch, w_out, final_norm_g):
    b, s, d = x.shape
    depth = w_in.shape[0]
    n_sel = min(TOPK_MAX, s // 4)
    m = b * s
    tabs = _rope_tables(s)
    t_att = _tile(s, 512)
    tm_in = _tile(s, 1024)
    x2 = x.reshape(m, d)
    for l in range(depth):
        w_big, w_small = _split_w_in(w_in[l], d)
        zb, zs = _in_proj(x2, norm_g[l].reshape(1, d), w_big, w_small, tabs, tm_in, 512)

        cum = _forget_cumsum(zs, forget_b[l].reshape(1, N_HEADS), b, s, _tile(s, 1024))
        o_fox = _flash(zb, COL_FQ // BRANCH_WIDTH, zb, COL_FK // BRANCH_WIDTH, _feature_major(zb, COL_FV, b, s),
                       b, s, t_att, HEAD_DIM, "fox", cum)

        zs3 = zs.reshape(b, s, 2 * LANES)
        wt = jnp.swapaxes(zs3[:, :, N_HEADS:N_HEADS + IDX_HEADS], 1, 2)
        ik = zs3[:, :, LANES + QK_ROPE:].astype(BF16)
        k2 = jnp.concatenate([ik, ik], axis=-1)
        mask = _indexer_mask(zb, wt, k2, b, s, _tile(s, 256), _tile(s, 512), n_sel)
        o_dsa = _flash(zb, COL_DQ // BRANCH_WIDTH, zb, COL_DK // BRANCH_WIDTH, _feature_major(zb, COL_DV, b, s),
                       b, s, t_att, HEAD_DIM, "mask", mask)

        wq = _pad_heads(w_q_up[l], slice(0, QK_NOPE + QK_ROPE), QK_NOPE + QK_ROPE).astype(BF16)
        wk = _pad_heads(w_kv_up[l], slice(0, QK_NOPE), QK_NOPE + V_DIM).astype(BF16)
        wv = w_kv_up[l].reshape(KV_LORA, N_HEADS, QK_NOPE + V_DIM)[:, :, QK_NOPE:].reshape(
            KV_LORA, BRANCH_WIDTH).astype(BF16)
        q_mla, k_mla, v_mla = _mla_up(zb, zs, q_norm_g[l].reshape(1, Q_LORA), kv_norm_g[l].reshape(1, KV_LORA),
                                      wq, wk, wv, tabs, _tile(s, 512))
        o_mla = _flash(q_mla, 0, k_mla, 0, _feature_major(v_mla, 0, b, s), b, s, t_att, MLA_QK_PAD, "plain")

        y = _branch_merge(o_fox, o_dsa, o_mla, zb, w_branch[l].astype(BF16), d, _tile(s, 512), 1024)
        last = l == depth - 1
        x2 = _out_proj(x2, y, w_out[l].astype(BF16), final_norm_g.reshape(1, d), _tile(s, 512), last)
    return x2.reshape(b, s, d)
```

```python
import functools

import jax
import jax.numpy as jnp
import numpy as np
from jax import lax
from jax.experimental import pallas as pl
from jax.experimental.pallas import tpu as pltpu

F32 = jnp.float32
BF16 = jnp.bfloat16

HEAD_DIM = 128
N_HEADS = 8
BRANCH_WIDTH = N_HEADS * HEAD_DIM
N_BRANCH = 3
IDX_HEADS = 16
IDX_DIM = 64
TOPK_MAX = 256
Q_LORA = 512
KV_LORA = 512
QK_NOPE = 128
QK_ROPE = 64
V_DIM = 128
MLA_QK_PAD = 256
FOX_QK_PAD = 256
VT_ROWS = 144
ROPE_THETA = 10000.0
EPS = 1e-6

LANES = 128
VMEM_LIMIT = 56 * 1024 * 1024
NEG_BIG = -1e30
INT_MIN = -(2 ** 31)
LOG2E = 1.4426950408889634
Q_STRIP = 256
COUNT_ACCS = 4
PIPE_LEAD = 2
PIPE_LAG = 2

COL_FQ, COL_FK, COL_FV = 0, 1024, 2048
COL_DQ, COL_DK, COL_DV = 3072, 4096, 5120
COL_IQ = 6144
COL_CQ = 7168
COL_MERGE = 8192


def _params(*sem):
    return pltpu.CompilerParams(dimension_semantics=sem, vmem_limit_bytes=VMEM_LIMIT)


def _rope_tables(s):
    pos = jnp.arange(s, dtype=F32)

    def cs(half):
        inv = ROPE_THETA ** (-jnp.arange(half, dtype=F32) / half)
        ang = pos[:, None] * inv[None, :]
        return jnp.cos(ang), jnp.sin(ang)

    c64, s64 = cs(64)
    c32, s32 = cs(32)
    z32 = jnp.zeros_like(s32)
    cos128 = jnp.concatenate([c64, c64], axis=-1)
    sin128 = jnp.concatenate([-s64, s64], axis=-1)
    cos64 = jnp.concatenate([c32, c32, c32, c32], axis=-1)
    sin64_lo = jnp.concatenate([-s32, z32, -s32, z32], axis=-1)
    sin64_hi = jnp.concatenate([z32, s32, z32, s32], axis=-1)
    return cos128, sin128, cos64, sin64_lo, sin64_hi


def _rope128(x, cos, sin):
    return x * cos + pltpu.roll(x, 64, 1) * sin


def _rope64(x, cos, sin_lo, sin_hi):
    return x * cos + pltpu.roll(x, 96, 1) * sin_lo + pltpu.roll(x, 32, 1) * sin_hi


def _in_proj_kernel(x_ref, g_ref, w_ref, ws_ref, c128_ref, s128_ref, c64_ref, s64l_ref, s64h_ref,
                    zb_ref, zs_ref, h_scr, *, tn, q_scale):
    j = pl.program_id(1)

    @pl.when(j == 0)
    def _():
        xf = x_ref[...]
        y = xf * lax.rsqrt(jnp.mean(xf * xf, axis=-1, keepdims=True) + EPS)
        h = (y * g_ref[...]).astype(BF16)
        h_scr[...] = h
        zs = jnp.dot(h, ws_ref[...], preferred_element_type=F32)
        zs_ref[:, :LANES] = zs[:, :LANES]
        zs_ref[:, LANES:] = _rope64(zs[:, LANES:], c64_ref[...], s64l_ref[...], s64h_ref[...])

    acc = jnp.dot(h_scr[...], w_ref[...], preferred_element_type=F32)
    col = j * tn
    is_q = ((col >= COL_FQ) & (col < COL_FK)) | ((col >= COL_DQ) & (col < COL_DK))
    acc = acc * jnp.where(is_q, q_scale, 1.0).astype(F32)
    is_r128 = (col >= COL_DQ) & (col < COL_DV)
    is_r64 = (col >= COL_IQ) & (col < COL_CQ)

    @pl.when(is_r128)
    def _():
        for g in range(tn // LANES):
            sl = slice(g * LANES, (g + 1) * LANES)
            zb_ref[:, sl] = _rope128(acc[:, sl], c128_ref[...], s128_ref[...]).astype(BF16)

    @pl.when(is_r64)
    def _():
        for g in range(tn // LANES):
            sl = slice(g * LANES, (g + 1) * LANES)
            zb_ref[:, sl] = _rope64(acc[:, sl], c64_ref[...], s64l_ref[...], s64h_ref[...]).astype(BF16)

    @pl.when(jnp.logical_not(is_r128 | is_r64))
    def _():
        zb_ref[...] = acc.astype(BF16)


def _in_proj(x2, g, w_big, w_small, tabs, tm, tn):
    m, d = x2.shape
    n = w_big.shape[1]
    c128, s128, c64, s64l, s64h = tabs
    s = c128.shape[0]
    nsb = s // tm
    tab_spec = pl.BlockSpec((tm, LANES), lambda i, j: (i % nsb, 0))
    return pl.pallas_call(
        functools.partial(_in_proj_kernel, tn=tn, q_scale=HEAD_DIM ** -0.5 * LOG2E),
        grid=(m // tm, n // tn),
        in_specs=[
            pl.BlockSpec((tm, d), lambda i, j: (i, 0)),
            pl.BlockSpec((1, d), lambda i, j: (0, 0)),
            pl.BlockSpec((d, tn), lambda i, j: (0, j)),
            pl.BlockSpec((d, 2 * LANES), lambda i, j: (0, 0)),
            tab_spec, tab_spec, tab_spec, tab_spec, tab_spec,
        ],
        out_specs=[
            pl.BlockSpec((tm, tn), lambda i, j: (i, j)),
            pl.BlockSpec((tm, 2 * LANES), lambda i, j: (i, 0)),
        ],
        out_shape=[
            jax.ShapeDtypeStruct((m, n), BF16),
            jax.ShapeDtypeStruct((m, 2 * LANES), F32),
        ],
        scratch_shapes=[pltpu.VMEM((tm, d), BF16)],
        compiler_params=_params("parallel", "arbitrary"),
        name="in_proj",
    )(x2, g, w_big, w_small, c128, s128, c64, s64l, s64h)


def _bf16_floor(x):
    return pltpu.bitcast(pltpu.bitcast(x, jnp.int32) & jnp.int32(-65536), F32)


def _cum_kernel(zs_ref, fb_ref, hi_ref, mid_ref, lo_ref, carry_scr, *, t):
    @pl.when(pl.program_id(1) == 0)
    def _():
        carry_scr[...] = jnp.zeros_like(carry_scr)

    x = zs_ref[:, :N_HEADS] + fb_ref[...]
    logf = -(jnp.maximum(-x, 0.0) + jnp.log1p(jnp.exp(-jnp.abs(x))))
    r = lax.broadcasted_iota(jnp.int32, (t, t), 0)
    c = lax.broadcasted_iota(jnp.int32, (t, t), 1)
    tri = jnp.where(c <= r, 1.0, 0.0).astype(F32)
    cum = jnp.dot(tri, logf, preferred_element_type=F32, precision=lax.Precision.HIGHEST) + carry_scr[...]
    carry_scr[...] = cum[t - 1:t, :]
    c = cum * (-LOG2E)
    hi = _bf16_floor(c)
    mid = _bf16_floor(c - hi)
    hi_ref[...] = hi.astype(BF16)
    mid_ref[...] = mid.astype(BF16)
    lo_ref[...] = ((c - hi) - mid).astype(BF16)


def _forget_cumsum(zs, fb, b, s, t):
    return pl.pallas_call(
        functools.partial(_cum_kernel, t=t),
        grid=(b, s // t),
        in_specs=[
            pl.BlockSpec((t, 2 * LANES), lambda bi, i: (bi * (s // t) + i, 0)),
            pl.BlockSpec((1, N_HEADS), lambda bi, i: (0, 0)),
        ],
        out_specs=[pl.BlockSpec((t, N_HEADS), lambda bi, i: (bi * (s // t) + i, 0))] * 3,
        out_shape=[jax.ShapeDtypeStruct((b * s, N_HEADS), BF16)] * 3,
        scratch_shapes=[pltpu.VMEM((1, N_HEADS), F32)],
        compiler_params=_params("parallel", "arbitrary"),
        name="forget_cumsum",
    )(zs, fb)


def _flash_kernel(qt_ref, kt_ref, q_ref, k_ref, v_ref, *rest, masked, dk, t):
    if masked:
        mask_ref, o_ref, m_scr, l_scr, acc_scr = rest
    else:
        o_ref, m_scr, l_scr, acc_scr = rest
        mask_ref = None
    p = pl.program_id(1)
    qi = qt_ref[p]
    ki = kt_ref[p]
    qs = min(t, Q_STRIP)

    @pl.when(ki == 0)
    def _():
        m_scr[...] = jnp.full_like(m_scr, NEG_BIG)
        l_scr[...] = jnp.zeros_like(l_scr)
        acc_scr[...] = jnp.zeros_like(acc_scr)

    def step(diag):
        units = [(c, h) for c in range(t // qs) for h in range(N_HEADS)]

        def n_keys(c):
            return (c + 1) * qs if diag else t

        def logits(c, h):
            k_h = k_ref[:n_keys(c), h * dk:(h + 1) * dk]
            q_h = q_ref[c * qs:(c + 1) * qs, h * dk:(h + 1) * dk]
            return lax.dot_general(k_h, q_h, (((1,), (1,)), ((), ())), preferred_element_type=F32)

        def keep_mask(c):
            nk = n_keys(c)
            if masked:
                return mask_ref[:, c * qs:(c + 1) * qs].astype(jnp.int32) != 0
            if diag:
                kr = lax.broadcasted_iota(jnp.int32, (nk, qs), 0)
                qc = lax.broadcasted_iota(jnp.int32, (nk, qs), 1) + c * qs
                return kr <= qc
            return None

        keeps = {}

        def softmax(c, h, st):
            cs = slice(c * qs, (c + 1) * qs)
            if c not in keeps:
                keeps[c] = keep_mask(c)
            if keeps[c] is not None:
                st = jnp.where(keeps[c], st, NEG_BIG)
            m_prev = m_scr[h:h + 1, cs]
            m_new = jnp.maximum(m_prev, jnp.max(st, axis=0, keepdims=True))
            m_scr[h:h + 1, cs] = m_new
            alpha = jnp.exp2(m_prev - m_new)
            return alpha, jnp.exp2(st - m_new).astype(BF16)

        def accumulate(c, h, alpha, pt):
            cs = slice(c * qs, (c + 1) * qs)
            vt_h = v_ref[h * VT_ROWS:(h + 1) * VT_ROWS, :n_keys(c)]
            pv = jnp.dot(vt_h, pt, preferred_element_type=F32)
            sl = slice(h * HEAD_DIM, (h + 1) * HEAD_DIM)
            acc_scr[sl, cs] = alpha * acc_scr[sl, cs] + pv[:HEAD_DIM]
            l_scr[h:h + 1, cs] = alpha * l_scr[h:h + 1, cs] + pv[HEAD_DIM:HEAD_DIM + 1]

        n_u = len(units)
        st_q = [logits(*units[i]) for i in range(min(PIPE_LEAD, n_u))]
        p_q = []
        for u in range(n_u + PIPE_LAG):
            if u + PIPE_LEAD < n_u:
                st_q.append(logits(*units[u + PIPE_LEAD]))
            if u < n_u:
                p_q.append(softmax(*units[u], st_q.pop(0)))
            if u >= PIPE_LAG:
                accumulate(*units[u - PIPE_LAG], *p_q.pop(0))

    if masked:
        step(False)
    else:
        @pl.when(ki == qi)
        def _():
            step(True)

        @pl.when(ki != qi)
        def _():
            step(False)

    @pl.when(ki == qi)
    def _():
        for h in range(N_HEADS):
            sl = slice(h * HEAD_DIM, (h + 1) * HEAD_DIM)
            o = acc_scr[sl, :] / l_scr[h:h + 1, :]
            o_ref[:, sl] = o.T.astype(BF16)


def _pair_tables(n):
    qt = np.concatenate([np.full(i + 1, i, np.int32) for i in range(n)])
    kt = np.concatenate([np.arange(i + 1, dtype=np.int32) for i in range(n)])
    return jnp.asarray(qt), jnp.asarray(kt)


def _flash(q_arr, q_col, k_arr, k_col, vt_arr, b, s, t, dk, name, mask=None):
    n = s // t
    qt, kt = _pair_tables(n)
    wq = N_HEADS * dk
    wv = N_HEADS * HEAD_DIM
    in_specs = [
        pl.BlockSpec((t, wq), lambda bi, p, qt, kt: (bi * n + qt[p], q_col)),
        pl.BlockSpec((t, wq), lambda bi, p, qt, kt: (bi * n + kt[p], k_col)),
        pl.BlockSpec((None, N_HEADS * VT_ROWS, t), lambda bi, p, qt, kt: (bi, 0, kt[p])),
    ]
    args = [q_arr, k_arr, vt_arr]
    if mask is not None:
        in_specs.append(pl.BlockSpec((None, t, t), lambda bi, p, qt, kt: (bi, kt[p], qt[p])))
        args.append(mask)
    grid_spec = pltpu.PrefetchScalarGridSpec(
        num_scalar_prefetch=2,
        grid=(b, n * (n + 1) // 2),
        in_specs=in_specs,
        out_specs=pl.BlockSpec((t, wv), lambda bi, p, qt, kt: (bi * n + qt[p], 0)),
        scratch_shapes=[
            pltpu.VMEM((N_HEADS, t), F32),
            pltpu.VMEM((N_HEADS, t), F32),
            pltpu.VMEM((wv, t), F32),
        ],
    )
    return pl.pallas_call(
        functools.partial(_flash_kernel, masked=mask is not None, dk=dk, t=t),
        grid_spec=grid_spec,
        out_shape=jax.ShapeDtypeStruct((b * s, wv), BF16),
        compiler_params=_params("parallel", "arbitrary"),
        name=name,
    )(qt, kt, *args)


def _count(sc_ref, nvalid, cand, tk, tq, ind_fn):
    group = _count_group(sc_ref.shape[0])

    lanes = COUNT_ACCS * 8

    def body(g, a):
        for e in range(group):
            c = g * group + e
            ind = ind_fn(sc_ref[c], cand, c)
            a = a + jnp.sum(ind.reshape(tk // lanes, lanes, tq), axis=0)
        return a

    part = lax.fori_loop(0, (nvalid + group - 1) // group, body, jnp.zeros((lanes, tq), jnp.int32))
    return jnp.sum(part, axis=0, keepdims=True)


def _count_group(nchunks):
    return 2 if nchunks % 2 == 0 else 1


def _indexer_kernel(q_ref, wt_ref, k2_ref, mask_ref, sc_ref, qm_scr, j_scr, *, tq, tk, n_sel, idx_bits):
    qi = pl.program_id(1)
    one, zero = jnp.int32(1), jnp.int32(0)
    nchunks = sc_ref.shape[0]
    nvalid = ((qi + 1) * tq + tk - 1) // tk
    qcol = qi * tq + lax.broadcasted_iota(jnp.int32, (1, tq), 1)
    krow0 = lax.broadcasted_iota(jnp.int32, (tk, 1), 0)

    half = lax.broadcasted_iota(jnp.int32, (tq, LANES), 1) // IDX_DIM
    for g in range(IDX_HEADS // 2):
        qp = q_ref[:, g * LANES:(g + 1) * LANES]
        for e in range(2):
            qm_scr[2 * g + e] = jnp.where(half == e, qp, jnp.zeros_like(qp))

    def score_chunk(c, kmax):
        kc = k2_ref[pl.ds(pl.multiple_of(c * tk, tk), tk), :]
        acc = jnp.zeros((tk, tq), F32)
        for h in range(IDX_HEADS):
            x = lax.dot_general(kc, qm_scr[h], (((1,), (1,)), ((), ())), preferred_element_type=F32)
            acc = acc + wt_ref[h:h + 1, :] * jnp.maximum(x, 0.0)
        score = acc * (IDX_HEADS * IDX_DIM) ** -0.5
        bits = pltpu.bitcast(score, jnp.int32)
        key = bits ^ ((bits >> 31) & 0x7FFFFFFF)
        causal = (c * tk + krow0) <= qcol
        key = jnp.where(causal, key, INT_MIN)
        sc_ref[c] = key
        return jnp.maximum(kmax, jnp.max(key.reshape(tk // 8, 8, tq), axis=0))

    kmax = lax.fori_loop(0, nvalid, score_chunk, jnp.full((8, tq), INT_MIN, jnp.int32))
    kmax = jnp.max(kmax, axis=0, keepdims=True)

    group = _count_group(nchunks)
    if group > 1:
        def pad_chunk(c, carry):
            sc_ref[c] = jnp.full((tk, tq), INT_MIN, jnp.int32)
            return carry

        lax.fori_loop(nvalid, (nvalid + group - 1) // group * group, pad_chunk, 0)

    def count_ge(cand):
        return _count(sc_ref, nvalid, cand, tk, tq, lambda k, cd, c: jnp.where(k >= cd, one, zero))

    n_keys = qcol + 1
    f0 = count_ge(jnp.zeros((1, tq), jnp.int32))
    pos = f0 >= n_sel
    lo0 = jnp.where(pos, 0, INT_MIN + 1)
    hi0 = jnp.where(pos, kmax, jnp.minimum(kmax, -1)) + 1
    flo0 = jnp.where(pos, f0, n_keys)

    def searching(flo, w):
        return jnp.where(flo > n_sel, jnp.where(w > 1, 1.0, 0.0), 0.0)

    def bisect_cond(carry):
        it, lo, w, flo = carry
        return (jnp.max(searching(flo, w)) > 0.0) & (it < 32)

    def bisect_body(carry):
        it, lo, w, flo = carry
        active = searching(flo, w) > 0
        off = w >> 1
        cand = lo + off
        cnt = count_ge(cand)
        ok = cnt >= n_sel
        lo = jnp.where(active, jnp.where(ok, cand, lo), lo)
        flo = jnp.where(active, jnp.where(ok, cnt, flo), flo)
        w = jnp.where(active, jnp.where(ok, w - off, off), w)
        return it + 1, lo, w, flo

    _, lo, _, flo = lax.while_loop(bisect_cond, bisect_body, (jnp.int32(0), lo0, hi0 - lo0, flo0))
    found = n_keys >= n_sel
    tie = jnp.where(found, jnp.where(flo > n_sel, one, zero), zero) > 0
    gt_thr = jnp.where(found, jnp.where(tie, lo, lo - 1), INT_MIN)
    thr = jnp.where(tie, lo, INT_MIN + 1)

    j_scr[...] = jnp.full_like(j_scr, 2 ** idx_bits - 1)
    has_tie = jnp.max(jnp.where(tie, 1.0, 0.0)) > 0.0

    @pl.when(has_tie)
    def _():
        cgt = _count(sc_ref, nvalid, thr, tk, tq, lambda k, cd, c: jnp.where(k > cd, one, zero))
        need = n_sel - cgt

        def jbit(i, jv):
            cand = jv | lax.shift_left(jnp.int32(1), idx_bits - 1 - i)
            cnt = _count(sc_ref, nvalid, cand, tk, tq,
                         lambda k, cd, c: jnp.where(k == thr, jnp.where((c * tk + krow0) < cd, one, zero), zero))
            return jnp.where(cnt < need, cand, jv)

        j_scr[...] = lax.fori_loop(0, idx_bits, jbit, jnp.zeros((1, tq), jnp.int32))

    jlast = j_scr[...]

    def write_chunk(c, carry):
        key = sc_ref[c]
        sel = jnp.where(key > gt_thr, one,
                        jnp.where(key == thr, jnp.where((c * tk + krow0) <= jlast, one, zero), zero))
        mask_ref[pl.ds(pl.multiple_of(c * tk, tk), tk), :] = sel.astype(jnp.int8)
        return carry

    lax.fori_loop(0, nvalid, write_chunk, 0)

    def zero_chunk(c, carry):
        mask_ref[pl.ds(pl.multiple_of(c * tk, tk), tk), :] = jnp.zeros((tk, tq), jnp.int8)
        return carry

    lax.fori_loop(nvalid, nchunks, zero_chunk, 0)


def _indexer_mask(zb, wt, k2, b, s, tq, tk, n_sel):
    nq = s // tq
    idx_bits = max(1, int(np.ceil(np.log2(s))))
    return pl.pallas_call(
        functools.partial(_indexer_kernel, tq=tq, tk=tk, n_sel=n_sel, idx_bits=idx_bits),
        grid=(b, nq),
        in_specs=[
            pl.BlockSpec((tq, IDX_HEADS * IDX_DIM), lambda bi, i: (bi * nq + i, COL_IQ // (IDX_HEADS * IDX_DIM))),
            pl.BlockSpec((None, IDX_HEADS, tq), lambda bi, i: (bi, 0, i)),
            pl.BlockSpec((None, s, LANES), lambda bi, i: (bi, 0, 0)),
        ],
        out_specs=pl.BlockSpec((None, s, tq), lambda bi, i: (bi, 0, i)),
        out_shape=jax.ShapeDtypeStruct((b, s, s), jnp.int8),
        scratch_shapes=[
            pltpu.VMEM((s // tk, tk, tq), jnp.int32),
            pltpu.VMEM((IDX_HEADS, tq, LANES), BF16),
            pltpu.VMEM((1, tq), jnp.int32),
        ],
        compiler_params=_params("parallel", "arbitrary"),
        name="indexer_mask",
    )(zb, wt, k2)


def _mla_up_kernel(c_ref, zs_ref, qg_ref, kvg_ref, wq_ref, wk_ref, wv_ref, c64_ref, s64l_ref, s64h_ref,
                   q_ref, k_ref, v_ref, *, q_scale):
    def norm(xf, g):
        y = xf * lax.rsqrt(jnp.mean(xf * xf, axis=-1, keepdims=True) + EPS)
        return (y * g).astype(BF16)

    cq = norm(c_ref[:, :Q_LORA].astype(F32), qg_ref[...])
    ckv = norm(c_ref[:, Q_LORA:].astype(F32), kvg_ref[...])
    q = jnp.dot(cq, wq_ref[...], preferred_element_type=F32) * q_scale
    k = jnp.dot(ckv, wk_ref[...], preferred_element_type=F32)
    v_ref[...] = jnp.dot(ckv, wv_ref[...], preferred_element_type=F32).astype(BF16)
    lane = lax.broadcasted_iota(jnp.int32, (1, LANES), 1)
    k_rot = jnp.where(lane < QK_ROPE, zs_ref[:, LANES:], 0.0)
    cos, sin_lo, sin_hi = c64_ref[...], s64l_ref[...], s64h_ref[...]
    for h in range(N_HEADS):
        nope = slice(h * MLA_QK_PAD, h * MLA_QK_PAD + QK_NOPE)
        rope = slice(h * MLA_QK_PAD + QK_NOPE, (h + 1) * MLA_QK_PAD)
        q_ref[:, nope] = q[:, nope].astype(BF16)
        q_ref[:, rope] = _rope64(q[:, rope], cos, sin_lo, sin_hi).astype(BF16)
        k_ref[:, nope] = k[:, nope].astype(BF16)
        k_ref[:, rope] = k_rot.astype(BF16)


def _mla_up(zb, zs, qg, kvg, wq, wk, wv, tabs, tm):
    m = zb.shape[0]
    _, _, c64, s64l, s64h = tabs
    nsb = c64.shape[0] // tm
    tab_spec = pl.BlockSpec((tm, LANES), lambda i: (i % nsb, 0))
    wqk = N_HEADS * MLA_QK_PAD
    full = lambda a: pl.BlockSpec(a.shape, lambda i: (0,) * a.ndim)
    return pl.pallas_call(
        functools.partial(_mla_up_kernel, q_scale=(QK_NOPE + QK_ROPE) ** -0.5 * LOG2E),
        grid=(m // tm,),
        in_specs=[
            pl.BlockSpec((tm, Q_LORA + KV_LORA), lambda i: (i, COL_CQ // (Q_LORA + KV_LORA))),
            pl.BlockSpec((tm, 2 * LANES), lambda i: (i, 0)),
            full(qg), full(kvg), full(wq), full(wk), full(wv),
            tab_spec, tab_spec, tab_spec,
        ],
        out_specs=[
            pl.BlockSpec((tm, wqk), lambda i: (i, 0)),
            pl.BlockSpec((tm, wqk), lambda i: (i, 0)),
            pl.BlockSpec((tm, BRANCH_WIDTH), lambda i: (i, 0)),
        ],
        out_shape=[
            jax.ShapeDtypeStruct((m, wqk), BF16),
            jax.ShapeDtypeStruct((m, wqk), BF16),
            jax.ShapeDtypeStruct((m, BRANCH_WIDTH), BF16),
        ],
        compiler_params=_params("parallel"),
        name="mla_up",
    )(zb, zs, qg, kvg, wq, wk, wv, c64, s64l, s64h)


def _branch_kernel(o0_ref, o1_ref, o2_ref, g0_ref, g1_ref, g2_ref, m0_ref, m1_ref, m2_ref, w_ref, y_ref):
    y = None
    for i, (o_ref, g_ref, m_ref) in enumerate(((o0_ref, g0_ref, m0_ref), (o1_ref, g1_ref, m1_ref),
                                               (o2_ref, g2_ref, m2_ref))):
        g = g_ref[...].astype(F32)
        a = (o_ref[...].astype(F32) * (g * jax.nn.sigmoid(g))).astype(BF16)
        term = jax.nn.sigmoid(m_ref[...].astype(F32)) * jnp.dot(a, w_ref[i], preferred_element_type=F32)
        y = term if y is None else y + term
    y_ref[...] = y.astype(BF16)


def _branch_merge(o_fox, o_dsa, o_mla, zb, wb, d, tm, tn):
    m = zb.shape[0]
    o_spec = pl.BlockSpec((tm, BRANCH_WIDTH), lambda i, j: (i, 0))
    silu0 = (COL_MERGE + N_BRANCH * d) // BRANCH_WIDTH

    def g_spec(i_br):
        return pl.BlockSpec((tm, BRANCH_WIDTH), lambda i, j: (i, silu0 + i_br))

    def m_spec(i_br):
        base = (COL_MERGE + i_br * d) // tn
        return pl.BlockSpec((tm, tn), lambda i, j: (i, base + j))

    return pl.pallas_call(
        _branch_kernel,
        grid=(m // tm, d // tn),
        in_specs=[o_spec, o_spec, o_spec, g_spec(0), g_spec(1), g_spec(2), m_spec(0), m_spec(1), m_spec(2),
                  pl.BlockSpec((N_BRANCH, BRANCH_WIDTH, tn), lambda i, j: (0, 0, j))],
        out_specs=pl.BlockSpec((tm, tn), lambda i, j: (i, j)),
        out_shape=jax.ShapeDtypeStruct((m, d), BF16),
        compiler_params=_params("parallel", "arbitrary"),
        name="branch_merge",
    )(o_fox, o_dsa, o_mla, zb, zb, zb, zb, zb, zb, wb)


def _out_kernel(x_ref, y_ref, w_ref, g_ref, o_ref, *, final_norm):
    r = x_ref[...] + jnp.dot(y_ref[...], w_ref[...], preferred_element_type=F32)
    if final_norm:
        r = r * lax.rsqrt(jnp.mean(r * r, axis=-1, keepdims=True) + EPS) * g_ref[...]
    o_ref[...] = r


def _out_proj(x2, y, w, g, tm, final_norm):
    m, d = x2.shape
    return pl.pallas_call(
        functools.partial(_out_kernel, final_norm=final_norm),
        grid=(m // tm,),
        in_specs=[
            pl.BlockSpec((tm, d), lambda i: (i, 0)),
            pl.BlockSpec((tm, d), lambda i: (i, 0)),
            pl.BlockSpec((d, d), lambda i: (0, 0)),
            pl.BlockSpec((1, d), lambda i: (0, 0)),
        ],
        out_specs=pl.BlockSpec((tm, d), lambda i: (i, 0)),
        out_shape=jax.ShapeDtypeStruct((m, d), F32),
        compiler_params=_params("parallel"),
        name="out_proj",
    )(x2, y, w, g)


def _split_w_in(w_in, d):
    sizes = (BRANCH_WIDTH, BRANCH_WIDTH, BRANCH_WIDTH, N_HEADS, BRANCH_WIDTH, BRANCH_WIDTH, BRANCH_WIDTH,
             IDX_HEADS * IDX_DIM, IDX_DIM, IDX_HEADS, Q_LORA, KV_LORA, QK_ROPE,
             N_BRANCH * BRANCH_WIDTH, N_BRANCH * d)
    pts = np.cumsum(sizes)[:-1].tolist()
    (f_q, f_k, f_v, f_gate, d_q, d_k, d_v, i_q, i_k, i_w, c_q, c_kv, k_pe, silu_g, merge_g) = jnp.split(
        w_in, pts, axis=-1)
    big = jnp.concatenate([f_q, f_k, f_v, d_q, d_k, d_v, i_q, c_q, c_kv, merge_g, silu_g], axis=-1).astype(BF16)
    pad = jnp.zeros((w_in.shape[0], LANES - N_HEADS - IDX_HEADS), w_in.dtype)
    small = jnp.concatenate([f_gate, i_w, pad, k_pe, i_k], axis=-1).astype(BF16)
    return big, small


def _pad_heads(w, take, width_in):
    r = w.shape[0]
    w3 = w.reshape(r, N_HEADS, width_in)[:, :, take]
    w3 = jnp.pad(w3, ((0, 0), (0, 0), (0, MLA_QK_PAD - w3.shape[-1])))
    return w3.reshape(r, N_HEADS * MLA_QK_PAD)


def _tile(s, pref):
    return pref if s % pref == 0 else s


def _feature_major(a, col, b, s):
    v = a[:, col:col + BRANCH_WIDTH].reshape(b, s, N_HEADS, HEAD_DIM)
    vt = jnp.transpose(v, (0, 2, 3, 1))
    ones = jnp.ones((b, N_HEADS, VT_ROWS - HEAD_DIM, s), vt.dtype)
    return jnp.concatenate([vt, ones], axis=2).reshape(b, N_HEADS * VT_ROWS, s)


def _fox_operands(zb, neg_cum_pieces):
    m = zb.shape[0]
    q = zb[:, COL_FQ:COL_FQ + BRANCH_WIDTH].reshape(m, N_HEADS, HEAD_DIM)
    k = zb[:, COL_FK:COL_FK + BRANCH_WIDTH].reshape(m, N_HEADS, HEAD_DIM)
    pad = jnp.zeros((m, N_HEADS, FOX_QK_PAD - HEAD_DIM - 3), zb.dtype)
    bias = jnp.stack(neg_cum_pieces, axis=-1)
    q_aug = jnp.concatenate([q, jnp.ones((m, N_HEADS, 3), zb.dtype), pad], axis=-1)
    k_aug = jnp.concatenate([k, bias, pad], axis=-1)
    return q_aug.reshape(m, N_HEADS * FOX_QK_PAD), k_aug.reshape(m, N_HEADS * FOX_QK_PAD)


def kernel(x, norm_g, w_in, forget_b, q_norm_g, w_q_up, kv_norm_g, w_kv_up, w_branch, w_out, final_norm_g):
    b, s, d = x.shape
    depth = w_in.shape[0]
    n_sel = min(TOPK_MAX, s // 4)
    m = b * s
    tabs = _rope_tables(s)
    t_att = _tile(s, 512)
    tm_in = _tile(s, 1024)
    x2 = x.reshape(m, d)
    for l in range(depth):
        w_big, w_small = _split_w_in(w_in[l], d)
        zb, zs = _in_proj(x2, norm_g[l].reshape(1, d), w_big, w_small, tabs, tm_in, 512)

        cum = _forget_cumsum(zs, forget_b[l].reshape(1, N_HEADS), b, s, _tile(s, 1024))
        q_fox, k_fox = _fox_operands(zb, cum)
        o_fox = _flash(q_fox, 0, k_fox, 0, _feature_major(zb, COL_FV, b, s), b, s, t_att, FOX_QK_PAD, "flash_fox")

        zs3 = zs.reshape(b, s, 2 * LANES)
        wt = jnp.swapaxes(zs3[:, :, N_HEADS:N_HEADS + IDX_HEADS], 1, 2)
        ik = zs3[:, :, LANES + QK_ROPE:].astype(BF16)
        k2 = jnp.concatenate([ik, ik], axis=-1)
        mask = _indexer_mask(zb, wt, k2, b, s, _tile(s, 256), _tile(s, 512), n_sel)
        o_dsa = _flash(zb, COL_DQ // BRANCH_WIDTH, zb, COL_DK // BRANCH_WIDTH, _feature_major(zb, COL_DV, b, s),
                       b, s, t_att, HEAD_DIM, "flash_mask", mask)

        wq = _pad_heads(w_q_up[l], slice(0, QK_NOPE + QK_ROPE), QK_NOPE + QK_ROPE).astype(BF16)
        wk = _pad_heads(w_kv_up[l], slice(0, QK_NOPE), QK_NOPE + V_DIM).astype(BF16)
        wv = w_kv_up[l].reshape(KV_LORA, N_HEADS, QK_NOPE + V_DIM)[:, :, QK_NOPE:].reshape(
            KV_LORA, BRANCH_WIDTH).astype(BF16)
        q_mla, k_mla, v_mla = _mla_up(zb, zs, q_norm_g[l].reshape(1, Q_LORA), kv_norm_g[l].reshape(1, KV_LORA),
                                      wq, wk, wv, tabs, _tile(s, 512))
        o_mla = _flash(q_mla, 0, k_mla, 0, _feature_major(v_mla, 0, b, s), b, s, t_att, MLA_QK_PAD, "flash_plain")

        y = _branch_merge(o_fox, o_dsa, o_mla, zb, w_branch[l].astype(BF16), d, _tile(s, 512), 1024)
        last = l == depth - 1
        x2 = _out_proj(x2, y, w_out[l].astype(BF16), final_norm_g.reshape(1, d), _tile(s, 512), last)
    return x2.reshape(b, s, d)
```

```python
import functools

import jax
import jax.numpy as jnp
import numpy as np
from jax import lax
from jax.experimental import pallas as pl
from jax.experimental.pallas import tpu as pltpu

F32 = jnp.float32
BF16 = jnp.bfloat16

HEAD_DIM = 128
N_HEADS = 8
BRANCH_WIDTH = N_HEADS * HEAD_DIM
N_BRANCH = 3
IDX_HEADS = 16
IDX_DIM = 64
TOPK_MAX = 256
Q_LORA = 512
KV_LORA = 512
QK_NOPE = 128
QK_ROPE = 64
V_DIM = 128
MLA_QK_PAD = 256
FOX_BIAS_COLS = 3 * N_HEADS
ONES_ROWS = 16
ROPE_THETA = 10000.0
EPS = 1e-6

LANES = 128
VMEM_LIMIT = 56 * 1024 * 1024
NEG_BIG = -1e30
INT_MIN = -(2 ** 31)
LOG2E = 1.4426950408889634
Q_STRIP = 256
COUNT_ACCS = 4
PIPE_LEAD = 2
PIPE_LAG = 2

COL_FQ, COL_FK, COL_FV = 0, 1024, 2048
COL_DQ, COL_DK, COL_DV = 3072, 4096, 5120
COL_IQ = 6144
COL_CQ = 7168
COL_MERGE = 8192


def _params(*sem):
    return pltpu.CompilerParams(dimension_semantics=sem, vmem_limit_bytes=VMEM_LIMIT)


def _rope_tables(s):
    pos = jnp.arange(s, dtype=F32)

    def cs(half):
        inv = ROPE_THETA ** (-jnp.arange(half, dtype=F32) / half)
        ang = pos[:, None] * inv[None, :]
        return jnp.cos(ang), jnp.sin(ang)

    c64, s64 = cs(64)
    c32, s32 = cs(32)
    z32 = jnp.zeros_like(s32)
    cos128 = jnp.concatenate([c64, c64], axis=-1)
    sin128 = jnp.concatenate([-s64, s64], axis=-1)
    cos64 = jnp.concatenate([c32, c32, c32, c32], axis=-1)
    sin64_lo = jnp.concatenate([-s32, z32, -s32, z32], axis=-1)
    sin64_hi = jnp.concatenate([z32, s32, z32, s32], axis=-1)
    return cos128, sin128, cos64, sin64_lo, sin64_hi


def _rope128(x, cos, sin):
    return x * cos + pltpu.roll(x, 64, 1) * sin


def _rope64(x, cos, sin_lo, sin_hi):
    return x * cos + pltpu.roll(x, 96, 1) * sin_lo + pltpu.roll(x, 32, 1) * sin_hi


def _in_proj_kernel(x_ref, g_ref, w_ref, ws_ref, c128_ref, s128_ref, c64_ref, s64l_ref, s64h_ref,
                    zb_ref, zs_ref, vt_ref, h_scr, *, tn, q_scale):
    j = pl.program_id(1)

    @pl.when(j == 0)
    def _():
        xf = x_ref[...]
        y = xf * lax.rsqrt(jnp.mean(xf * xf, axis=-1, keepdims=True) + EPS)
        h = (y * g_ref[...]).astype(BF16)
        h_scr[...] = h
        zs = jnp.dot(h, ws_ref[...], preferred_element_type=F32)
        zs_ref[:, :LANES] = zs[:, :LANES]
        zs_ref[:, LANES:] = _rope64(zs[:, LANES:], c64_ref[...], s64l_ref[...], s64h_ref[...])

    acc = jnp.dot(h_scr[...], w_ref[...], preferred_element_type=F32)
    col = j * tn
    is_q = ((col >= COL_FQ) & (col < COL_FK)) | ((col >= COL_DQ) & (col < COL_DK))
    acc = acc * jnp.where(is_q, q_scale, 1.0).astype(F32)
    is_r128 = (col >= COL_DQ) & (col < COL_DV)
    is_r64 = (col >= COL_IQ) & (col < COL_CQ)

    @pl.when(is_r128)
    def _():
        for g in range(tn // LANES):
            sl = slice(g * LANES, (g + 1) * LANES)
            zb_ref[:, sl] = _rope128(acc[:, sl], c128_ref[...], s128_ref[...]).astype(BF16)

    @pl.when(is_r64)
    def _():
        for g in range(tn // LANES):
            sl = slice(g * LANES, (g + 1) * LANES)
            zb_ref[:, sl] = _rope64(acc[:, sl], c64_ref[...], s64l_ref[...], s64h_ref[...]).astype(BF16)

    @pl.when(jnp.logical_not(is_r128 | is_r64))
    def _():
        zb_ref[...] = acc.astype(BF16)

    is_v = ((col >= COL_FV) & (col < COL_DQ)) | ((col >= COL_DV) & (col < COL_IQ))

    @pl.when(is_v)
    def _():
        vt_ref[...] = acc.T.astype(BF16)


def _vt_row_block(j, tn):
    per = BRANCH_WIDTH // tn
    fv0, dv0 = COL_FV // tn, COL_DV // tn
    in_fox = jnp.clip(j - fv0, 0, per - 1)
    in_dsa = per + jnp.clip(j - dv0, 0, per - 1)
    return jnp.where(j < dv0, in_fox, in_dsa)


def _in_proj(x2, g, w_big, w_small, tabs, tm, tn):
    m, d = x2.shape
    n = w_big.shape[1]
    c128, s128, c64, s64l, s64h = tabs
    s = c128.shape[0]
    nsb = s // tm
    tab_spec = pl.BlockSpec((tm, LANES), lambda i, j: (i % nsb, 0))
    return pl.pallas_call(
        functools.partial(_in_proj_kernel, tn=tn, q_scale=HEAD_DIM ** -0.5 * LOG2E),
        grid=(m // tm, n // tn),
        in_specs=[
            pl.BlockSpec((tm, d), lambda i, j: (i, 0)),
            pl.BlockSpec((1, d), lambda i, j: (0, 0)),
            pl.BlockSpec((d, tn), lambda i, j: (0, j)),
            pl.BlockSpec((d, 2 * LANES), lambda i, j: (0, 0)),
            tab_spec, tab_spec, tab_spec, tab_spec, tab_spec,
        ],
        out_specs=[
            pl.BlockSpec((tm, tn), lambda i, j: (i, j)),
            pl.BlockSpec((tm, 2 * LANES), lambda i, j: (i, 0)),
            pl.BlockSpec((None, tn, tm), lambda i, j: (i // nsb, _vt_row_block(j, tn), i % nsb)),
        ],
        out_shape=[
            jax.ShapeDtypeStruct((m, n), BF16),
            jax.ShapeDtypeStruct((m, 2 * LANES), F32),
            jax.ShapeDtypeStruct((m // s, 2 * BRANCH_WIDTH, s), BF16),
        ],
        scratch_shapes=[pltpu.VMEM((tm, d), BF16)],
        compiler_params=_params("parallel", "arbitrary"),
        name="in_proj",
    )(x2, g, w_big, w_small, c128, s128, c64, s64l, s64h)


def _bf16_floor(x):
    return pltpu.bitcast(pltpu.bitcast(x, jnp.int32) & jnp.int32(-65536), F32)


def _cum_kernel(zs_ref, fb_ref, kb_ref, carry_scr, *, t):
    @pl.when(pl.program_id(1) == 0)
    def _():
        carry_scr[...] = jnp.zeros_like(carry_scr)

    lane = lax.broadcasted_iota(jnp.int32, (1, LANES), 1)
    x = zs_ref[:, :LANES] + fb_ref[...]
    logf = -(jnp.maximum(-x, 0.0) + jnp.log1p(jnp.exp(-jnp.abs(x))))
    logf = jnp.where(lane < FOX_BIAS_COLS, logf, 0.0)
    r = lax.broadcasted_iota(jnp.int32, (t, t), 0)
    c = lax.broadcasted_iota(jnp.int32, (t, t), 1)
    tri = jnp.where(c <= r, 1.0, 0.0).astype(F32)
    cum = jnp.dot(tri, logf, preferred_element_type=F32, precision=lax.Precision.HIGHEST) + carry_scr[...]
    carry_scr[...] = cum[t - 1:t, :]
    c = cum * (-LOG2E)
    hi = _bf16_floor(c)
    mid = _bf16_floor(c - hi)
    lo = (c - hi) - mid
    kb_ref[...] = jnp.where(lane < N_HEADS, hi, jnp.where(lane < 2 * N_HEADS, mid, lo)).astype(BF16)


def _forget_cumsum(zs, fb, b, s, t):
    fb128 = jnp.pad(jnp.tile(fb, (1, 3)), ((0, 0), (0, LANES - FOX_BIAS_COLS)))
    return pl.pallas_call(
        functools.partial(_cum_kernel, t=t),
        grid=(b, s // t),
        in_specs=[
            pl.BlockSpec((t, 2 * LANES), lambda bi, i: (bi * (s // t) + i, 0)),
            pl.BlockSpec((1, LANES), lambda bi, i: (0, 0)),
        ],
        out_specs=pl.BlockSpec((t, LANES), lambda bi, i: (bi * (s // t) + i, 0)),
        out_shape=jax.ShapeDtypeStruct((b * s, LANES), BF16),
        scratch_shapes=[pltpu.VMEM((1, LANES), F32)],
        compiler_params=_params("parallel", "arbitrary"),
        name="forget_cumsum",
    )(zs, fb128)


def _flash_kernel(qt_ref, kt_ref, q_ref, k_ref, v_ref, *rest, extra, dk, t):
    if extra is None:
        o_ref, m_scr, l_scr, acc_scr = rest
        extra_ref = None
    else:
        extra_ref, o_ref, m_scr, l_scr, acc_scr = rest
    masked = extra == "mask"
    p = pl.program_id(1)
    qi = qt_ref[p]
    ki = kt_ref[p]
    qs = min(t, Q_STRIP)

    @pl.when(ki == 0)
    def _():
        m_scr[...] = jnp.full_like(m_scr, NEG_BIG)
        l_scr[...] = jnp.zeros_like(l_scr)
        acc_scr[...] = jnp.zeros_like(acc_scr)

    def step(diag):
        units = [(c, h) for c in range(t // qs) for h in range(N_HEADS)]

        def n_keys(c):
            return (c + 1) * qs if diag else t

        def logits(c, h):
            k_h = k_ref[:n_keys(c), h * dk:(h + 1) * dk]
            q_h = q_ref[c * qs:(c + 1) * qs, h * dk:(h + 1) * dk]
            if extra == "key_bias":
                lane = lax.broadcasted_iota(jnp.int32, (qs, LANES), 1)
                pick = jnp.where(lane < FOX_BIAS_COLS, jnp.where(lane % N_HEADS == h, 1.0, 0.0), 0.0)
                k_h = jnp.concatenate([k_h, extra_ref[:n_keys(c), :]], axis=1)
                q_h = jnp.concatenate([q_h, pick.astype(q_h.dtype)], axis=1)
            return lax.dot_general(k_h, q_h, (((1,), (1,)), ((), ())), preferred_element_type=F32)

        def keep_mask(c):
            nk = n_keys(c)
            if masked:
                return extra_ref[:, c * qs:(c + 1) * qs].astype(jnp.int32) != 0
            if diag:
                kr = lax.broadcasted_iota(jnp.int32, (nk, qs), 0)
                qc = lax.broadcasted_iota(jnp.int32, (nk, qs), 1) + c * qs
                return kr <= qc
            return None

        keeps = {}

        def softmax(c, h, st):
            cs = slice(c * qs, (c + 1) * qs)
            if c not in keeps:
                keeps[c] = keep_mask(c)
            if keeps[c] is not None:
                st = jnp.where(keeps[c], st, NEG_BIG)
            m_prev = m_scr[h:h + 1, cs]
            m_new = jnp.maximum(m_prev, jnp.max(st, axis=0, keepdims=True))
            m_scr[h:h + 1, cs] = m_new
            alpha = jnp.exp2(m_prev - m_new)
            return alpha, jnp.exp2(st - m_new).astype(BF16)

        def accumulate(c, h, alpha, pt):
            cs = slice(c * qs, (c + 1) * qs)
            nk = n_keys(c)
            ones = jnp.ones((ONES_ROWS, nk), v_ref.dtype)
            vt_h = jnp.concatenate([v_ref[h * HEAD_DIM:(h + 1) * HEAD_DIM, :nk], ones], axis=0)
            pv = jnp.dot(vt_h, pt, preferred_element_type=F32)
            sl = slice(h * HEAD_DIM, (h + 1) * HEAD_DIM)
            acc_scr[sl, cs] = alpha * acc_scr[sl, cs] + pv[:HEAD_DIM]
            l_scr[h:h + 1, cs] = alpha * l_scr[h:h + 1, cs] + pv[HEAD_DIM:HEAD_DIM + 1]

        n_u = len(units)
        st_q = [logits(*units[i]) for i in range(min(PIPE_LEAD, n_u))]
        p_q = []
        for u in range(n_u + PIPE_LAG):
            if u + PIPE_LEAD < n_u:
                st_q.append(logits(*units[u + PIPE_LEAD]))
            if u < n_u:
                p_q.append(softmax(*units[u], st_q.pop(0)))
            if u >= PIPE_LAG:
                accumulate(*units[u - PIPE_LAG], *p_q.pop(0))

    if masked:
        step(False)
    else:
        @pl.when(ki == qi)
        def _():
            step(True)

        @pl.when(ki != qi)
        def _():
            step(False)

    @pl.when(ki == qi)
    def _():
        for h in range(N_HEADS):
            sl = slice(h * HEAD_DIM, (h + 1) * HEAD_DIM)
            o = acc_scr[sl, :] / l_scr[h:h + 1, :]
            o_ref[:, sl] = o.T.astype(BF16)


def _pair_tables(n):
    qt = np.concatenate([np.full(i + 1, i, np.int32) for i in range(n)])
    kt = np.concatenate([np.arange(i + 1, dtype=np.int32) for i in range(n)])
    return jnp.asarray(qt), jnp.asarray(kt)


def _flash(q_arr, q_col, k_arr, k_col, vt_arr, vt_blk, b, s, t, dk, name, mask=None, key_bias=None):
    n = s // t
    qt, kt = _pair_tables(n)
    wq = N_HEADS * dk
    wv = N_HEADS * HEAD_DIM
    in_specs = [
        pl.BlockSpec((t, wq), lambda bi, p, qt, kt: (bi * n + qt[p], q_col)),
        pl.BlockSpec((t, wq), lambda bi, p, qt, kt: (bi * n + kt[p], k_col)),
        pl.BlockSpec((None, wv, t), lambda bi, p, qt, kt: (bi, vt_blk, kt[p])),
    ]
    args = [q_arr, k_arr, vt_arr]
    extra = None
    if mask is not None:
        extra = "mask"
        in_specs.append(pl.BlockSpec((None, t, t), lambda bi, p, qt, kt: (bi, kt[p], qt[p])))
        args.append(mask)
    elif key_bias is not None:
        extra = "key_bias"
        in_specs.append(pl.BlockSpec((t, LANES), lambda bi, p, qt, kt: (bi * n + kt[p], 0)))
        args.append(key_bias)
    grid_spec = pltpu.PrefetchScalarGridSpec(
        num_scalar_prefetch=2,
        grid=(b, n * (n + 1) // 2),
        in_specs=in_specs,
        out_specs=pl.BlockSpec((t, wv), lambda bi, p, qt, kt: (bi * n + qt[p], 0)),
        scratch_shapes=[
            pltpu.VMEM((N_HEADS, t), F32),
            pltpu.VMEM((N_HEADS, t), F32),
            pltpu.VMEM((wv, t), F32),
        ],
    )
    return pl.pallas_call(
        functools.partial(_flash_kernel, extra=extra, dk=dk, t=t),
        grid_spec=grid_spec,
        out_shape=jax.ShapeDtypeStruct((b * s, wv), BF16),
        compiler_params=_params("parallel", "arbitrary"),
        name=name,
    )(qt, kt, *args)


def _count(sc_ref, nvalid, cand, tk, tq, ind_fn):
    group = _count_group(sc_ref.shape[0])

    lanes = COUNT_ACCS * 8

    def body(g, a):
        for e in range(group):
            c = g * group + e
            ind = ind_fn(sc_ref[c], cand, c)
            a = a + jnp.sum(ind.reshape(tk // lanes, lanes, tq), axis=0)
        return a

    part = lax.fori_loop(0, (nvalid + group - 1) // group, body, jnp.zeros((lanes, tq), jnp.int32))
    return jnp.sum(part, axis=0, keepdims=True)


def _count_group(nchunks):
    return 2 if nchunks % 2 == 0 else 1


def _indexer_kernel(q_ref, wt_ref, k2_ref, mask_ref, sc_ref, qm_scr, j_scr, *, tq, tk, n_sel, idx_bits):
    qi = pl.program_id(1)
    one, zero = jnp.int32(1), jnp.int32(0)
    nchunks = sc_ref.shape[0]
    nvalid = ((qi + 1) * tq + tk - 1) // tk
    qcol = qi * tq + lax.broadcasted_iota(jnp.int32, (1, tq), 1)
    krow0 = lax.broadcasted_iota(jnp.int32, (tk, 1), 0)

    half = lax.broadcasted_iota(jnp.int32, (tq, LANES), 1) // IDX_DIM
    for g in range(IDX_HEADS // 2):
        qp = q_ref[:, g * LANES:(g + 1) * LANES]
        for e in range(2):
            qm_scr[2 * g + e] = jnp.where(half == e, qp, jnp.zeros_like(qp))

    def score_chunk(c, kmax):
        kc = k2_ref[pl.ds(pl.multiple_of(c * tk, tk), tk), :]
        acc = jnp.zeros((tk, tq), F32)
        for h in range(IDX_HEADS):
            x = lax.dot_general(kc, qm_scr[h], (((1,), (1,)), ((), ())), preferred_element_type=F32)
            acc = acc + wt_ref[h:h + 1, :] * jnp.maximum(x, 0.0)
        score = acc * (IDX_HEADS * IDX_DIM) ** -0.5
        bits = pltpu.bitcast(score, jnp.int32)
        key = bits ^ ((bits >> 31) & 0x7FFFFFFF)
        causal = (c * tk + krow0) <= qcol
        key = jnp.where(causal, key, INT_MIN)
        sc_ref[c] = key
        return jnp.maximum(kmax, jnp.max(key.reshape(tk // 8, 8, tq), axis=0))

    kmax = lax.fori_loop(0, nvalid, score_chunk, jnp.full((8, tq), INT_MIN, jnp.int32))
    kmax = jnp.max(kmax, axis=0, keepdims=True)

    group = _count_group(nchunks)
    if group > 1:
        def pad_chunk(c, carry):
            sc_ref[c] = jnp.full((tk, tq), INT_MIN, jnp.int32)
            return carry

        lax.fori_loop(nvalid, (nvalid + group - 1) // group * group, pad_chunk, 0)

    def count_ge(cand):
        return _count(sc_ref, nvalid, cand, tk, tq, lambda k, cd, c: jnp.where(k >= cd, one, zero))

    n_keys = qcol + 1
    f0 = count_ge(jnp.zeros((1, tq), jnp.int32))
    pos = f0 >= n_sel
    lo0 = jnp.where(pos, 0, INT_MIN + 1)
    hi0 = jnp.where(pos, kmax, jnp.minimum(kmax, -1)) + 1
    flo0 = jnp.where(pos, f0, n_keys)

    def searching(flo, w):
        return jnp.where(flo > n_sel, jnp.where(w > 1, 1.0, 0.0), 0.0)

    def bisect_cond(carry):
        it, lo, w, flo = carry
        return (jnp.max(searching(flo, w)) > 0.0) & (it < 32)

    def bisect_body(carry):
        it, lo, w, flo = carry
        active = searching(flo, w) > 0
        off = w >> 1
        cand = lo + off
        cnt = count_ge(cand)
        ok = cnt >= n_sel
        lo = jnp.where(active, jnp.where(ok, cand, lo), lo)
        flo = jnp.where(active, jnp.where(ok, cnt, flo), flo)
        w = jnp.where(active, jnp.where(ok, w - off, off), w)
        return it + 1, lo, w, flo

    _, lo, _, flo = lax.while_loop(bisect_cond, bisect_body, (jnp.int32(0), lo0, hi0 - lo0, flo0))
    found = n_keys >= n_sel
    tie = jnp.where(found, jnp.where(flo > n_sel, one, zero), zero) > 0
    gt_thr = jnp.where(found, jnp.where(tie, lo, lo - 1), INT_MIN)
    thr = jnp.where(tie, lo, INT_MIN + 1)

    j_scr[...] = jnp.full_like(j_scr, 2 ** idx_bits - 1)
    has_tie = jnp.max(jnp.where(tie, 1.0, 0.0)) > 0.0

    @pl.when(has_tie)
    def _():
        cgt = _count(sc_ref, nvalid, thr, tk, tq, lambda k, cd, c: jnp.where(k > cd, one, zero))
        need = n_sel - cgt

        def jbit(i, jv):
            cand = jv | lax.shift_left(jnp.int32(1), idx_bits - 1 - i)
            cnt = _count(sc_ref, nvalid, cand, tk, tq,
                         lambda k, cd, c: jnp.where(k == thr, jnp.where((c * tk + krow0) < cd, one, zero), zero))
            return jnp.where(cnt < need, cand, jv)

        j_scr[...] = lax.fori_loop(0, idx_bits, jbit, jnp.zeros((1, tq), jnp.int32))

    jlast = j_scr[...]

    def write_chunk(c, carry):
        key = sc_ref[c]
        sel = jnp.where(key > gt_thr, one,
                        jnp.where(key == thr, jnp.where((c * tk + krow0) <= jlast, one, zero), zero))
        mask_ref[pl.ds(pl.multiple_of(c * tk, tk), tk), :] = sel.astype(jnp.int8)
        return carry

    lax.fori_loop(0, nvalid, write_chunk, 0)

    def zero_chunk(c, carry):
        mask_ref[pl.ds(pl.multiple_of(c * tk, tk), tk), :] = jnp.zeros((tk, tq), jnp.int8)
        return carry

    lax.fori_loop(nvalid, nchunks, zero_chunk, 0)


def _indexer_mask(zb, wt, k2, b, s, tq, tk, n_sel):
    nq = s // tq
    idx_bits = max(1, int(np.ceil(np.log2(s))))
    return pl.pallas_call(
        functools.partial(_indexer_kernel, tq=tq, tk=tk, n_sel=n_sel, idx_bits=idx_bits),
        grid=(b, nq),
        in_specs=[
            pl.BlockSpec((tq, IDX_HEADS * IDX_DIM), lambda bi, i: (bi * nq + i, COL_IQ // (IDX_HEADS * IDX_DIM))),
            pl.BlockSpec((None, IDX_HEADS, tq), lambda bi, i: (bi, 0, i)),
            pl.BlockSpec((None, s, LANES), lambda bi, i: (bi, 0, 0)),
        ],
        out_specs=pl.BlockSpec((None, s, tq), lambda bi, i: (bi, 0, i)),
        out_shape=jax.ShapeDtypeStruct((b, s, s), jnp.int8),
        scratch_shapes=[
            pltpu.VMEM((s // tk, tk, tq), jnp.int32),
            pltpu.VMEM((IDX_HEADS, tq, LANES), BF16),
            pltpu.VMEM((1, tq), jnp.int32),
        ],
        compiler_params=_params("parallel", "arbitrary"),
        name="indexer_mask",
    )(zb, wt, k2)


def _mla_up_kernel(c_ref, zs_ref, qg_ref, kvg_ref, wq_ref, wk_ref, wv_ref, c64_ref, s64l_ref, s64h_ref,
                   q_ref, k_ref, v_ref, *, q_scale):
    def norm(xf, g):
        y = xf * lax.rsqrt(jnp.mean(xf * xf, axis=-1, keepdims=True) + EPS)
        return (y * g).astype(BF16)

    cq = norm(c_ref[:, :Q_LORA].astype(F32), qg_ref[...])
    ckv = norm(c_ref[:, Q_LORA:].astype(F32), kvg_ref[...])
    q = jnp.dot(cq, wq_ref[...], preferred_element_type=F32) * q_scale
    k = jnp.dot(ckv, wk_ref[...], preferred_element_type=F32)
    v_ref[...] = jnp.dot(ckv, wv_ref[...], preferred_element_type=F32).T.astype(BF16)
    lane = lax.broadcasted_iota(jnp.int32, (1, LANES), 1)
    k_rot = jnp.where(lane < QK_ROPE, zs_ref[:, LANES:], 0.0)
    cos, sin_lo, sin_hi = c64_ref[...], s64l_ref[...], s64h_ref[...]
    for h in range(N_HEADS):
        nope = slice(h * MLA_QK_PAD, h * MLA_QK_PAD + QK_NOPE)
        rope = slice(h * MLA_QK_PAD + QK_NOPE, (h + 1) * MLA_QK_PAD)
        q_ref[:, nope] = q[:, nope].astype(BF16)
        q_ref[:, rope] = _rope64(q[:, rope], cos, sin_lo, sin_hi).astype(BF16)
        k_ref[:, nope] = k[:, nope].astype(BF16)
        k_ref[:, rope] = k_rot.astype(BF16)


def _mla_up(zb, zs, qg, kvg, wq, wk, wv, tabs, tm):
    m = zb.shape[0]
    _, _, c64, s64l, s64h = tabs
    nsb = c64.shape[0] // tm
    tab_spec = pl.BlockSpec((tm, LANES), lambda i: (i % nsb, 0))
    wqk = N_HEADS * MLA_QK_PAD
    full = lambda a: pl.BlockSpec(a.shape, lambda i: (0,) * a.ndim)
    return pl.pallas_call(
        functools.partial(_mla_up_kernel, q_scale=(QK_NOPE + QK_ROPE) ** -0.5 * LOG2E),
        grid=(m // tm,),
        in_specs=[
            pl.BlockSpec((tm, Q_LORA + KV_LORA), lambda i: (i, COL_CQ // (Q_LORA + KV_LORA))),
            pl.BlockSpec((tm, 2 * LANES), lambda i: (i, 0)),
            full(qg), full(kvg), full(wq), full(wk), full(wv),
            tab_spec, tab_spec, tab_spec,
        ],
        out_specs=[
            pl.BlockSpec((tm, wqk), lambda i: (i, 0)),
            pl.BlockSpec((tm, wqk), lambda i: (i, 0)),
            pl.BlockSpec((None, BRANCH_WIDTH, tm), lambda i: (i // nsb, 0, i % nsb)),
        ],
        out_shape=[
            jax.ShapeDtypeStruct((m, wqk), BF16),
            jax.ShapeDtypeStruct((m, wqk), BF16),
            jax.ShapeDtypeStruct((m // (nsb * tm), BRANCH_WIDTH, nsb * tm), BF16),
        ],
        compiler_params=_params("parallel"),
        name="mla_up",
    )(zb, zs, qg, kvg, wq, wk, wv, c64, s64l, s64h)


def _branch_kernel(o0_ref, o1_ref, o2_ref, g0_ref, g1_ref, g2_ref, m0_ref, m1_ref, m2_ref, w_ref, y_ref):
    y = None
    for i, (o_ref, g_ref, m_ref) in enumerate(((o0_ref, g0_ref, m0_ref), (o1_ref, g1_ref, m1_ref),
                                               (o2_ref, g2_ref, m2_ref))):
        g = g_ref[...].astype(F32)
        a = (o_ref[...].astype(F32) * (g * jax.nn.sigmoid(g))).astype(BF16)
        term = jax.nn.sigmoid(m_ref[...].astype(F32)) * jnp.dot(a, w_ref[i], preferred_element_type=F32)
        y = term if y is None else y + term
    y_ref[...] = y.astype(BF16)


def _branch_merge(o_fox, o_dsa, o_mla, zb, wb, d, tm, tn):
    m = zb.shape[0]
    o_spec = pl.BlockSpec((tm, BRANCH_WIDTH), lambda i, j: (i, 0))
    silu0 = (COL_MERGE + N_BRANCH * d) // BRANCH_WIDTH

    def g_spec(i_br):
        return pl.BlockSpec((tm, BRANCH_WIDTH), lambda i, j: (i, silu0 + i_br))

    def m_spec(i_br):
        base = (COL_MERGE + i_br * d) // tn
        return pl.BlockSpec((tm, tn), lambda i, j: (i, base + j))

    return pl.pallas_call(
        _branch_kernel,
        grid=(m // tm, d // tn),
        in_specs=[o_spec, o_spec, o_spec, g_spec(0), g_spec(1), g_spec(2), m_spec(0), m_spec(1), m_spec(2),
                  pl.BlockSpec((N_BRANCH, BRANCH_WIDTH, tn), lambda i, j: (0, 0, j))],
        out_specs=pl.BlockSpec((tm, tn), lambda i, j: (i, j)),
        out_shape=jax.ShapeDtypeStruct((m, d), BF16),
        compiler_params=_params("parallel", "arbitrary"),
        name="branch_merge",
    )(o_fox, o_dsa, o_mla, zb, zb, zb, zb, zb, zb, wb)


def _out_kernel(x_ref, y_ref, w_ref, g_ref, o_ref, *, final_norm):
    r = x_ref[...] + jnp.dot(y_ref[...], w_ref[...], preferred_element_type=F32)
    if final_norm:
        r = r * lax.rsqrt(jnp.mean(r * r, axis=-1, keepdims=True) + EPS) * g_ref[...]
    o_ref[...] = r


def _out_proj(x2, y, w, g, tm, final_norm):
    m, d = x2.shape
    return pl.pallas_call(
        functools.partial(_out_kernel, final_norm=final_norm),
        grid=(m // tm,),
        in_specs=[
            pl.BlockSpec((tm, d), lambda i: (i, 0)),
            pl.BlockSpec((tm, d), lambda i: (i, 0)),
            pl.BlockSpec((d, d), lambda i: (0, 0)),
            pl.BlockSpec((1, d), lambda i: (0, 0)),
        ],
        out_specs=pl.BlockSpec((tm, d), lambda i: (i, 0)),
        out_shape=jax.ShapeDtypeStruct((m, d), F32),
        compiler_params=_params("parallel"),
        name="out_proj",
    )(x2, y, w, g)


def _split_w_in(w_in, d):
    sizes = (BRANCH_WIDTH, BRANCH_WIDTH, BRANCH_WIDTH, N_HEADS, BRANCH_WIDTH, BRANCH_WIDTH, BRANCH_WIDTH,
             IDX_HEADS * IDX_DIM, IDX_DIM, IDX_HEADS, Q_LORA, KV_LORA, QK_ROPE,
             N_BRANCH * BRANCH_WIDTH, N_BRANCH * d)
    pts = np.cumsum(sizes)[:-1].tolist()
    (f_q, f_k, f_v, f_gate, d_q, d_k, d_v, i_q, i_k, i_w, c_q, c_kv, k_pe, silu_g, merge_g) = jnp.split(
        w_in, pts, axis=-1)
    big = jnp.concatenate([f_q, f_k, f_v, d_q, d_k, d_v, i_q, c_q, c_kv, merge_g, silu_g], axis=-1).astype(BF16)
    pad = jnp.zeros((w_in.shape[0], LANES - FOX_BIAS_COLS - IDX_HEADS), w_in.dtype)
    small = jnp.concatenate([f_gate, f_gate, f_gate, i_w, pad, k_pe, i_k], axis=-1).astype(BF16)
    return big, small


def _pad_heads(w, take, width_in):
    r = w.shape[0]
    w3 = w.reshape(r, N_HEADS, width_in)[:, :, take]
    w3 = jnp.pad(w3, ((0, 0), (0, 0), (0, MLA_QK_PAD - w3.shape[-1])))
    return w3.reshape(r, N_HEADS * MLA_QK_PAD)


def _tile(s, pref):
    return pref if s % pref == 0 else s


def kernel(x, norm_g, w_in, forget_b, q_norm_g, w_q_up, kv_norm_g, w_kv_up, w_branch, w_out, final_norm_g):
    b, s, d = x.shape
    depth = w_in.shape[0]
    n_sel = min(TOPK_MAX, s // 4)
    m = b * s
    tabs = _rope_tables(s)
    t_att = _tile(s, 512)
    tm_in = _tile(s, 1024)
    x2 = x.reshape(m, d)
    for l in range(depth):
        w_big, w_small = _split_w_in(w_in[l], d)
        zb, zs, vt = _in_proj(x2, norm_g[l].reshape(1, d), w_big, w_small, tabs, tm_in, 512)

        key_bias = _forget_cumsum(zs, forget_b[l].reshape(1, N_HEADS), b, s, _tile(s, 1024))
        o_fox = _flash(zb, COL_FQ // BRANCH_WIDTH, zb, COL_FK // BRANCH_WIDTH, vt, 0,
                       b, s, t_att, HEAD_DIM, "flash_fox", key_bias=key_bias)

        zs3 = zs.reshape(b, s, 2 * LANES)
        wt = jnp.swapaxes(zs3[:, :, FOX_BIAS_COLS:FOX_BIAS_COLS + IDX_HEADS], 1, 2)
        ik = zs3[:, :, LANES + QK_ROPE:].astype(BF16)
        k2 = jnp.concatenate([ik, ik], axis=-1)
        mask = _indexer_mask(zb, wt, k2, b, s, _tile(s, 256), _tile(s, 512), n_sel)
        o_dsa = _flash(zb, COL_DQ // BRANCH_WIDTH, zb, COL_DK // BRANCH_WIDTH, vt, 1,
                       b, s, t_att, HEAD_DIM, "flash_mask", mask=mask)

        wq = _pad_heads(w_q_up[l], slice(0, QK_NOPE + QK_ROPE), QK_NOPE + QK_ROPE).astype(BF16)
        wk = _pad_heads(w_kv_up[l], slice(0, QK_NOPE), QK_NOPE + V_DIM).astype(BF16)
        wv = w_kv_up[l].reshape(KV_LORA, N_HEADS, QK_NOPE + V_DIM)[:, :, QK_NOPE:].reshape(
            KV_LORA, BRANCH_WIDTH).astype(BF16)
        q_mla, k_mla, vt_mla = _mla_up(zb, zs, q_norm_g[l].reshape(1, Q_LORA), kv_norm_g[l].reshape(1, KV_LORA),
                                       wq, wk, wv, tabs, _tile(s, 512))
        o_mla = _flash(q_mla, 0, k_mla, 0, vt_mla, 0, b, s, t_att, MLA_QK_PAD, "flash_plain")

        y = _branch_merge(o_fox, o_dsa, o_mla, zb, w_branch[l].astype(BF16), d, _tile(s, 512), 1024)
        last = l == depth - 1
        x2 = _out_proj(x2, y, w_out[l].astype(BF16), final_norm_g.reshape(1, d), _tile(s, 512), last)
    return x2.reshape(b, s, d)
```

```python
import functools

import jax
import jax.numpy as jnp
import numpy as np
from jax import lax
from jax.experimental import pallas as pl
from jax.experimental.pallas import tpu as pltpu

F32 = jnp.float32
BF16 = jnp.bfloat16

HEAD_DIM = 128
N_HEADS = 8
BRANCH_WIDTH = N_HEADS * HEAD_DIM
N_BRANCH = 3
IDX_HEADS = 16
IDX_DIM = 64
TOPK_MAX = 256
Q_LORA = 512
KV_LORA = 512
QK_NOPE = 128
QK_ROPE = 64
V_DIM = 128
MLA_QK_PAD = 256
FOX_BIAS_COLS = 3 * N_HEADS
UNDERFLOW_LOG2 = 160.0
ONES_ROWS = 16
ROPE_THETA = 10000.0
EPS = 1e-6

LANES = 128
VMEM_LIMIT = 56 * 1024 * 1024
NEG_BIG = -1e30
INT_MIN = -(2 ** 31)
LOG2E = 1.4426950408889634
Q_STRIP = 256
COUNT_ACCS = 4
PIPE_LEAD = 2
PIPE_LAG = 2

COL_FQ, COL_FK, COL_FV = 0, 1024, 2048
COL_DQ, COL_DK, COL_DV = 3072, 4096, 5120
COL_IQ = 6144
COL_CQ = 7168
COL_MERGE = 8192


def _params(*sem):
    return pltpu.CompilerParams(dimension_semantics=sem, vmem_limit_bytes=VMEM_LIMIT)


def _rope_tables(s):
    pos = jnp.arange(s, dtype=F32)

    def cs(half):
        inv = ROPE_THETA ** (-jnp.arange(half, dtype=F32) / half)
        ang = pos[:, None] * inv[None, :]
        return jnp.cos(ang), jnp.sin(ang)

    c64, s64 = cs(64)
    c32, s32 = cs(32)
    z32 = jnp.zeros_like(s32)
    cos128 = jnp.concatenate([c64, c64], axis=-1)
    sin128 = jnp.concatenate([-s64, s64], axis=-1)
    cos64 = jnp.concatenate([c32, c32, c32, c32], axis=-1)
    sin64_lo = jnp.concatenate([-s32, z32, -s32, z32], axis=-1)
    sin64_hi = jnp.concatenate([z32, s32, z32, s32], axis=-1)
    return cos128, sin128, cos64, sin64_lo, sin64_hi


def _rope128(x, cos, sin):
    return x * cos + pltpu.roll(x, 64, 1) * sin


def _rope64(x, cos, sin_lo, sin_hi):
    return x * cos + pltpu.roll(x, 96, 1) * sin_lo + pltpu.roll(x, 32, 1) * sin_hi


def _in_proj_kernel(x_ref, g_ref, w_ref, ws_ref, c128_ref, s128_ref, c64_ref, s64l_ref, s64h_ref,
                    zb_ref, zs_ref, vt_ref, h_scr, *, tn, q_scale):
    j = pl.program_id(1)

    @pl.when(j == 0)
    def _():
        xf = x_ref[...]
        y = xf * lax.rsqrt(jnp.mean(xf * xf, axis=-1, keepdims=True) + EPS)
        h = (y * g_ref[...]).astype(BF16)
        h_scr[...] = h
        zs = jnp.dot(h, ws_ref[...], preferred_element_type=F32)
        zs_ref[:, :LANES] = zs[:, :LANES]
        zs_ref[:, LANES:] = _rope64(zs[:, LANES:], c64_ref[...], s64l_ref[...], s64h_ref[...])

    acc = jnp.dot(h_scr[...], w_ref[...], preferred_element_type=F32)
    col = j * tn
    is_q = ((col >= COL_FQ) & (col < COL_FK)) | ((col >= COL_DQ) & (col < COL_DK))
    acc = acc * jnp.where(is_q, q_scale, 1.0).astype(F32)
    is_r128 = (col >= COL_DQ) & (col < COL_DV)
    is_r64 = (col >= COL_IQ) & (col < COL_CQ)

    @pl.when(is_r128)
    def _():
        for g in range(tn // LANES):
            sl = slice(g * LANES, (g + 1) * LANES)
            zb_ref[:, sl] = _rope128(acc[:, sl], c128_ref[...], s128_ref[...]).astype(BF16)

    @pl.when(is_r64)
    def _():
        for g in range(tn // LANES):
            sl = slice(g * LANES, (g + 1) * LANES)
            zb_ref[:, sl] = _rope64(acc[:, sl], c64_ref[...], s64l_ref[...], s64h_ref[...]).astype(BF16)

    @pl.when(jnp.logical_not(is_r128 | is_r64))
    def _():
        zb_ref[...] = acc.astype(BF16)

    is_v = ((col >= COL_FV) & (col < COL_DQ)) | ((col >= COL_DV) & (col < COL_IQ))

    @pl.when(is_v)
    def _():
        vt_ref[...] = acc.T.astype(BF16)


def _vt_row_block(j, tn):
    per = BRANCH_WIDTH // tn
    fv0, dv0 = COL_FV // tn, COL_DV // tn
    in_fox = jnp.clip(j - fv0, 0, per - 1)
    in_dsa = per + jnp.clip(j - dv0, 0, per - 1)
    return jnp.where(j < dv0, in_fox, in_dsa)


def _in_proj(x2, g, w_big, w_small, tabs, tm, tn):
    m, d = x2.shape
    n = w_big.shape[1]
    c128, s128, c64, s64l, s64h = tabs
    s = c128.shape[0]
    nsb = s // tm
    tab_spec = pl.BlockSpec((tm, LANES), lambda i, j: (i % nsb, 0))
    return pl.pallas_call(
        functools.partial(_in_proj_kernel, tn=tn, q_scale=HEAD_DIM ** -0.5 * LOG2E),
        grid=(m // tm, n // tn),
        in_specs=[
            pl.BlockSpec((tm, d), lambda i, j: (i, 0)),
            pl.BlockSpec((1, d), lambda i, j: (0, 0)),
            pl.BlockSpec((d, tn), lambda i, j: (0, j)),
            pl.BlockSpec((d, 2 * LANES), lambda i, j: (0, 0)),
            tab_spec, tab_spec, tab_spec, tab_spec, tab_spec,
        ],
        out_specs=[
            pl.BlockSpec((tm, tn), lambda i, j: (i, j)),
            pl.BlockSpec((tm, 2 * LANES), lambda i, j: (i, 0)),
            pl.BlockSpec((None, tn, tm), lambda i, j: (i // nsb, _vt_row_block(j, tn), i % nsb)),
        ],
        out_shape=[
            jax.ShapeDtypeStruct((m, n), BF16),
            jax.ShapeDtypeStruct((m, 2 * LANES), F32),
            jax.ShapeDtypeStruct((m // s, 2 * BRANCH_WIDTH, s), BF16),
        ],
        scratch_shapes=[pltpu.VMEM((tm, d), BF16)],
        compiler_params=_params("parallel", "arbitrary"),
        name="in_proj",
    )(x2, g, w_big, w_small, c128, s128, c64, s64l, s64h)


def _bf16_floor(x):
    return pltpu.bitcast(pltpu.bitcast(x, jnp.int32) & jnp.int32(-65536), F32)


def _cum_kernel(zs_ref, fb_ref, kb_ref, carry_scr, *, t):
    @pl.when(pl.program_id(1) == 0)
    def _():
        carry_scr[...] = jnp.zeros_like(carry_scr)

    lane = lax.broadcasted_iota(jnp.int32, (1, LANES), 1)
    x = zs_ref[:, :LANES] + fb_ref[...]
    logf = -(jnp.maximum(-x, 0.0) + jnp.log1p(jnp.exp(-jnp.abs(x))))
    logf = jnp.where(lane < FOX_BIAS_COLS, logf, 0.0)
    r = lax.broadcasted_iota(jnp.int32, (t, t), 0)
    c = lax.broadcasted_iota(jnp.int32, (t, t), 1)
    tri = jnp.where(c <= r, 1.0, 0.0).astype(F32)
    cum = jnp.dot(tri, logf, preferred_element_type=F32, precision=lax.Precision.HIGHEST) + carry_scr[...]
    carry_scr[...] = cum[t - 1:t, :]
    c = cum * (-LOG2E)
    hi = _bf16_floor(c)
    mid = _bf16_floor(c - hi)
    lo = (c - hi) - mid
    kb_ref[...] = jnp.where(lane < N_HEADS, hi, jnp.where(lane < 2 * N_HEADS, mid, lo)).astype(BF16)


def _forget_cumsum(zs, fb, b, s, t):
    fb128 = jnp.pad(jnp.tile(fb, (1, 3)), ((0, 0), (0, LANES - FOX_BIAS_COLS)))
    return pl.pallas_call(
        functools.partial(_cum_kernel, t=t),
        grid=(b, s // t),
        in_specs=[
            pl.BlockSpec((t, 2 * LANES), lambda bi, i: (bi * (s // t) + i, 0)),
            pl.BlockSpec((1, LANES), lambda bi, i: (0, 0)),
        ],
        out_specs=pl.BlockSpec((t, LANES), lambda bi, i: (bi * (s // t) + i, 0)),
        out_shape=jax.ShapeDtypeStruct((b * s, LANES), BF16),
        scratch_shapes=[pltpu.VMEM((1, LANES), F32)],
        compiler_params=_params("parallel", "arbitrary"),
        name="forget_cumsum",
    )(zs, fb128)


def _flash_kernel(qt_ref, kt_ref, q_ref, k_ref, v_ref, *rest, extra, dk, t):
    qn_scr = None
    if extra is None:
        o_ref, m_scr, l_scr, acc_scr = rest
        extra_ref = None
    elif extra == "mask":
        extra_ref, o_ref, m_scr, l_scr, acc_scr = rest
    else:
        extra_ref, o_ref, m_scr, l_scr, acc_scr, qn_scr = rest
    masked = extra == "mask"
    p = pl.program_id(1)
    qi = qt_ref[p]
    ki = kt_ref[p]
    qs = min(t, Q_STRIP)

    def max_row_norm2(ref, h):
        xf = ref[:, h * dk:(h + 1) * dk].astype(F32)
        return jnp.max(jnp.sum(xf * xf, axis=1, keepdims=True), axis=0, keepdims=True)

    @pl.when(ki == qi)
    def _():
        m_scr[...] = jnp.full_like(m_scr, NEG_BIG)
        l_scr[...] = jnp.zeros_like(l_scr)
        acc_scr[...] = jnp.zeros_like(acc_scr)
        if qn_scr is not None:
            for h in range(N_HEADS):
                qn_scr[h:h + 1, :] = jnp.broadcast_to(max_row_norm2(q_ref, h), (1, LANES))

    def negligible():
        bias_max = jnp.max(extra_ref[...].astype(F32), axis=0, keepdims=True)
        worst = None
        for h in range(N_HEADS):
            dot_bound = jnp.sqrt(qn_scr[h:h + 1, 0:1] * max_row_norm2(k_ref, h)) * 1.01
            bias = (bias_max[:, h:h + 1] + bias_max[:, N_HEADS + h:N_HEADS + h + 1]
                    + bias_max[:, 2 * N_HEADS + h:2 * N_HEADS + h + 1])
            m_min = jnp.min(m_scr[h:h + 1, :], axis=1, keepdims=True)
            margin = m_min - (dot_bound + bias)
            worst = margin if worst is None else jnp.minimum(worst, margin)
        return jnp.min(worst) > UNDERFLOW_LOG2

    def step(diag):
        units = [(c, h) for c in range(t // qs) for h in range(N_HEADS)]

        def n_keys(c):
            return (c + 1) * qs if diag else t

        def logits(c, h):
            k_h = k_ref[:n_keys(c), h * dk:(h + 1) * dk]
            q_h = q_ref[c * qs:(c + 1) * qs, h * dk:(h + 1) * dk]
            if extra == "key_bias":
                lane = lax.broadcasted_iota(jnp.int32, (qs, LANES), 1)
                pick = jnp.where(lane < FOX_BIAS_COLS, jnp.where(lane % N_HEADS == h, 1.0, 0.0), 0.0)
                k_h = jnp.concatenate([k_h, extra_ref[:n_keys(c), :]], axis=1)
                q_h = jnp.concatenate([q_h, pick.astype(q_h.dtype)], axis=1)
            return lax.dot_general(k_h, q_h, (((1,), (1,)), ((), ())), preferred_element_type=F32)

        def keep_mask(c):
            nk = n_keys(c)
            if masked:
                return extra_ref[:, c * qs:(c + 1) * qs].astype(jnp.int32) != 0
            if diag:
                kr = lax.broadcasted_iota(jnp.int32, (nk, qs), 0)
                qc = lax.broadcasted_iota(jnp.int32, (nk, qs), 1) + c * qs
                return kr <= qc
            return None

        keeps = {}

        def softmax(c, h, st):
            cs = slice(c * qs, (c + 1) * qs)
            if c not in keeps:
                keeps[c] = keep_mask(c)
            if keeps[c] is not None:
                st = jnp.where(keeps[c], st, NEG_BIG)
            m_prev = m_scr[h:h + 1, cs]
            m_new = jnp.maximum(m_prev, jnp.max(st, axis=0, keepdims=True))
            m_scr[h:h + 1, cs] = m_new
            alpha = jnp.exp2(m_prev - m_new)
            return alpha, jnp.exp2(st - m_new).astype(BF16)

        def accumulate(c, h, alpha, pt):
            cs = slice(c * qs, (c + 1) * qs)
            nk = n_keys(c)
            ones = jnp.ones((ONES_ROWS, nk), v_ref.dtype)
            vt_h = jnp.concatenate([v_ref[h * HEAD_DIM:(h + 1) * HEAD_DIM, :nk], ones], axis=0)
            pv = jnp.dot(vt_h, pt, preferred_element_type=F32)
            sl = slice(h * HEAD_DIM, (h + 1) * HEAD_DIM)
            acc_scr[sl, cs] = alpha * acc_scr[sl, cs] + pv[:HEAD_DIM]
            l_scr[h:h + 1, cs] = alpha * l_scr[h:h + 1, cs] + pv[HEAD_DIM:HEAD_DIM + 1]

        n_u = len(units)
        st_q = [logits(*units[i]) for i in range(min(PIPE_LEAD, n_u))]
        p_q = []
        for u in range(n_u + PIPE_LAG):
            if u + PIPE_LEAD < n_u:
                st_q.append(logits(*units[u + PIPE_LEAD]))
            if u < n_u:
                p_q.append(softmax(*units[u], st_q.pop(0)))
            if u >= PIPE_LAG:
                accumulate(*units[u - PIPE_LAG], *p_q.pop(0))

    if masked:
        step(False)
    else:
        @pl.when(ki == qi)
        def _():
            step(True)

        if qn_scr is None:
            @pl.when(ki != qi)
            def _():
                step(False)
        else:
            @pl.when(jnp.logical_and(ki != qi, jnp.logical_not(negligible())))
            def _():
                step(False)

    @pl.when(ki == 0)
    def _():
        for h in range(N_HEADS):
            sl = slice(h * HEAD_DIM, (h + 1) * HEAD_DIM)
            o = acc_scr[sl, :] / l_scr[h:h + 1, :]
            o_ref[:, sl] = o.T.astype(BF16)


def _pair_tables(n):
    qt = np.concatenate([np.full(i + 1, i, np.int32) for i in range(n)])
    kt = np.concatenate([np.arange(i, -1, -1, dtype=np.int32) for i in range(n)])
    return jnp.asarray(qt), jnp.asarray(kt)


def _flash(q_arr, q_col, k_arr, k_col, vt_arr, vt_blk, b, s, t, dk, name, mask=None, key_bias=None):
    n = s // t
    qt, kt = _pair_tables(n)
    wq = N_HEADS * dk
    wv = N_HEADS * HEAD_DIM
    in_specs = [
        pl.BlockSpec((t, wq), lambda bi, p, qt, kt: (bi * n + qt[p], q_col)),
        pl.BlockSpec((t, wq), lambda bi, p, qt, kt: (bi * n + kt[p], k_col)),
        pl.BlockSpec((None, wv, t), lambda bi, p, qt, kt: (bi, vt_blk, kt[p])),
    ]
    args = [q_arr, k_arr, vt_arr]
    extra = None
    if mask is not None:
        extra = "mask"
        in_specs.append(pl.BlockSpec((None, t, t), lambda bi, p, qt, kt: (bi, kt[p], qt[p])))
        args.append(mask)
    elif key_bias is not None:
        extra = "key_bias"
        in_specs.append(pl.BlockSpec((t, LANES), lambda bi, p, qt, kt: (bi * n + kt[p], 0)))
        args.append(key_bias)
    scratch = [pltpu.VMEM((N_HEADS, t), F32), pltpu.VMEM((N_HEADS, t), F32), pltpu.VMEM((wv, t), F32)]
    if extra == "key_bias":
        scratch.append(pltpu.VMEM((N_HEADS, LANES), F32))
    grid_spec = pltpu.PrefetchScalarGridSpec(
        num_scalar_prefetch=2,
        grid=(b, n * (n + 1) // 2),
        in_specs=in_specs,
        out_specs=pl.BlockSpec((t, wv), lambda bi, p, qt, kt: (bi * n + qt[p], 0)),
        scratch_shapes=scratch,
    )
    return pl.pallas_call(
        functools.partial(_flash_kernel, extra=extra, dk=dk, t=t),
        grid_spec=grid_spec,
        out_shape=jax.ShapeDtypeStruct((b * s, wv), BF16),
        compiler_params=_params("parallel", "arbitrary"),
        name=name,
    )(qt, kt, *args)


def _count(sc_ref, nvalid, cand, tk, tq, ind_fn):
    group = _count_group(sc_ref.shape[0])

    lanes = COUNT_ACCS * 8

    def body(g, a):
        for e in range(group):
            c = g * group + e
            ind = ind_fn(sc_ref[c], cand, c)
            a = a + jnp.sum(ind.reshape(tk // lanes, lanes, tq), axis=0)
        return a

    part = lax.fori_loop(0, (nvalid + group - 1) // group, body, jnp.zeros((lanes, tq), jnp.int32))
    return jnp.sum(part, axis=0, keepdims=True)


def _count_group(nchunks):
    return 2 if nchunks % 2 == 0 else 1


def _indexer_kernel(q_ref, wt_ref, k2_ref, mask_ref, sc_ref, qm_scr, j_scr, *, tq, tk, n_sel, idx_bits):
    qi = pl.program_id(1)
    one, zero = jnp.int32(1), jnp.int32(0)
    nchunks = sc_ref.shape[0]
    nvalid = ((qi + 1) * tq + tk - 1) // tk
    qcol = qi * tq + lax.broadcasted_iota(jnp.int32, (1, tq), 1)
    krow0 = lax.broadcasted_iota(jnp.int32, (tk, 1), 0)

    half = lax.broadcasted_iota(jnp.int32, (tq, LANES), 1) // IDX_DIM
    for g in range(IDX_HEADS // 2):
        qp = q_ref[:, g * LANES:(g + 1) * LANES]
        for e in range(2):
            qm_scr[2 * g + e] = jnp.where(half == e, qp, jnp.zeros_like(qp))

    def score_chunk(c, kmax):
        kc = k2_ref[pl.ds(pl.multiple_of(c * tk, tk), tk), :]
        acc = jnp.zeros((tk, tq), F32)
        for h in range(IDX_HEADS):
            x = lax.dot_general(kc, qm_scr[h], (((1,), (1,)), ((), ())), preferred_element_type=F32)
            acc = acc + wt_ref[h:h + 1, :] * jnp.maximum(x, 0.0)
        score = acc * (IDX_HEADS * IDX_DIM) ** -0.5
        bits = pltpu.bitcast(score, jnp.int32)
        key = bits ^ ((bits >> 31) & 0x7FFFFFFF)
        causal = (c * tk + krow0) <= qcol
        key = jnp.where(causal, key, INT_MIN)
        sc_ref[c] = key
        return jnp.maximum(kmax, jnp.max(key.reshape(tk // 8, 8, tq), axis=0))

    kmax = lax.fori_loop(0, nvalid, score_chunk, jnp.full((8, tq), INT_MIN, jnp.int32))
    kmax = jnp.max(kmax, axis=0, keepdims=True)

    group = _count_group(nchunks)
    if group > 1:
        def pad_chunk(c, carry):
            sc_ref[c] = jnp.full((tk, tq), INT_MIN, jnp.int32)
            return carry

        lax.fori_loop(nvalid, (nvalid + group - 1) // group * group, pad_chunk, 0)

    def count_ge(cand):
        return _count(sc_ref, nvalid, cand, tk, tq, lambda k, cd, c: jnp.where(k >= cd, one, zero))

    n_keys = qcol + 1
    f0 = count_ge(jnp.zeros((1, tq), jnp.int32))
    pos = f0 >= n_sel
    lo0 = jnp.where(pos, 0, INT_MIN + 1)
    hi0 = jnp.where(pos, kmax, jnp.minimum(kmax, -1)) + 1
    flo0 = jnp.where(pos, f0, n_keys)

    def searching(flo, w):
        return jnp.where(flo > n_sel, jnp.where(w > 1, 1.0, 0.0), 0.0)

    def bisect_cond(carry):
        it, lo, w, flo = carry
        return (jnp.max(searching(flo, w)) > 0.0) & (it < 32)

    def bisect_body(carry):
        it, lo, w, flo = carry
        active = searching(flo, w) > 0
        off = w >> 1
        cand = lo + off
        cnt = count_ge(cand)
        ok = cnt >= n_sel
        lo = jnp.where(active, jnp.where(ok, cand, lo), lo)
        flo = jnp.where(active, jnp.where(ok, cnt, flo), flo)
        w = jnp.where(active, jnp.where(ok, w - off, off), w)
        return it + 1, lo, w, flo

    _, lo, _, flo = lax.while_loop(bisect_cond, bisect_body, (jnp.int32(0), lo0, hi0 - lo0, flo0))
    found = n_keys >= n_sel
    tie = jnp.where(found, jnp.where(flo > n_sel, one, zero), zero) > 0
    gt_thr = jnp.where(found, jnp.where(tie, lo, lo - 1), INT_MIN)
    thr = jnp.where(tie, lo, INT_MIN + 1)

    j_scr[...] = jnp.full_like(j_scr, 2 ** idx_bits - 1)
    has_tie = jnp.max(jnp.where(tie, 1.0, 0.0)) > 0.0

    @pl.when(has_tie)
    def _():
        cgt = _count(sc_ref, nvalid, thr, tk, tq, lambda k, cd, c: jnp.where(k > cd, one, zero))
        need = n_sel - cgt

        def jbit(i, jv):
            cand = jv | lax.shift_left(jnp.int32(1), idx_bits - 1 - i)
            cnt = _count(sc_ref, nvalid, cand, tk, tq,
                         lambda k, cd, c: jnp.where(k == thr, jnp.where((c * tk + krow0) < cd, one, zero), zero))
            return jnp.where(cnt < need, cand, jv)

        j_scr[...] = lax.fori_loop(0, idx_bits, jbit, jnp.zeros((1, tq), jnp.int32))

    jlast = j_scr[...]

    def write_chunk(c, carry):
        key = sc_ref[c]
        sel = jnp.where(key > gt_thr, one,
                        jnp.where(key == thr, jnp.where((c * tk + krow0) <= jlast, one, zero), zero))
        mask_ref[pl.ds(pl.multiple_of(c * tk, tk), tk), :] = sel.astype(jnp.int8)
        return carry

    lax.fori_loop(0, nvalid, write_chunk, 0)

    def zero_chunk(c, carry):
        mask_ref[pl.ds(pl.multiple_of(c * tk, tk), tk), :] = jnp.zeros((tk, tq), jnp.int8)
        return carry

    lax.fori_loop(nvalid, nchunks, zero_chunk, 0)


def _indexer_mask(zb, wt, k2, b, s, tq, tk, n_sel):
    nq = s // tq
    idx_bits = max(1, int(np.ceil(np.log2(s))))
    return pl.pallas_call(
        functools.partial(_indexer_kernel, tq=tq, tk=tk, n_sel=n_sel, idx_bits=idx_bits),
        grid=(b, nq),
        in_specs=[
            pl.BlockSpec((tq, IDX_HEADS * IDX_DIM), lambda bi, i: (bi * nq + i, COL_IQ // (IDX_HEADS * IDX_DIM))),
            pl.BlockSpec((None, IDX_HEADS, tq), lambda bi, i: (bi, 0, i)),
            pl.BlockSpec((None, s, LANES), lambda bi, i: (bi, 0, 0)),
        ],
        out_specs=pl.BlockSpec((None, s, tq), lambda bi, i: (bi, 0, i)),
        out_shape=jax.ShapeDtypeStruct((b, s, s), jnp.int8),
        scratch_shapes=[
            pltpu.VMEM((s // tk, tk, tq), jnp.int32),
            pltpu.VMEM((IDX_HEADS, tq, LANES), BF16),
            pltpu.VMEM((1, tq), jnp.int32),
        ],
        compiler_params=_params("parallel", "arbitrary"),
        name="indexer_mask",
    )(zb, wt, k2)


def _mla_up_kernel(c_ref, zs_ref, qg_ref, kvg_ref, wq_ref, wk_ref, wv_ref, c64_ref, s64l_ref, s64h_ref,
                   q_ref, k_ref, v_ref, *, q_scale):
    def norm(xf, g):
        y = xf * lax.rsqrt(jnp.mean(xf * xf, axis=-1, keepdims=True) + EPS)
        return (y * g).astype(BF16)

    cq = norm(c_ref[:, :Q_LORA].astype(F32), qg_ref[...])
    ckv = norm(c_ref[:, Q_LORA:].astype(F32), kvg_ref[...])
    q = jnp.dot(cq, wq_ref[...], preferred_element_type=F32) * q_scale
    k = jnp.dot(ckv, wk_ref[...], preferred_element_type=F32)
    v_ref[...] = jnp.dot(ckv, wv_ref[...], preferred_element_type=F32).T.astype(BF16)
    lane = lax.broadcasted_iota(jnp.int32, (1, LANES), 1)
    k_rot = jnp.where(lane < QK_ROPE, zs_ref[:, LANES:], 0.0)
    cos, sin_lo, sin_hi = c64_ref[...], s64l_ref[...], s64h_ref[...]
    for h in range(N_HEADS):
        nope = slice(h * MLA_QK_PAD, h * MLA_QK_PAD + QK_NOPE)
        rope = slice(h * MLA_QK_PAD + QK_NOPE, (h + 1) * MLA_QK_PAD)
        q_ref[:, nope] = q[:, nope].astype(BF16)
        q_ref[:, rope] = _rope64(q[:, rope], cos, sin_lo, sin_hi).astype(BF16)
        k_ref[:, nope] = k[:, nope].astype(BF16)
        k_ref[:, rope] = k_rot.astype(BF16)


def _mla_up(zb, zs, qg, kvg, wq, wk, wv, tabs, tm):
    m = zb.shape[0]
    _, _, c64, s64l, s64h = tabs
    nsb = c64.shape[0] // tm
    tab_spec = pl.BlockSpec((tm, LANES), lambda i: (i % nsb, 0))
    wqk = N_HEADS * MLA_QK_PAD
    full = lambda a: pl.BlockSpec(a.shape, lambda i: (0,) * a.ndim)
    return pl.pallas_call(
        functools.partial(_mla_up_kernel, q_scale=(QK_NOPE + QK_ROPE) ** -0.5 * LOG2E),
        grid=(m // tm,),
        in_specs=[
            pl.BlockSpec((tm, Q_LORA + KV_LORA), lambda i: (i, COL_CQ // (Q_LORA + KV_LORA))),
            pl.BlockSpec((tm, 2 * LANES), lambda i: (i, 0)),
            full(qg), full(kvg), full(wq), full(wk), full(wv),
            tab_spec, tab_spec, tab_spec,
        ],
        out_specs=[
            pl.BlockSpec((tm, wqk), lambda i: (i, 0)),
            pl.BlockSpec((tm, wqk), lambda i: (i, 0)),
            pl.BlockSpec((None, BRANCH_WIDTH, tm), lambda i: (i // nsb, 0, i % nsb)),
        ],
        out_shape=[
            jax.ShapeDtypeStruct((m, wqk), BF16),
            jax.ShapeDtypeStruct((m, wqk), BF16),
            jax.ShapeDtypeStruct((m // (nsb * tm), BRANCH_WIDTH, nsb * tm), BF16),
        ],
        compiler_params=_params("parallel"),
        name="mla_up",
    )(zb, zs, qg, kvg, wq, wk, wv, c64, s64l, s64h)


def _branch_kernel(o0_ref, o1_ref, o2_ref, g0_ref, g1_ref, g2_ref, m0_ref, m1_ref, m2_ref, w_ref, y_ref):
    y = None
    for i, (o_ref, g_ref, m_ref) in enumerate(((o0_ref, g0_ref, m0_ref), (o1_ref, g1_ref, m1_ref),
                                               (o2_ref, g2_ref, m2_ref))):
        g = g_ref[...].astype(F32)
        a = (o_ref[...].astype(F32) * (g * jax.nn.sigmoid(g))).astype(BF16)
        term = jax.nn.sigmoid(m_ref[...].astype(F32)) * jnp.dot(a, w_ref[i], preferred_element_type=F32)
        y = term if y is None else y + term
    y_ref[...] = y.astype(BF16)


def _branch_merge(o_fox, o_dsa, o_mla, zb, wb, d, tm, tn):
    m = zb.shape[0]
    o_spec = pl.BlockSpec((tm, BRANCH_WIDTH), lambda i, j: (i, 0))
    silu0 = (COL_MERGE + N_BRANCH * d) // BRANCH_WIDTH

    def g_spec(i_br):
        return pl.BlockSpec((tm, BRANCH_WIDTH), lambda i, j: (i, silu0 + i_br))

    def m_spec(i_br):
        base = (COL_MERGE + i_br * d) // tn
        return pl.BlockSpec((tm, tn), lambda i, j: (i, base + j))

    return pl.pallas_call(
        _branch_kernel,
        grid=(m // tm, d // tn),
        in_specs=[o_spec, o_spec, o_spec, g_spec(0), g_spec(1), g_spec(2), m_spec(0), m_spec(1), m_spec(2),
                  pl.BlockSpec((N_BRANCH, BRANCH_WIDTH, tn), lambda i, j: (0, 0, j))],
        out_specs=pl.BlockSpec((tm, tn), lambda i, j: (i, j)),
        out_shape=jax.ShapeDtypeStruct((m, d), BF16),
        compiler_params=_params("parallel", "arbitrary"),
        name="branch_merge",
    )(o_fox, o_dsa, o_mla, zb, zb, zb, zb, zb, zb, wb)


def _out_kernel(x_ref, y_ref, w_ref, g_ref, o_ref, *, final_norm):
    r = x_ref[...] + jnp.dot(y_ref[...], w_ref[...], preferred_element_type=F32)
    if final_norm:
        r = r * lax.rsqrt(jnp.mean(r * r, axis=-1, keepdims=True) + EPS) * g_ref[...]
    o_ref[...] = r


def _out_proj(x2, y, w, g, tm, final_norm):
    m, d = x2.shape
    return pl.pallas_call(
        functools.partial(_out_kernel, final_norm=final_norm),
        grid=(m // tm,),
        in_specs=[
            pl.BlockSpec((tm, d), lambda i: (i, 0)),
            pl.BlockSpec((tm, d), lambda i: (i, 0)),
            pl.BlockSpec((d, d), lambda i: (0, 0)),
            pl.BlockSpec((1, d), lambda i: (0, 0)),
        ],
        out_specs=pl.BlockSpec((tm, d), lambda i: (i, 0)),
        out_shape=jax.ShapeDtypeStruct((m, d), F32),
        compiler_params=_params("parallel"),
        name="out_proj",
    )(x2, y, w, g)


def _split_w_in(w_in, d):
    sizes = (BRANCH_WIDTH, BRANCH_WIDTH, BRANCH_WIDTH, N_HEADS, BRANCH_WIDTH, BRANCH_WIDTH, BRANCH_WIDTH,
             IDX_HEADS * IDX_DIM, IDX_DIM, IDX_HEADS, Q_LORA, KV_LORA, QK_ROPE,
             N_BRANCH * BRANCH_WIDTH, N_BRANCH * d)
    pts = np.cumsum(sizes)[:-1].tolist()
    (f_q, f_k, f_v, f_gate, d_q, d_k, d_v, i_q, i_k, i_w, c_q, c_kv, k_pe, silu_g, merge_g) = jnp.split(
        w_in, pts, axis=-1)
    big = jnp.concatenate([f_q, f_k, f_v, d_q, d_k, d_v, i_q, c_q, c_kv, merge_g, silu_g], axis=-1).astype(BF16)
    pad = jnp.zeros((w_in.shape[0], LANES - FOX_BIAS_COLS - IDX_HEADS), w_in.dtype)
    small = jnp.concatenate([f_gate, f_gate, f_gate, i_w, pad, k_pe, i_k], axis=-1).astype(BF16)
    return big, small


def _pad_heads(w, take, width_in):
    r = w.shape[0]
    w3 = w.reshape(r, N_HEADS, width_in)[:, :, take]
    w3 = jnp.pad(w3, ((0, 0), (0, 0), (0, MLA_QK_PAD - w3.shape[-1])))
    return w3.reshape(r, N_HEADS * MLA_QK_PAD)


def _tile(s, pref):
    return pref if s % pref == 0 else s


def kernel(x, norm_g, w_in, forget_b, q_norm_g, w_q_up, kv_norm_g, w_kv_up, w_branch, w_out, final_norm_g):
    b, s, d = x.shape
    depth = w_in.shape[0]
    n_sel = min(TOPK_MAX, s // 4)
    m = b * s
    tabs = _rope_tables(s)
    t_att = _tile(s, 1024)
    tm_in = _tile(s, 1024)
    x2 = x.reshape(m, d)
    for l in range(depth):
        w_big, w_small = _split_w_in(w_in[l], d)
        zb, zs, vt = _in_proj(x2, norm_g[l].reshape(1, d), w_big, w_small, tabs, tm_in, 512)

        key_bias = _forget_cumsum(zs, forget_b[l].reshape(1, N_HEADS), b, s, _tile(s, 1024))
        o_fox = _flash(zb, COL_FQ // BRANCH_WIDTH, zb, COL_FK // BRANCH_WIDTH, vt, 0,
                       b, s, t_att, HEAD_DIM, "flash_fox", key_bias=key_bias)

        zs3 = zs.reshape(b, s, 2 * LANES)
        wt = jnp.swapaxes(zs3[:, :, FOX_BIAS_COLS:FOX_BIAS_COLS + IDX_HEADS], 1, 2)
        ik = zs3[:, :, LANES + QK_ROPE:].astype(BF16)
        k2 = jnp.concatenate([ik, ik], axis=-1)
        mask = _indexer_mask(zb, wt, k2, b, s, _tile(s, 256), _tile(s, 512), n_sel)
        o_dsa = _flash(zb, COL_DQ // BRANCH_WIDTH, zb, COL_DK // BRANCH_WIDTH, vt, 1,
                       b, s, t_att, HEAD_DIM, "flash_mask", mask=mask)

        wq = _pad_heads(w_q_up[l], slice(0, QK_NOPE + QK_ROPE), QK_NOPE + QK_ROPE).astype(BF16)
        wk = _pad_heads(w_kv_up[l], slice(0, QK_NOPE), QK_NOPE + V_DIM).astype(BF16)
        wv = w_kv_up[l].reshape(KV_LORA, N_HEADS, QK_NOPE + V_DIM)[:, :, QK_NOPE:].reshape(
            KV_LORA, BRANCH_WIDTH).astype(BF16)
        q_mla, k_mla, vt_mla = _mla_up(zb, zs, q_norm_g[l].reshape(1, Q_LORA), kv_norm_g[l].reshape(1, KV_LORA),
                                       wq, wk, wv, tabs, _tile(s, 512))
        o_mla = _flash(q_mla, 0, k_mla, 0, vt_mla, 0, b, s, t_att, MLA_QK_PAD, "flash_plain")

        y = _branch_merge(o_fox, o_dsa, o_mla, zb, w_branch[l].astype(BF16), d, _tile(s, 512), 1024)
        last = l == depth - 1
        x2 = _out_proj(x2, y, w_out[l].astype(BF16), final_norm_g.reshape(1, d), _tile(s, 512), last)
    return x2.reshape(b, s, d)
```

```python
import functools

import jax
import jax.numpy as jnp
import numpy as np
from jax import lax
from jax.experimental import pallas as pl
from jax.experimental.pallas import tpu as pltpu

F32 = jnp.float32
BF16 = jnp.bfloat16

HEAD_DIM = 128
N_HEADS = 8
BRANCH_WIDTH = N_HEADS * HEAD_DIM
N_BRANCH = 3
IDX_HEADS = 16
IDX_DIM = 64
TOPK_MAX = 256
Q_LORA = 512
KV_LORA = 512
QK_NOPE = 128
QK_ROPE = 64
V_DIM = 128
MLA_QK_PAD = 256
FOX_BIAS_COLS = 3 * N_HEADS
NORM_SLACK = 1.02
UNDERFLOW_LOG2 = 160.0
ONES_ROWS = 16
ROPE_THETA = 10000.0
EPS = 1e-6

LANES = 128
VMEM_LIMIT = 56 * 1024 * 1024
NEG_BIG = -1e30
INT_MIN = -(2 ** 31)
LOG2E = 1.4426950408889634
Q_STRIP = 256
COUNT_ACCS = 4
PIPE_LEAD = 2
PIPE_LAG = 2

COL_FQ, COL_FK, COL_FV = 0, 1024, 2048
COL_DQ, COL_DK, COL_DV = 3072, 4096, 5120
COL_IQ = 6144
COL_CQ = 7168
COL_MERGE = 8192


def _params(*sem):
    return pltpu.CompilerParams(dimension_semantics=sem, vmem_limit_bytes=VMEM_LIMIT)


def _rope_tables(s):
    pos = jnp.arange(s, dtype=F32)

    def cs(half):
        inv = ROPE_THETA ** (-jnp.arange(half, dtype=F32) / half)
        ang = pos[:, None] * inv[None, :]
        return jnp.cos(ang), jnp.sin(ang)

    c64, s64 = cs(64)
    c32, s32 = cs(32)
    z32 = jnp.zeros_like(s32)
    cos128 = jnp.concatenate([c64, c64], axis=-1)
    sin128 = jnp.concatenate([-s64, s64], axis=-1)
    cos64 = jnp.concatenate([c32, c32, c32, c32], axis=-1)
    sin64_lo = jnp.concatenate([-s32, z32, -s32, z32], axis=-1)
    sin64_hi = jnp.concatenate([z32, s32, z32, s32], axis=-1)
    return cos128, sin128, cos64, sin64_lo, sin64_hi


def _rope128(x, cos, sin):
    return x * cos + pltpu.roll(x, 64, 1) * sin


def _rope64(x, cos, sin_lo, sin_hi):
    return x * cos + pltpu.roll(x, 96, 1) * sin_lo + pltpu.roll(x, 32, 1) * sin_hi


def _in_proj_kernel(x_ref, g_ref, w_ref, ws_ref, c128_ref, s128_ref, c64_ref, s64l_ref, s64h_ref,
                    zb_ref, zs_ref, vt_ref, nrm_ref, h_scr, *, tn, q_scale):
    j = pl.program_id(1)

    @pl.when(j == 0)
    def _():
        nrm_ref[...] = jnp.zeros_like(nrm_ref)
        xf = x_ref[...]
        y = xf * lax.rsqrt(jnp.mean(xf * xf, axis=-1, keepdims=True) + EPS)
        h = (y * g_ref[...]).astype(BF16)
        h_scr[...] = h
        zs = jnp.dot(h, ws_ref[...], preferred_element_type=F32)
        zs_ref[:, :LANES] = zs[:, :LANES]
        zs_ref[:, LANES:] = _rope64(zs[:, LANES:], c64_ref[...], s64l_ref[...], s64h_ref[...])

    acc = jnp.dot(h_scr[...], w_ref[...], preferred_element_type=F32)
    col = j * tn
    is_q = ((col >= COL_FQ) & (col < COL_FK)) | ((col >= COL_DQ) & (col < COL_DK))
    acc = acc * jnp.where(is_q, q_scale, 1.0).astype(F32)
    is_r128 = (col >= COL_DQ) & (col < COL_DV)
    is_r64 = (col >= COL_IQ) & (col < COL_CQ)

    @pl.when(is_r128)
    def _():
        for g in range(tn // LANES):
            sl = slice(g * LANES, (g + 1) * LANES)
            zb_ref[:, sl] = _rope128(acc[:, sl], c128_ref[...], s128_ref[...]).astype(BF16)

    @pl.when(is_r64)
    def _():
        for g in range(tn // LANES):
            sl = slice(g * LANES, (g + 1) * LANES)
            zb_ref[:, sl] = _rope64(acc[:, sl], c64_ref[...], s64l_ref[...], s64h_ref[...]).astype(BF16)

    @pl.when(jnp.logical_not(is_r128 | is_r64))
    def _():
        zb_ref[...] = acc.astype(BF16)

    is_v = ((col >= COL_FV) & (col < COL_DQ)) | ((col >= COL_DV) & (col < COL_IQ))

    @pl.when(is_v)
    def _():
        vt_ref[...] = acc.T.astype(BF16)

    is_fq = (col >= COL_FQ) & (col < COL_FK)
    is_fk = (col >= COL_FK) & (col < COL_FV)

    @pl.when(is_fq | is_fk)
    def _():
        lane = lax.broadcasted_iota(jnp.int32, (1, LANES), 1)
        first = (col % BRANCH_WIDTH) // HEAD_DIM + jnp.where(is_fq, N_HEADS, 0)
        nrm = nrm_ref[...]
        for g in range(tn // HEAD_DIM):
            a = acc[:, g * HEAD_DIM:(g + 1) * HEAD_DIM]
            nrm = jnp.where(lane == first + g, jnp.sum(a * a, axis=1, keepdims=True), nrm)
        nrm_ref[...] = nrm


def _vt_row_block(j, tn):
    per = BRANCH_WIDTH // tn
    fv0, dv0 = COL_FV // tn, COL_DV // tn
    in_fox = jnp.clip(j - fv0, 0, per - 1)
    in_dsa = per + jnp.clip(j - dv0, 0, per - 1)
    return jnp.where(j < dv0, in_fox, in_dsa)


def _in_proj(x2, g, w_big, w_small, tabs, tm, tn):
    m, d = x2.shape
    n = w_big.shape[1]
    c128, s128, c64, s64l, s64h = tabs
    s = c128.shape[0]
    nsb = s // tm
    tab_spec = pl.BlockSpec((tm, LANES), lambda i, j: (i % nsb, 0))
    return pl.pallas_call(
        functools.partial(_in_proj_kernel, tn=tn, q_scale=HEAD_DIM ** -0.5 * LOG2E),
        grid=(m // tm, n // tn),
        in_specs=[
            pl.BlockSpec((tm, d), lambda i, j: (i, 0)),
            pl.BlockSpec((1, d), lambda i, j: (0, 0)),
            pl.BlockSpec((d, tn), lambda i, j: (0, j)),
            pl.BlockSpec((d, 2 * LANES), lambda i, j: (0, 0)),
            tab_spec, tab_spec, tab_spec, tab_spec, tab_spec,
        ],
        out_specs=[
            pl.BlockSpec((tm, tn), lambda i, j: (i, j)),
            pl.BlockSpec((tm, 2 * LANES), lambda i, j: (i, 0)),
            pl.BlockSpec((None, tn, tm), lambda i, j: (i // nsb, _vt_row_block(j, tn), i % nsb)),
            pl.BlockSpec((tm, LANES), lambda i, j: (i, 0)),
        ],
        out_shape=[
            jax.ShapeDtypeStruct((m, n), BF16),
            jax.ShapeDtypeStruct((m, 2 * LANES), F32),
            jax.ShapeDtypeStruct((m // s, 2 * BRANCH_WIDTH, s), BF16),
            jax.ShapeDtypeStruct((m, LANES), F32),
        ],
        scratch_shapes=[pltpu.VMEM((tm, d), BF16)],
        compiler_params=_params("parallel", "arbitrary"),
        name="in_proj",
    )(x2, g, w_big, w_small, c128, s128, c64, s64l, s64h)


def _bf16_floor(x):
    return pltpu.bitcast(pltpu.bitcast(x, jnp.int32) & jnp.int32(-65536), F32)


def _cum_kernel(zs_ref, fb_ref, kb_ref, carry_scr, *, t):
    @pl.when(pl.program_id(1) == 0)
    def _():
        carry_scr[...] = jnp.zeros_like(carry_scr)

    lane = lax.broadcasted_iota(jnp.int32, (1, LANES), 1)
    x = zs_ref[:, :LANES] + fb_ref[...]
    logf = -(jnp.maximum(-x, 0.0) + jnp.log1p(jnp.exp(-jnp.abs(x))))
    logf = jnp.where(lane < FOX_BIAS_COLS, logf, 0.0)
    r = lax.broadcasted_iota(jnp.int32, (t, t), 0)
    c = lax.broadcasted_iota(jnp.int32, (t, t), 1)
    tri = jnp.where(c <= r, 1.0, 0.0).astype(F32)
    cum = jnp.dot(tri, logf, preferred_element_type=F32, precision=lax.Precision.HIGHEST) + carry_scr[...]
    carry_scr[...] = cum[t - 1:t, :]
    c = cum * (-LOG2E)
    hi = _bf16_floor(c)
    mid = _bf16_floor(c - hi)
    lo = (c - hi) - mid
    kb_ref[...] = jnp.where(lane < N_HEADS, hi, jnp.where(lane < 2 * N_HEADS, mid, lo)).astype(BF16)


def _forget_cumsum(zs, fb, b, s, t):
    fb128 = jnp.pad(jnp.tile(fb, (1, 3)), ((0, 0), (0, LANES - FOX_BIAS_COLS)))
    return pl.pallas_call(
        functools.partial(_cum_kernel, t=t),
        grid=(b, s // t),
        in_specs=[
            pl.BlockSpec((t, 2 * LANES), lambda bi, i: (bi * (s // t) + i, 0)),
            pl.BlockSpec((1, LANES), lambda bi, i: (0, 0)),
        ],
        out_specs=pl.BlockSpec((t, LANES), lambda bi, i: (bi * (s // t) + i, 0)),
        out_shape=jax.ShapeDtypeStruct((b * s, LANES), BF16),
        scratch_shapes=[pltpu.VMEM((1, LANES), F32)],
        compiler_params=_params("parallel", "arbitrary"),
        name="forget_cumsum",
    )(zs, fb128)


def _flash_kernel(qt_ref, kt_ref, q_ref, k_ref, v_ref, *rest, extra, dk, t):
    qn_scr = None
    if extra is None:
        o_ref, m_scr, l_scr, acc_scr = rest
        extra_ref = None
    elif extra == "mask":
        extra_ref, o_ref, m_scr, l_scr, acc_scr = rest
    else:
        extra_ref, qnorm_ref, knorm_ref, o_ref, m_scr, l_scr, acc_scr, qn_scr = rest
    masked = extra == "mask"
    p = pl.program_id(1)
    qi = qt_ref[p]
    ki = kt_ref[p]
    qs = min(t, Q_STRIP)

    @pl.when(ki == qi)
    def _():
        m_scr[...] = jnp.full_like(m_scr, NEG_BIG)
        l_scr[...] = jnp.zeros_like(l_scr)
        acc_scr[...] = jnp.zeros_like(acc_scr)
        if qn_scr is not None:
            qn_scr[...] = pltpu.roll(jnp.max(qnorm_ref[...], axis=0, keepdims=True), LANES - N_HEADS, 1)

    def negligible():
        kn = jnp.max(knorm_ref[...], axis=0, keepdims=True)
        dot_bound = jnp.sqrt(qn_scr[...] * kn) * NORM_SLACK
        bm = jnp.max(extra_ref[...].astype(F32), axis=0, keepdims=True)
        bias = bm + pltpu.roll(bm, LANES - N_HEADS, 1) + pltpu.roll(bm, LANES - 2 * N_HEADS, 1)
        bound = dot_bound + bias
        m_min = jnp.min(m_scr[...], axis=1, keepdims=True)
        head = lax.broadcasted_iota(jnp.int32, (N_HEADS, LANES), 0)
        lane = lax.broadcasted_iota(jnp.int32, (N_HEADS, LANES), 1)
        margin = jnp.where(head == lane, m_min - bound, jnp.inf)
        return jnp.min(margin) > UNDERFLOW_LOG2

    def step(diag):
        units = [(c, h) for c in range(t // qs) for h in range(N_HEADS)]

        def n_keys(c):
            return (c + 1) * qs if diag else t

        def logits(c, h):
            k_h = k_ref[:n_keys(c), h * dk:(h + 1) * dk]
            q_h = q_ref[c * qs:(c + 1) * qs, h * dk:(h + 1) * dk]
            if extra == "key_bias":
                lane = lax.broadcasted_iota(jnp.int32, (qs, LANES), 1)
                pick = jnp.where(lane < FOX_BIAS_COLS, jnp.where(lane % N_HEADS == h, 1.0, 0.0), 0.0)
                k_h = jnp.concatenate([k_h, extra_ref[:n_keys(c), :]], axis=1)
                q_h = jnp.concatenate([q_h, pick.astype(q_h.dtype)], axis=1)
            return lax.dot_general(k_h, q_h, (((1,), (1,)), ((), ())), preferred_element_type=F32)

        def keep_mask(c):
            nk = n_keys(c)
            if masked:
                return extra_ref[:, c * qs:(c + 1) * qs].astype(jnp.int32) != 0
            if diag:
                kr = lax.broadcasted_iota(jnp.int32, (nk, qs), 0)
                qc = lax.broadcasted_iota(jnp.int32, (nk, qs), 1) + c * qs
                return kr <= qc
            return None

        keeps = {}

        def softmax(c, h, st):
            cs = slice(c * qs, (c + 1) * qs)
            if c not in keeps:
                keeps[c] = keep_mask(c)
            if keeps[c] is not None:
                st = jnp.where(keeps[c], st, NEG_BIG)
            m_prev = m_scr[h:h + 1, cs]
            m_new = jnp.maximum(m_prev, jnp.max(st, axis=0, keepdims=True))
            m_scr[h:h + 1, cs] = m_new
            alpha = jnp.exp2(m_prev - m_new)
            return alpha, jnp.exp2(st - m_new).astype(BF16)

        def accumulate(c, h, alpha, pt):
            cs = slice(c * qs, (c + 1) * qs)
            nk = n_keys(c)
            ones = jnp.ones((ONES_ROWS, nk), v_ref.dtype)
            vt_h = jnp.concatenate([v_ref[h * HEAD_DIM:(h + 1) * HEAD_DIM, :nk], ones], axis=0)
            pv = jnp.dot(vt_h, pt, preferred_element_type=F32)
            sl = slice(h * HEAD_DIM, (h + 1) * HEAD_DIM)
            acc_scr[sl, cs] = alpha * acc_scr[sl, cs] + pv[:HEAD_DIM]
            l_scr[h:h + 1, cs] = alpha * l_scr[h:h + 1, cs] + pv[HEAD_DIM:HEAD_DIM + 1]

        n_u = len(units)
        st_q = [logits(*units[i]) for i in range(min(PIPE_LEAD, n_u))]
        p_q = []
        for u in range(n_u + PIPE_LAG):
            if u + PIPE_LEAD < n_u:
                st_q.append(logits(*units[u + PIPE_LEAD]))
            if u < n_u:
                p_q.append(softmax(*units[u], st_q.pop(0)))
            if u >= PIPE_LAG:
                accumulate(*units[u - PIPE_LAG], *p_q.pop(0))

    if masked:
        step(False)
    else:
        @pl.when(ki == qi)
        def _():
            step(True)

        if qn_scr is None:
            @pl.when(ki != qi)
            def _():
                step(False)
        else:
            @pl.when(jnp.logical_and(ki != qi, jnp.logical_not(negligible())))
            def _():
                step(False)

    @pl.when(ki == 0)
    def _():
        for h in range(N_HEADS):
            sl = slice(h * HEAD_DIM, (h + 1) * HEAD_DIM)
            o = acc_scr[sl, :] / l_scr[h:h + 1, :]
            o_ref[:, sl] = o.T.astype(BF16)


def _pair_tables(n):
    qt = np.concatenate([np.full(i + 1, i, np.int32) for i in range(n)])
    kt = np.concatenate([np.arange(i, -1, -1, dtype=np.int32) for i in range(n)])
    return jnp.asarray(qt), jnp.asarray(kt)


def _flash(q_arr, q_col, k_arr, k_col, vt_arr, vt_blk, b, s, t, dk, name, mask=None, key_bias=None, norms=None):
    n = s // t
    qt, kt = _pair_tables(n)
    wq = N_HEADS * dk
    wv = N_HEADS * HEAD_DIM
    in_specs = [
        pl.BlockSpec((t, wq), lambda bi, p, qt, kt: (bi * n + qt[p], q_col)),
        pl.BlockSpec((t, wq), lambda bi, p, qt, kt: (bi * n + kt[p], k_col)),
        pl.BlockSpec((None, wv, t), lambda bi, p, qt, kt: (bi, vt_blk, kt[p])),
    ]
    args = [q_arr, k_arr, vt_arr]
    extra = None
    if mask is not None:
        extra = "mask"
        in_specs.append(pl.BlockSpec((None, t, t), lambda bi, p, qt, kt: (bi, kt[p], qt[p])))
        args.append(mask)
    elif key_bias is not None:
        extra = "key_bias"
        in_specs.append(pl.BlockSpec((t, LANES), lambda bi, p, qt, kt: (bi * n + kt[p], 0)))
        in_specs.append(pl.BlockSpec((t, LANES), lambda bi, p, qt, kt: (bi * n + qt[p], 0)))
        in_specs.append(pl.BlockSpec((t, LANES), lambda bi, p, qt, kt: (bi * n + kt[p], 0)))
        args += [key_bias, norms, norms]
    scratch = [pltpu.VMEM((N_HEADS, t), F32), pltpu.VMEM((N_HEADS, t), F32), pltpu.VMEM((wv, t), F32)]
    if extra == "key_bias":
        scratch.append(pltpu.VMEM((1, LANES), F32))
    grid_spec = pltpu.PrefetchScalarGridSpec(
        num_scalar_prefetch=2,
        grid=(b, n * (n + 1) // 2),
        in_specs=in_specs,
        out_specs=pl.BlockSpec((t, wv), lambda bi, p, qt, kt: (bi * n + qt[p], 0)),
        scratch_shapes=scratch,
    )
    return pl.pallas_call(
        functools.partial(_flash_kernel, extra=extra, dk=dk, t=t),
        grid_spec=grid_spec,
        out_shape=jax.ShapeDtypeStruct((b * s, wv), BF16),
        compiler_params=_params("parallel", "arbitrary"),
        name=name,
    )(qt, kt, *args)


def _count(sc_ref, nvalid, cand, tk, tq, ind_fn):
    group = _count_group(sc_ref.shape[0])

    lanes = COUNT_ACCS * 8

    def body(g, a):
        for e in range(group):
            c = g * group + e
            ind = ind_fn(sc_ref[c], cand, c)
            a = a + jnp.sum(ind.reshape(tk // lanes, lanes, tq), axis=0)
        return a

    part = lax.fori_loop(0, (nvalid + group - 1) // group, body, jnp.zeros((lanes, tq), jnp.int32))
    return jnp.sum(part, axis=0, keepdims=True)


def _count_group(nchunks):
    return 2 if nchunks % 2 == 0 else 1


def _indexer_kernel(q_ref, wt_ref, k2_ref, mask_ref, sc_ref, qm_scr, j_scr, *, tq, tk, n_sel, idx_bits):
    qi = pl.program_id(1)
    one, zero = jnp.int32(1), jnp.int32(0)
    nchunks = sc_ref.shape[0]
    nvalid = ((qi + 1) * tq + tk - 1) // tk
    qcol = qi * tq + lax.broadcasted_iota(jnp.int32, (1, tq), 1)
    krow0 = lax.broadcasted_iota(jnp.int32, (tk, 1), 0)

    half = lax.broadcasted_iota(jnp.int32, (tq, LANES), 1) // IDX_DIM
    for g in range(IDX_HEADS // 2):
        qp = q_ref[:, g * LANES:(g + 1) * LANES]
        for e in range(2):
            qm_scr[2 * g + e] = jnp.where(half == e, qp, jnp.zeros_like(qp))

    def score_chunk(c, kmax):
        kc = k2_ref[pl.ds(pl.multiple_of(c * tk, tk), tk), :]
        acc = jnp.zeros((tk, tq), F32)
        for h in range(IDX_HEADS):
            x = lax.dot_general(kc, qm_scr[h], (((1,), (1,)), ((), ())), preferred_element_type=F32)
            acc = acc + wt_ref[h:h + 1, :] * jnp.maximum(x, 0.0)
        score = acc * (IDX_HEADS * IDX_DIM) ** -0.5
        bits = pltpu.bitcast(score, jnp.int32)
        key = bits ^ ((bits >> 31) & 0x7FFFFFFF)
        causal = (c * tk + krow0) <= qcol
        key = jnp.where(causal, key, INT_MIN)
        sc_ref[c] = key
        return jnp.maximum(kmax, jnp.max(key.reshape(tk // 8, 8, tq), axis=0))

    kmax = lax.fori_loop(0, nvalid, score_chunk, jnp.full((8, tq), INT_MIN, jnp.int32))
    kmax = jnp.max(kmax, axis=0, keepdims=True)

    group = _count_group(nchunks)
    if group > 1:
        def pad_chunk(c, carry):
            sc_ref[c] = jnp.full((tk, tq), INT_MIN, jnp.int32)
            return carry

        lax.fori_loop(nvalid, (nvalid + group - 1) // group * group, pad_chunk, 0)

    def count_ge(cand):
        return _count(sc_ref, nvalid, cand, tk, tq, lambda k, cd, c: jnp.where(k >= cd, one, zero))

    n_keys = qcol + 1
    f0 = count_ge(jnp.zeros((1, tq), jnp.int32))
    pos = f0 >= n_sel
    lo0 = jnp.where(pos, 0, INT_MIN + 1)
    hi0 = jnp.where(pos, kmax, jnp.minimum(kmax, -1)) + 1
    flo0 = jnp.where(pos, f0, n_keys)

    def searching(flo, w):
        return jnp.where(flo > n_sel, jnp.where(w > 1, 1.0, 0.0), 0.0)

    def bisect_cond(carry):
        it, lo, w, flo = carry
        return (jnp.max(searching(flo, w)) > 0.0) & (it < 32)

    def bisect_body(carry):
        it, lo, w, flo = carry
        active = searching(flo, w) > 0
        off = w >> 1
        cand = lo + off
        cnt = count_ge(cand)
        ok = cnt >= n_sel
        lo = jnp.where(active, jnp.where(ok, cand, lo), lo)
        flo = jnp.where(active, jnp.where(ok, cnt, flo), flo)
        w = jnp.where(active, jnp.where(ok, w - off, off), w)
        return it + 1, lo, w, flo

    _, lo, _, flo = lax.while_loop(bisect_cond, bisect_body, (jnp.int32(0), lo0, hi0 - lo0, flo0))
    found = n_keys >= n_sel
    tie = jnp.where(found, jnp.where(flo > n_sel, one, zero), zero) > 0
    gt_thr = jnp.where(found, jnp.where(tie, lo, lo - 1), INT_MIN)
    thr = jnp.where(tie, lo, INT_MIN + 1)

    j_scr[...] = jnp.full_like(j_scr, 2 ** idx_bits - 1)
    has_tie = jnp.max(jnp.where(tie, 1.0, 0.0)) > 0.0

    @pl.when(has_tie)
    def _():
        cgt = _count(sc_ref, nvalid, thr, tk, tq, lambda k, cd, c: jnp.where(k > cd, one, zero))
        need = n_sel - cgt

        def jbit(i, jv):
            cand = jv | lax.shift_left(jnp.int32(1), idx_bits - 1 - i)
            cnt = _count(sc_ref, nvalid, cand, tk, tq,
                         lambda k, cd, c: jnp.where(k == thr, jnp.where((c * tk + krow0) < cd, one, zero), zero))
            return jnp.where(cnt < need, cand, jv)

        j_scr[...] = lax.fori_loop(0, idx_bits, jbit, jnp.zeros((1, tq), jnp.int32))

    jlast = j_scr[...]

    def write_chunk(c, carry):
        key = sc_ref[c]
        sel = jnp.where(key > gt_thr, one,
                        jnp.where(key == thr, jnp.where((c * tk + krow0) <= jlast, one, zero), zero))
        mask_ref[pl.ds(pl.multiple_of(c * tk, tk), tk), :] = sel.astype(jnp.int8)
        return carry

    lax.fori_loop(0, nvalid, write_chunk, 0)

    def zero_chunk(c, carry):
        mask_ref[pl.ds(pl.multiple_of(c * tk, tk), tk), :] = jnp.zeros((tk, tq), jnp.int8)
        return carry

    lax.fori_loop(nvalid, nchunks, zero_chunk, 0)


def _indexer_mask(zb, wt, k2, b, s, tq, tk, n_sel):
    nq = s // tq
    idx_bits = max(1, int(np.ceil(np.log2(s))))
    return pl.pallas_call(
        functools.partial(_indexer_kernel, tq=tq, tk=tk, n_sel=n_sel, idx_bits=idx_bits),
        grid=(b, nq),
        in_specs=[
            pl.BlockSpec((tq, IDX_HEADS * IDX_DIM), lambda bi, i: (bi * nq + i, COL_IQ // (IDX_HEADS * IDX_DIM))),
            pl.BlockSpec((None, IDX_HEADS, tq), lambda bi, i: (bi, 0, i)),
            pl.BlockSpec((None, s, LANES), lambda bi, i: (bi, 0, 0)),
        ],
        out_specs=pl.BlockSpec((None, s, tq), lambda bi, i: (bi, 0, i)),
        out_shape=jax.ShapeDtypeStruct((b, s, s), jnp.int8),
        scratch_shapes=[
            pltpu.VMEM((s // tk, tk, tq), jnp.int32),
            pltpu.VMEM((IDX_HEADS, tq, LANES), BF16),
            pltpu.VMEM((1, tq), jnp.int32),
        ],
        compiler_params=_params("parallel", "arbitrary"),
        name="indexer_mask",
    )(zb, wt, k2)


def _mla_up_kernel(c_ref, zs_ref, qg_ref, kvg_ref, wq_ref, wk_ref, wv_ref, c64_ref, s64l_ref, s64h_ref,
                   q_ref, k_ref, v_ref, *, q_scale):
    def norm(xf, g):
        y = xf * lax.rsqrt(jnp.mean(xf * xf, axis=-1, keepdims=True) + EPS)
        return (y * g).astype(BF16)

    cq = norm(c_ref[:, :Q_LORA].astype(F32), qg_ref[...])
    ckv = norm(c_ref[:, Q_LORA:].astype(F32), kvg_ref[...])
    q = jnp.dot(cq, wq_ref[...], preferred_element_type=F32) * q_scale
    k = jnp.dot(ckv, wk_ref[...], preferred_element_type=F32)
    v_ref[...] = jnp.dot(ckv, wv_ref[...], preferred_element_type=F32).T.astype(BF16)
    lane = lax.broadcasted_iota(jnp.int32, (1, LANES), 1)
    k_rot = jnp.where(lane < QK_ROPE, zs_ref[:, LANES:], 0.0)
    cos, sin_lo, sin_hi = c64_ref[...], s64l_ref[...], s64h_ref[...]
    for h in range(N_HEADS):
        nope = slice(h * MLA_QK_PAD, h * MLA_QK_PAD + QK_NOPE)
        rope = slice(h * MLA_QK_PAD + QK_NOPE, (h + 1) * MLA_QK_PAD)
        q_ref[:, nope] = q[:, nope].astype(BF16)
        q_ref[:, rope] = _rope64(q[:, rope], cos, sin_lo, sin_hi).astype(BF16)
        k_ref[:, nope] = k[:, nope].astype(BF16)
        k_ref[:, rope] = k_rot.astype(BF16)


def _mla_up(zb, zs, qg, kvg, wq, wk, wv, tabs, tm):
    m = zb.shape[0]
    _, _, c64, s64l, s64h = tabs
    nsb = c64.shape[0] // tm
    tab_spec = pl.BlockSpec((tm, LANES), lambda i: (i % nsb, 0))
    wqk = N_HEADS * MLA_QK_PAD
    full = lambda a: pl.BlockSpec(a.shape, lambda i: (0,) * a.ndim)
    return pl.pallas_call(
        functools.partial(_mla_up_kernel, q_scale=(QK_NOPE + QK_ROPE) ** -0.5 * LOG2E),
        grid=(m // tm,),
        in_specs=[
            pl.BlockSpec((tm, Q_LORA + KV_LORA), lambda i: (i, COL_CQ // (Q_LORA + KV_LORA))),
            pl.BlockSpec((tm, 2 * LANES), lambda i: (i, 0)),
            full(qg), full(kvg), full(wq), full(wk), full(wv),
            tab_spec, tab_spec, tab_spec,
        ],
        out_specs=[
            pl.BlockSpec((tm, wqk), lambda i: (i, 0)),
            pl.BlockSpec((tm, wqk), lambda i: (i, 0)),
            pl.BlockSpec((None, BRANCH_WIDTH, tm), lambda i: (i // nsb, 0, i % nsb)),
        ],
        out_shape=[
            jax.ShapeDtypeStruct((m, wqk), BF16),
            jax.ShapeDtypeStruct((m, wqk), BF16),
            jax.ShapeDtypeStruct((m // (nsb * tm), BRANCH_WIDTH, nsb * tm), BF16),
        ],
        compiler_params=_params("parallel"),
        name="mla_up",
    )(zb, zs, qg, kvg, wq, wk, wv, c64, s64l, s64h)


def _branch_kernel(o0_ref, o1_ref, o2_ref, g0_ref, g1_ref, g2_ref, m0_ref, m1_ref, m2_ref, w_ref, y_ref):
    y = None
    for i, (o_ref, g_ref, m_ref) in enumerate(((o0_ref, g0_ref, m0_ref), (o1_ref, g1_ref, m1_ref),
                                               (o2_ref, g2_ref, m2_ref))):
        g = g_ref[...].astype(F32)
        a = (o_ref[...].astype(F32) * (g * jax.nn.sigmoid(g))).astype(BF16)
        term = jax.nn.sigmoid(m_ref[...].astype(F32)) * jnp.dot(a, w_ref[i], preferred_element_type=F32)
        y = term if y is None else y + term
    y_ref[...] = y.astype(BF16)


def _branch_merge(o_fox, o_dsa, o_mla, zb, wb, d, tm, tn):
    m = zb.shape[0]
    o_spec = pl.BlockSpec((tm, BRANCH_WIDTH), lambda i, j: (i, 0))
    silu0 = (COL_MERGE + N_BRANCH * d) // BRANCH_WIDTH

    def g_spec(i_br):
        return pl.BlockSpec((tm, BRANCH_WIDTH), lambda i, j: (i, silu0 + i_br))

    def m_spec(i_br):
        base = (COL_MERGE + i_br * d) // tn
        return pl.BlockSpec((tm, tn), lambda i, j: (i, base + j))

    return pl.pallas_call(
        _branch_kernel,
        grid=(m // tm, d // tn),
        in_specs=[o_spec, o_spec, o_spec, g_spec(0), g_spec(1), g_spec(2), m_spec(0), m_spec(1), m_spec(2),
                  pl.BlockSpec((N_BRANCH, BRANCH_WIDTH, tn), lambda i, j: (0, 0, j))],
        out_specs=pl.BlockSpec((tm, tn), lambda i, j: (i, j)),
        out_shape=jax.ShapeDtypeStruct((m, d), BF16),
        compiler_params=_params("parallel", "arbitrary"),
        name="branch_merge",
    )(o_fox, o_dsa, o_mla, zb, zb, zb, zb, zb, zb, wb)


def _out_kernel(x_ref, y_ref, w_ref, g_ref, o_ref, *, final_norm):
    r = x_ref[...] + jnp.dot(y_ref[...], w_ref[...], preferred_element_type=F32)
    if final_norm:
        r = r * lax.rsqrt(jnp.mean(r * r, axis=-1, keepdims=True) + EPS) * g_ref[...]
    o_ref[...] = r


def _out_proj(x2, y, w, g, tm, final_norm):
    m, d = x2.shape
    return pl.pallas_call(
        functools.partial(_out_kernel, final_norm=final_norm),
        grid=(m // tm,),
        in_specs=[
            pl.BlockSpec((tm, d), lambda i: (i, 0)),
            pl.BlockSpec((tm, d), lambda i: (i, 0)),
            pl.BlockSpec((d, d), lambda i: (0, 0)),
            pl.BlockSpec((1, d), lambda i: (0, 0)),
        ],
        out_specs=pl.BlockSpec((tm, d), lambda i: (i, 0)),
        out_shape=jax.ShapeDtypeStruct((m, d), F32),
        compiler_params=_params("parallel"),
        name="out_proj",
    )(x2, y, w, g)


def _split_w_in(w_in, d):
    sizes = (BRANCH_WIDTH, BRANCH_WIDTH, BRANCH_WIDTH, N_HEADS, BRANCH_WIDTH, BRANCH_WIDTH, BRANCH_WIDTH,
             IDX_HEADS * IDX_DIM, IDX_DIM, IDX_HEADS, Q_LORA, KV_LORA, QK_ROPE,
             N_BRANCH * BRANCH_WIDTH, N_BRANCH * d)
    pts = np.cumsum(sizes)[:-1].tolist()
    (f_q, f_k, f_v, f_gate, d_q, d_k, d_v, i_q, i_k, i_w, c_q, c_kv, k_pe, silu_g, merge_g) = jnp.split(
        w_in, pts, axis=-1)
    big = jnp.concatenate([f_q, f_k, f_v, d_q, d_k, d_v, i_q, c_q, c_kv, merge_g, silu_g], axis=-1).astype(BF16)
    pad = jnp.zeros((w_in.shape[0], LANES - FOX_BIAS_COLS - IDX_HEADS), w_in.dtype)
    small = jnp.concatenate([f_gate, f_gate, f_gate, i_w, pad, k_pe, i_k], axis=-1).astype(BF16)
    return big, small


def _pad_heads(w, take, width_in):
    r = w.shape[0]
    w3 = w.reshape(r, N_HEADS, width_in)[:, :, take]
    w3 = jnp.pad(w3, ((0, 0), (0, 0), (0, MLA_QK_PAD - w3.shape[-1])))
    return w3.reshape(r, N_HEADS * MLA_QK_PAD)


def _tile(s, pref):
    return pref if s % pref == 0 else s


def kernel(x, norm_g, w_in, forget_b, q_norm_g, w_q_up, kv_norm_g, w_kv_up, w_branch, w_out, final_norm_g):
    b, s, d = x.shape
    depth = w_in.shape[0]
    n_sel = min(TOPK_MAX, s // 4)
    m = b * s
    tabs = _rope_tables(s)
    t_att = _tile(s, 512)
    tm_in = _tile(s, 1024)
    x2 = x.reshape(m, d)
    for l in range(depth):
        w_big, w_small = _split_w_in(w_in[l], d)
        zb, zs, vt, norms = _in_proj(x2, norm_g[l].reshape(1, d), w_big, w_small, tabs, tm_in, 512)

        key_bias = _forget_cumsum(zs, forget_b[l].reshape(1, N_HEADS), b, s, _tile(s, 1024))
        o_fox = _flash(zb, COL_FQ // BRANCH_WIDTH, zb, COL_FK // BRANCH_WIDTH, vt, 0,
                       b, s, t_att, HEAD_DIM, "flash_fox", key_bias=key_bias, norms=norms)

        zs3 = zs.reshape(b, s, 2 * LANES)
        wt = jnp.swapaxes(zs3[:, :, FOX_BIAS_COLS:FOX_BIAS_COLS + IDX_HEADS], 1, 2)
        ik = zs3[:, :, LANES + QK_ROPE:].astype(BF16)
        k2 = jnp.concatenate([ik, ik], axis=-1)
        mask = _indexer_mask(zb, wt, k2, b, s, _tile(s, 256), _tile(s, 512), n_sel)
        o_dsa = _flash(zb, COL_DQ // BRANCH_WIDTH, zb, COL_DK // BRANCH_WIDTH, vt, 1,
                       b, s, t_att, HEAD_DIM, "flash_mask", mask=mask)

        wq = _pad_heads(w_q_up[l], slice(0, QK_NOPE + QK_ROPE), QK_NOPE + QK_ROPE).astype(BF16)
        wk = _pad_heads(w_kv_up[l], slice(0, QK_NOPE), QK_NOPE + V_DIM).astype(BF16)
        wv = w_kv_up[l].reshape(KV_LORA, N_HEADS, QK_NOPE + V_DIM)[:, :, QK_NOPE:].reshape(
            KV_LORA, BRANCH_WIDTH).astype(BF16)
        q_mla, k_mla, vt_mla = _mla_up(zb, zs, q_norm_g[l].reshape(1, Q_LORA), kv_norm_g[l].reshape(1, KV_LORA),
                                       wq, wk, wv, tabs, _tile(s, 512))
        o_mla = _flash(q_mla, 0, k_mla, 0, vt_mla, 0, b, s, t_att, MLA_QK_PAD, "flash_plain")

        y = _branch_merge(o_fox, o_dsa, o_mla, zb, w_branch[l].astype(BF16), d, _tile(s, 512), 1024)
        last = l == depth - 1
        x2 = _out_proj(x2, y, w_out[l].astype(BF16), final_norm_g.reshape(1, d), _tile(s, 512), last)
    return x2.reshape(b, s, d)
```

```python
import functools

import jax
import jax.numpy as jnp
import numpy as np
from jax import lax
from jax.experimental import pallas as pl
from jax.experimental.pallas import tpu as pltpu

F32 = jnp.float32
BF16 = jnp.bfloat16

HEAD_DIM = 128
N_HEADS = 8
BRANCH_WIDTH = N_HEADS * HEAD_DIM
N_BRANCH = 3
IDX_HEADS = 16
IDX_DIM = 64
TOPK_MAX = 256
Q_LORA = 512
KV_LORA = 512
QK_NOPE = 128
QK_ROPE = 64
V_DIM = 128
MLA_QK_PAD = 256
FOX_BIAS_COLS = 3 * N_HEADS
NORM_SLACK = 1.02
UNDERFLOW_LOG2 = 160.0
ONES_ROWS = 16
ROPE_THETA = 10000.0
EPS = 1e-6

LANES = 128
VMEM_LIMIT = 56 * 1024 * 1024
NEG_BIG = -1e30
INT_MIN = -(2 ** 31)
LOG2E = 1.4426950408889634
Q_STRIP = 256
COUNT_ACCS = 4
PIPE_LEAD = 2
PIPE_LAG = 2

COL_FQ, COL_FK, COL_FV = 0, 1024, 2048
COL_DQ, COL_DK, COL_DV = 3072, 4096, 5120
COL_IQ = 6144
COL_CQ = 7168
COL_MERGE = 8192


def _params(*sem):
    return pltpu.CompilerParams(dimension_semantics=sem, vmem_limit_bytes=VMEM_LIMIT)


def _rope_tables(s):
    pos = jnp.arange(s, dtype=F32)

    def cs(half):
        inv = ROPE_THETA ** (-jnp.arange(half, dtype=F32) / half)
        ang = pos[:, None] * inv[None, :]
        return jnp.cos(ang), jnp.sin(ang)

    c64, s64 = cs(64)
    c32, s32 = cs(32)
    z32 = jnp.zeros_like(s32)
    cos128 = jnp.concatenate([c64, c64], axis=-1)
    sin128 = jnp.concatenate([-s64, s64], axis=-1)
    cos64 = jnp.concatenate([c32, c32, c32, c32], axis=-1)
    sin64_lo = jnp.concatenate([-s32, z32, -s32, z32], axis=-1)
    sin64_hi = jnp.concatenate([z32, s32, z32, s32], axis=-1)
    return cos128, sin128, cos64, sin64_lo, sin64_hi


def _rope128(x, cos, sin):
    return x * cos + pltpu.roll(x, 64, 1) * sin


def _rope64(x, cos, sin_lo, sin_hi):
    return x * cos + pltpu.roll(x, 96, 1) * sin_lo + pltpu.roll(x, 32, 1) * sin_hi


def _in_proj_kernel(x_ref, g_ref, w_ref, ws_ref, c128_ref, s128_ref, c64_ref, s64l_ref, s64h_ref,
                    zb_ref, zs_ref, vt_ref, nrm_ref, h_scr, *, tn, q_scale):
    j = pl.program_id(1)

    @pl.when(j == 0)
    def _():
        nrm_ref[...] = jnp.zeros_like(nrm_ref)
        xf = x_ref[...]
        y = xf * lax.rsqrt(jnp.mean(xf * xf, axis=-1, keepdims=True) + EPS)
        h = (y * g_ref[...]).astype(BF16)
        h_scr[...] = h
        zs = jnp.dot(h, ws_ref[...], preferred_element_type=F32)
        zs_ref[:, :LANES] = zs[:, :LANES]
        zs_ref[:, LANES:] = _rope64(zs[:, LANES:], c64_ref[...], s64l_ref[...], s64h_ref[...])

    acc = jnp.dot(h_scr[...], w_ref[...], preferred_element_type=F32)
    col = j * tn
    is_q = ((col >= COL_FQ) & (col < COL_FK)) | ((col >= COL_DQ) & (col < COL_DK))
    acc = acc * jnp.where(is_q, q_scale, 1.0).astype(F32)
    is_r128 = (col >= COL_DQ) & (col < COL_DV)
    is_r64 = (col >= COL_IQ) & (col < COL_CQ)

    @pl.when(is_r128)
    def _():
        for g in range(tn // LANES):
            sl = slice(g * LANES, (g + 1) * LANES)
            zb_ref[:, sl] = _rope128(acc[:, sl], c128_ref[...], s128_ref[...]).astype(BF16)

    @pl.when(is_r64)
    def _():
        for g in range(tn // LANES):
            sl = slice(g * LANES, (g + 1) * LANES)
            zb_ref[:, sl] = _rope64(acc[:, sl], c64_ref[...], s64l_ref[...], s64h_ref[...]).astype(BF16)

    @pl.when(jnp.logical_not(is_r128 | is_r64))
    def _():
        zb_ref[...] = acc.astype(BF16)

    is_v = ((col >= COL_FV) & (col < COL_DQ)) | ((col >= COL_DV) & (col < COL_IQ))

    @pl.when(is_v)
    def _():
        vt_ref[...] = acc.T.astype(BF16)

    is_fq = (col >= COL_FQ) & (col < COL_FK)
    is_fk = (col >= COL_FK) & (col < COL_FV)

    @pl.when(is_fq | is_fk)
    def _():
        lane = lax.broadcasted_iota(jnp.int32, (1, LANES), 1)
        first = (col % BRANCH_WIDTH) // HEAD_DIM + jnp.where(is_fq, N_HEADS, 0)
        nrm = nrm_ref[...]
        for g in range(tn // HEAD_DIM):
            a = acc[:, g * HEAD_DIM:(g + 1) * HEAD_DIM]
            nrm = jnp.where(lane == first + g, jnp.sum(a * a, axis=1, keepdims=True), nrm)
        nrm_ref[...] = nrm


def _vt_row_block(j, tn):
    per = BRANCH_WIDTH // tn
    fv0, dv0 = COL_FV // tn, COL_DV // tn
    in_fox = jnp.clip(j - fv0, 0, per - 1)
    in_dsa = per + jnp.clip(j - dv0, 0, per - 1)
    return jnp.where(j < dv0, in_fox, in_dsa)


def _in_proj(x2, g, w_big, w_small, tabs, tm, tn):
    m, d = x2.shape
    n = w_big.shape[1]
    c128, s128, c64, s64l, s64h = tabs
    s = c128.shape[0]
    nsb = s // tm
    tab_spec = pl.BlockSpec((tm, LANES), lambda i, j: (i % nsb, 0))
    return pl.pallas_call(
        functools.partial(_in_proj_kernel, tn=tn, q_scale=HEAD_DIM ** -0.5 * LOG2E),
        grid=(m // tm, n // tn),
        in_specs=[
            pl.BlockSpec((tm, d), lambda i, j: (i, 0)),
            pl.BlockSpec((1, d), lambda i, j: (0, 0)),
            pl.BlockSpec((d, tn), lambda i, j: (0, j)),
            pl.BlockSpec((d, 2 * LANES), lambda i, j: (0, 0)),
            tab_spec, tab_spec, tab_spec, tab_spec, tab_spec,
        ],
        out_specs=[
            pl.BlockSpec((tm, tn), lambda i, j: (i, j)),
            pl.BlockSpec((tm, 2 * LANES), lambda i, j: (i, 0)),
            pl.BlockSpec((None, tn, tm), lambda i, j: (i // nsb, _vt_row_block(j, tn), i % nsb)),
            pl.BlockSpec((tm, LANES), lambda i, j: (i, 0)),
        ],
        out_shape=[
            jax.ShapeDtypeStruct((m, n), BF16),
            jax.ShapeDtypeStruct((m, 2 * LANES), F32),
            jax.ShapeDtypeStruct((m // s, 2 * BRANCH_WIDTH, s), BF16),
            jax.ShapeDtypeStruct((m, LANES), F32),
        ],
        scratch_shapes=[pltpu.VMEM((tm, d), BF16)],
        compiler_params=_params("parallel", "arbitrary"),
        name="in_proj",
    )(x2, g, w_big, w_small, c128, s128, c64, s64l, s64h)


def _bf16_floor(x):
    return pltpu.bitcast(pltpu.bitcast(x, jnp.int32) & jnp.int32(-65536), F32)


def _cum_kernel(zs_ref, fb_ref, kb_ref, carry_scr, *, t):
    @pl.when(pl.program_id(1) == 0)
    def _():
        carry_scr[...] = jnp.zeros_like(carry_scr)

    lane = lax.broadcasted_iota(jnp.int32, (1, LANES), 1)
    x = zs_ref[:, :LANES] + fb_ref[...]
    logf = -(jnp.maximum(-x, 0.0) + jnp.log1p(jnp.exp(-jnp.abs(x))))
    logf = jnp.where(lane < FOX_BIAS_COLS, logf, 0.0)
    r = lax.broadcasted_iota(jnp.int32, (t, t), 0)
    c = lax.broadcasted_iota(jnp.int32, (t, t), 1)
    tri = jnp.where(c <= r, 1.0, 0.0).astype(F32)
    cum = jnp.dot(tri, logf, preferred_element_type=F32, precision=lax.Precision.HIGHEST) + carry_scr[...]
    carry_scr[...] = cum[t - 1:t, :]
    c = cum * (-LOG2E)
    hi = _bf16_floor(c)
    mid = _bf16_floor(c - hi)
    lo = (c - hi) - mid
    kb_ref[...] = jnp.where(lane < N_HEADS, hi, jnp.where(lane < 2 * N_HEADS, mid, lo)).astype(BF16)


def _forget_cumsum(zs, fb, b, s, t):
    fb128 = jnp.pad(jnp.tile(fb, (1, 3)), ((0, 0), (0, LANES - FOX_BIAS_COLS)))
    return pl.pallas_call(
        functools.partial(_cum_kernel, t=t),
        grid=(b, s // t),
        in_specs=[
            pl.BlockSpec((t, 2 * LANES), lambda bi, i: (bi * (s // t) + i, 0)),
            pl.BlockSpec((1, LANES), lambda bi, i: (0, 0)),
        ],
        out_specs=pl.BlockSpec((t, LANES), lambda bi, i: (bi * (s // t) + i, 0)),
        out_shape=jax.ShapeDtypeStruct((b * s, LANES), BF16),
        scratch_shapes=[pltpu.VMEM((1, LANES), F32)],
        compiler_params=_params("parallel", "arbitrary"),
        name="forget_cumsum",
    )(zs, fb128)


def _flash_kernel(qt_ref, kt_ref, q_ref, k_ref, v_ref, *rest, extra, dk, t):
    qn_scr = None
    if extra is None:
        o_ref, m_scr, l_scr, acc_scr = rest
        extra_ref = None
    elif extra == "mask":
        extra_ref, o_ref, m_scr, l_scr, acc_scr = rest
    else:
        extra_ref, qnorm_ref, knorm_ref, o_ref, m_scr, l_scr, acc_scr, qn_scr = rest
    masked = extra == "mask"
    p = pl.program_id(1)
    qi = qt_ref[p]
    ki = kt_ref[p]
    qs = min(t, Q_STRIP)

    @pl.when(ki == qi)
    def _():
        m_scr[...] = jnp.full_like(m_scr, NEG_BIG)
        l_scr[...] = jnp.zeros_like(l_scr)
        acc_scr[...] = jnp.zeros_like(acc_scr)
        if qn_scr is not None:
            qn_scr[...] = pltpu.roll(jnp.max(qnorm_ref[...], axis=0, keepdims=True), LANES - N_HEADS, 1)

    def head_margins():
        kn = jnp.max(knorm_ref[...], axis=0, keepdims=True)
        dot_bound = jnp.sqrt(qn_scr[...] * kn) * NORM_SLACK
        bm = jnp.max(extra_ref[...].astype(F32), axis=0, keepdims=True)
        bias = bm + pltpu.roll(bm, LANES - N_HEADS, 1) + pltpu.roll(bm, LANES - 2 * N_HEADS, 1)
        bound = dot_bound + bias
        m_min = jnp.min(m_scr[...], axis=1, keepdims=True)
        head = lax.broadcasted_iota(jnp.int32, (N_HEADS, LANES), 0)
        lane = lax.broadcasted_iota(jnp.int32, (N_HEADS, LANES), 1)
        margin = jnp.where(head == lane, m_min - bound, jnp.inf)
        return jnp.min(margin, axis=1, keepdims=True)

    all_units = [(c, h) for c in range(t // qs) for h in range(N_HEADS)]

    def step(diag, units):

        def n_keys(c):
            return (c + 1) * qs if diag else t

        def logits(c, h):
            k_h = k_ref[:n_keys(c), h * dk:(h + 1) * dk]
            q_h = q_ref[c * qs:(c + 1) * qs, h * dk:(h + 1) * dk]
            if extra == "key_bias":
                lane = lax.broadcasted_iota(jnp.int32, (qs, LANES), 1)
                pick = jnp.where(lane < FOX_BIAS_COLS, jnp.where(lane % N_HEADS == h, 1.0, 0.0), 0.0)
                k_h = jnp.concatenate([k_h, extra_ref[:n_keys(c), :]], axis=1)
                q_h = jnp.concatenate([q_h, pick.astype(q_h.dtype)], axis=1)
            return lax.dot_general(k_h, q_h, (((1,), (1,)), ((), ())), preferred_element_type=F32)

        def keep_mask(c):
            nk = n_keys(c)
            if masked:
                return extra_ref[:, c * qs:(c + 1) * qs].astype(jnp.int32) != 0
            if diag:
                kr = lax.broadcasted_iota(jnp.int32, (nk, qs), 0)
                qc = lax.broadcasted_iota(jnp.int32, (nk, qs), 1) + c * qs
                return kr <= qc
            return None

        keeps = {}

        def softmax(c, h, st):
            cs = slice(c * qs, (c + 1) * qs)
            if c not in keeps:
                keeps[c] = keep_mask(c)
            if keeps[c] is not None:
                st = jnp.where(keeps[c], st, NEG_BIG)
            m_prev = m_scr[h:h + 1, cs]
            m_new = jnp.maximum(m_prev, jnp.max(st, axis=0, keepdims=True))
            m_scr[h:h + 1, cs] = m_new
            alpha = jnp.exp2(m_prev - m_new)
            return alpha, jnp.exp2(st - m_new).astype(BF16)

        def accumulate(c, h, alpha, pt):
            cs = slice(c * qs, (c + 1) * qs)
            nk = n_keys(c)
            ones = jnp.ones((ONES_ROWS, nk), v_ref.dtype)
            vt_h = jnp.concatenate([v_ref[h * HEAD_DIM:(h + 1) * HEAD_DIM, :nk], ones], axis=0)
            pv = jnp.dot(vt_h, pt, preferred_element_type=F32)
            sl = slice(h * HEAD_DIM, (h + 1) * HEAD_DIM)
            acc_scr[sl, cs] = alpha * acc_scr[sl, cs] + pv[:HEAD_DIM]
            l_scr[h:h + 1, cs] = alpha * l_scr[h:h + 1, cs] + pv[HEAD_DIM:HEAD_DIM + 1]

        n_u = len(units)
        st_q = [logits(*units[i]) for i in range(min(PIPE_LEAD, n_u))]
        p_q = []
        for u in range(n_u + PIPE_LAG):
            if u + PIPE_LEAD < n_u:
                st_q.append(logits(*units[u + PIPE_LEAD]))
            if u < n_u:
                p_q.append(softmax(*units[u], st_q.pop(0)))
            if u >= PIPE_LAG:
                accumulate(*units[u - PIPE_LAG], *p_q.pop(0))

    if masked:
        step(False, all_units)
    else:
        @pl.when(ki == qi)
        def _():
            step(True, all_units)

        if qn_scr is None:
            @pl.when(ki != qi)
            def _():
                step(False, all_units)
        else:
            margins = head_margins()
            off_diag = ki != qi
            every_head = jnp.logical_and(off_diag, jnp.max(margins) <= UNDERFLOW_LOG2)

            @pl.when(every_head)
            def _():
                step(False, all_units)

            some_heads = jnp.logical_and(off_diag, jnp.logical_not(every_head))
            for h in range(N_HEADS):
                @pl.when(jnp.logical_and(some_heads, margins[h, 0] <= UNDERFLOW_LOG2))
                def _(h=h):
                    step(False, [(c, h) for c in range(t // qs)])

    @pl.when(ki == 0)
    def _():
        for h in range(N_HEADS):
            sl = slice(h * HEAD_DIM, (h + 1) * HEAD_DIM)
            o = acc_scr[sl, :] / l_scr[h:h + 1, :]
            o_ref[:, sl] = o.T.astype(BF16)


def _pair_tables(n):
    qt = np.concatenate([np.full(i + 1, i, np.int32) for i in range(n)])
    kt = np.concatenate([np.arange(i, -1, -1, dtype=np.int32) for i in range(n)])
    return jnp.asarray(qt), jnp.asarray(kt)


def _flash(q_arr, q_col, k_arr, k_col, vt_arr, vt_blk, b, s, t, dk, name, mask=None, key_bias=None, norms=None):
    n = s // t
    qt, kt = _pair_tables(n)
    wq = N_HEADS * dk
    wv = N_HEADS * HEAD_DIM
    in_specs = [
        pl.BlockSpec((t, wq), lambda bi, p, qt, kt: (bi * n + qt[p], q_col)),
        pl.BlockSpec((t, wq), lambda bi, p, qt, kt: (bi * n + kt[p], k_col)),
        pl.BlockSpec((None, wv, t), lambda bi, p, qt, kt: (bi, vt_blk, kt[p])),
    ]
    args = [q_arr, k_arr, vt_arr]
    extra = None
    if mask is not None:
        extra = "mask"
        in_specs.append(pl.BlockSpec((None, t, t), lambda bi, p, qt, kt: (bi, kt[p], qt[p])))
        args.append(mask)
    elif key_bias is not None:
        extra = "key_bias"
        in_specs.append(pl.BlockSpec((t, LANES), lambda bi, p, qt, kt: (bi * n + kt[p], 0)))
        in_specs.append(pl.BlockSpec((t, LANES), lambda bi, p, qt, kt: (bi * n + qt[p], 0)))
        in_specs.append(pl.BlockSpec((t, LANES), lambda bi, p, qt, kt: (bi * n + kt[p], 0)))
        args += [key_bias, norms, norms]
    scratch = [pltpu.VMEM((N_HEADS, t), F32), pltpu.VMEM((N_HEADS, t), F32), pltpu.VMEM((wv, t), F32)]
    if extra == "key_bias":
        scratch.append(pltpu.VMEM((1, LANES), F32))
    grid_spec = pltpu.PrefetchScalarGridSpec(
        num_scalar_prefetch=2,
        grid=(b, n * (n + 1) // 2),
        in_specs=in_specs,
        out_specs=pl.BlockSpec((t, wv), lambda bi, p, qt, kt: (bi * n + qt[p], 0)),
        scratch_shapes=scratch,
    )
    return pl.pallas_call(
        functools.partial(_flash_kernel, extra=extra, dk=dk, t=t),
        grid_spec=grid_spec,
        out_shape=jax.ShapeDtypeStruct((b * s, wv), BF16),
        compiler_params=_params("parallel", "arbitrary"),
        name=name,
    )(qt, kt, *args)


def _count(sc_ref, nvalid, cand, tk, tq, ind_fn):
    group = _count_group(sc_ref.shape[0])

    lanes = COUNT_ACCS * 8

    def body(g, a):
        for e in range(group):
            c = g * group + e
            ind = ind_fn(sc_ref[c], cand, c)
            a = a + jnp.sum(ind.reshape(tk // lanes, lanes, tq), axis=0)
        return a

    part = lax.fori_loop(0, (nvalid + group - 1) // group, body, jnp.zeros((lanes, tq), jnp.int32))
    return jnp.sum(part, axis=0, keepdims=True)


def _count_group(nchunks):
    return 2 if nchunks % 2 == 0 else 1


def _indexer_kernel(q_ref, wt_ref, k2_ref, mask_ref, sc_ref, qm_scr, j_scr, *, tq, tk, n_sel, idx_bits):
    qi = pl.program_id(1)
    one, zero = jnp.int32(1), jnp.int32(0)
    nchunks = sc_ref.shape[0]
    nvalid = ((qi + 1) * tq + tk - 1) // tk
    qcol = qi * tq + lax.broadcasted_iota(jnp.int32, (1, tq), 1)
    krow0 = lax.broadcasted_iota(jnp.int32, (tk, 1), 0)

    half = lax.broadcasted_iota(jnp.int32, (tq, LANES), 1) // IDX_DIM
    for g in range(IDX_HEADS // 2):
        qp = q_ref[:, g * LANES:(g + 1) * LANES]
        for e in range(2):
            qm_scr[2 * g + e] = jnp.where(half == e, qp, jnp.zeros_like(qp))

    def score_chunk(c, carry):
        kmax, n_pos = carry
        kc = k2_ref[pl.ds(pl.multiple_of(c * tk, tk), tk), :]
        acc = jnp.zeros((tk, tq), F32)
        for h in range(IDX_HEADS):
            x = lax.dot_general(kc, qm_scr[h], (((1,), (1,)), ((), ())), preferred_element_type=F32)
            acc = acc + wt_ref[h:h + 1, :] * jnp.maximum(x, 0.0)
        score = acc * (IDX_HEADS * IDX_DIM) ** -0.5
        bits = pltpu.bitcast(score, jnp.int32)
        key = bits ^ ((bits >> 31) & 0x7FFFFFFF)
        causal = (c * tk + krow0) <= qcol
        key = jnp.where(causal, key, INT_MIN)
        sc_ref[c] = key
        nonneg = jnp.where(key >= 0, one, zero)
        return (jnp.maximum(kmax, jnp.max(key.reshape(tk // 8, 8, tq), axis=0)),
                n_pos + jnp.sum(nonneg.reshape(tk // 8, 8, tq), axis=0))

    kmax, n_pos = lax.fori_loop(0, nvalid, score_chunk,
                                (jnp.full((8, tq), INT_MIN, jnp.int32), jnp.zeros((8, tq), jnp.int32)))
    kmax = jnp.max(kmax, axis=0, keepdims=True)
    f0 = jnp.sum(n_pos, axis=0, keepdims=True)

    group = _count_group(nchunks)
    if group > 1:
        def pad_chunk(c, carry):
            sc_ref[c] = jnp.full((tk, tq), INT_MIN, jnp.int32)
            return carry

        lax.fori_loop(nvalid, (nvalid + group - 1) // group * group, pad_chunk, 0)

    def count_ge(cand):
        return _count(sc_ref, nvalid, cand, tk, tq, lambda k, cd, c: jnp.where(k >= cd, one, zero))

    n_keys = qcol + 1
    pos = f0 >= n_sel
    lo0 = jnp.where(pos, 0, INT_MIN + 1)
    hi0 = jnp.where(pos, kmax, jnp.minimum(kmax, -1)) + 1
    flo0 = jnp.where(pos, f0, n_keys)

    def searching(flo, w):
        return jnp.where(flo > n_sel, jnp.where(w > 1, 1.0, 0.0), 0.0)

    def bisect_cond(carry):
        it, lo, w, flo = carry
        return (jnp.max(searching(flo, w)) > 0.0) & (it < 32)

    def bisect_body(carry):
        it, lo, w, flo = carry
        active = searching(flo, w) > 0
        off = w >> 1
        cand = lo + off
        cnt = count_ge(cand)
        ok = cnt >= n_sel
        lo = jnp.where(active, jnp.where(ok, cand, lo), lo)
        flo = jnp.where(active, jnp.where(ok, cnt, flo), flo)
        w = jnp.where(active, jnp.where(ok, w - off, off), w)
        return it + 1, lo, w, flo

    _, lo, _, flo = lax.while_loop(bisect_cond, bisect_body, (jnp.int32(0), lo0, hi0 - lo0, flo0))
    found = n_keys >= n_sel
    tie = jnp.where(found, jnp.where(flo > n_sel, one, zero), zero) > 0
    gt_thr = jnp.where(found, jnp.where(tie, lo, lo - 1), INT_MIN)
    thr = jnp.where(tie, lo, INT_MIN + 1)

    j_scr[...] = jnp.full_like(j_scr, 2 ** idx_bits - 1)
    has_tie = jnp.max(jnp.where(tie, 1.0, 0.0)) > 0.0

    @pl.when(has_tie)
    def _():
        cgt = _count(sc_ref, nvalid, thr, tk, tq, lambda k, cd, c: jnp.where(k > cd, one, zero))
        need = n_sel - cgt

        def jbit(i, jv):
            cand = jv | lax.shift_left(jnp.int32(1), idx_bits - 1 - i)
            cnt = _count(sc_ref, nvalid, cand, tk, tq,
                         lambda k, cd, c: jnp.where(k == thr, jnp.where((c * tk + krow0) < cd, one, zero), zero))
            return jnp.where(cnt < need, cand, jv)

        j_scr[...] = lax.fori_loop(0, idx_bits, jbit, jnp.zeros((1, tq), jnp.int32))

    jlast = j_scr[...]

    def write_chunk(c, carry):
        key = sc_ref[c]
        sel = jnp.where(key > gt_thr, one,
                        jnp.where(key == thr, jnp.where((c * tk + krow0) <= jlast, one, zero), zero))
        mask_ref[pl.ds(pl.multiple_of(c * tk, tk), tk), :] = sel.astype(jnp.int8)
        return carry

    lax.fori_loop(0, nvalid, write_chunk, 0)

    def zero_chunk(c, carry):
        mask_ref[pl.ds(pl.multiple_of(c * tk, tk), tk), :] = jnp.zeros((tk, tq), jnp.int8)
        return carry

    lax.fori_loop(nvalid, nchunks, zero_chunk, 0)


def _indexer_mask(zb, wt, k2, b, s, tq, tk, n_sel):
    nq = s // tq
    idx_bits = max(1, int(np.ceil(np.log2(s))))
    return pl.pallas_call(
        functools.partial(_indexer_kernel, tq=tq, tk=tk, n_sel=n_sel, idx_bits=idx_bits),
        grid=(b, nq),
        in_specs=[
            pl.BlockSpec((tq, IDX_HEADS * IDX_DIM), lambda bi, i: (bi * nq + i, COL_IQ // (IDX_HEADS * IDX_DIM))),
            pl.BlockSpec((None, IDX_HEADS, tq), lambda bi, i: (bi, 0, i)),
            pl.BlockSpec((None, s, LANES), lambda bi, i: (bi, 0, 0)),
        ],
        out_specs=pl.BlockSpec((None, s, tq), lambda bi, i: (bi, 0, i)),
        out_shape=jax.ShapeDtypeStruct((b, s, s), jnp.int8),
        scratch_shapes=[
            pltpu.VMEM((s // tk, tk, tq), jnp.int32),
            pltpu.VMEM((IDX_HEADS, tq, LANES), BF16),
            pltpu.VMEM((1, tq), jnp.int32),
        ],
        compiler_params=_params("parallel", "arbitrary"),
        name="indexer_mask",
    )(zb, wt, k2)


def _mla_up_kernel(c_ref, zs_ref, qg_ref, kvg_ref, wq_ref, wk_ref, wv_ref, c64_ref, s64l_ref, s64h_ref,
                   q_ref, k_ref, v_ref, *, q_scale):
    def norm(xf, g):
        y = xf * lax.rsqrt(jnp.mean(xf * xf, axis=-1, keepdims=True) + EPS)
        return (y * g).astype(BF16)

    cq = norm(c_ref[:, :Q_LORA].astype(F32), qg_ref[...])
    ckv = norm(c_ref[:, Q_LORA:].astype(F32), kvg_ref[...])
    q = jnp.dot(cq, wq_ref[...], preferred_element_type=F32) * q_scale
    k = jnp.dot(ckv, wk_ref[...], preferred_element_type=F32)
    v_ref[...] = jnp.dot(ckv, wv_ref[...], preferred_element_type=F32).T.astype(BF16)
    lane = lax.broadcasted_iota(jnp.int32, (1, LANES), 1)
    k_rot = jnp.where(lane < QK_ROPE, zs_ref[:, LANES:], 0.0)
    cos, sin_lo, sin_hi = c64_ref[...], s64l_ref[...], s64h_ref[...]
    for h in range(N_HEADS):
        nope = slice(h * MLA_QK_PAD, h * MLA_QK_PAD + QK_NOPE)
        rope = slice(h * MLA_QK_PAD + QK_NOPE, (h + 1) * MLA_QK_PAD)
        q_ref[:, nope] = q[:, nope].astype(BF16)
        q_ref[:, rope] = _rope64(q[:, rope], cos, sin_lo, sin_hi).astype(BF16)
        k_ref[:, nope] = k[:, nope].astype(BF16)
        k_ref[:, rope] = k_rot.astype(BF16)


def _mla_up(zb, zs, qg, kvg, wq, wk, wv, tabs, tm):
    m = zb.shape[0]
    _, _, c64, s64l, s64h = tabs
    nsb = c64.shape[0] // tm
    tab_spec = pl.BlockSpec((tm, LANES), lambda i: (i % nsb, 0))
    wqk = N_HEADS * MLA_QK_PAD
    full = lambda a: pl.BlockSpec(a.shape, lambda i: (0,) * a.ndim)
    return pl.pallas_call(
        functools.partial(_mla_up_kernel, q_scale=(QK_NOPE + QK_ROPE) ** -0.5 * LOG2E),
        grid=(m // tm,),
        in_specs=[
            pl.BlockSpec((tm, Q_LORA + KV_LORA), lambda i: (i, COL_CQ // (Q_LORA + KV_LORA))),
            pl.BlockSpec((tm, 2 * LANES), lambda i: (i, 0)),
            full(qg), full(kvg), full(wq), full(wk), full(wv),
            tab_spec, tab_spec, tab_spec,
        ],
        out_specs=[
            pl.BlockSpec((tm, wqk), lambda i: (i, 0)),
            pl.BlockSpec((tm, wqk), lambda i: (i, 0)),
            pl.BlockSpec((None, BRANCH_WIDTH, tm), lambda i: (i // nsb, 0, i % nsb)),
        ],
        out_shape=[
            jax.ShapeDtypeStruct((m, wqk), BF16),
            jax.ShapeDtypeStruct((m, wqk), BF16),
            jax.ShapeDtypeStruct((m // (nsb * tm), BRANCH_WIDTH, nsb * tm), BF16),
        ],
        compiler_params=_params("parallel"),
        name="mla_up",
    )(zb, zs, qg, kvg, wq, wk, wv, c64, s64l, s64h)


def _branch_kernel(o0_ref, o1_ref, o2_ref, g0_ref, g1_ref, g2_ref, m0_ref, m1_ref, m2_ref, w_ref, y_ref):
    y = None
    for i, (o_ref, g_ref, m_ref) in enumerate(((o0_ref, g0_ref, m0_ref), (o1_ref, g1_ref, m1_ref),
                                               (o2_ref, g2_ref, m2_ref))):
        g = g_ref[...].astype(F32)
        a = (o_ref[...].astype(F32) * (g * jax.nn.sigmoid(g))).astype(BF16)
        term = jax.nn.sigmoid(m_ref[...].astype(F32)) * jnp.dot(a, w_ref[i], preferred_element_type=F32)
        y = term if y is None else y + term
    y_ref[...] = y.astype(BF16)


def _branch_merge(o_fox, o_dsa, o_mla, zb, wb, d, tm, tn):
    m = zb.shape[0]
    o_spec = pl.BlockSpec((tm, BRANCH_WIDTH), lambda i, j: (i, 0))
    silu0 = (COL_MERGE + N_BRANCH * d) // BRANCH_WIDTH

    def g_spec(i_br):
        return pl.BlockSpec((tm, BRANCH_WIDTH), lambda i, j: (i, silu0 + i_br))

    def m_spec(i_br):
        base = (COL_MERGE + i_br * d) // tn
        return pl.BlockSpec((tm, tn), lambda i, j: (i, base + j))

    return pl.pallas_call(
        _branch_kernel,
        grid=(m // tm, d // tn),
        in_specs=[o_spec, o_spec, o_spec, g_spec(0), g_spec(1), g_spec(2), m_spec(0), m_spec(1), m_spec(2),
                  pl.BlockSpec((N_BRANCH, BRANCH_WIDTH, tn), lambda i, j: (0, 0, j))],
        out_specs=pl.BlockSpec((tm, tn), lambda i, j: (i, j)),
        out_shape=jax.ShapeDtypeStruct((m, d), BF16),
        compiler_params=_params("parallel", "arbitrary"),
        name="branch_merge",
    )(o_fox, o_dsa, o_mla, zb, zb, zb, zb, zb, zb, wb)


def _out_kernel(x_ref, y_ref, w_ref, g_ref, o_ref, *, final_norm):
    r = x_ref[...] + jnp.dot(y_ref[...], w_ref[...], preferred_element_type=F32)
    if final_norm:
        r = r * lax.rsqrt(jnp.mean(r * r, axis=-1, keepdims=True) + EPS) * g_ref[...]
    o_ref[...] = r


def _out_proj(x2, y, w, g, tm, final_norm):
    m, d = x2.shape
    return pl.pallas_call(
        functools.partial(_out_kernel, final_norm=final_norm),
        grid=(m // tm,),
        in_specs=[
            pl.BlockSpec((tm, d), lambda i: (i, 0)),
            pl.BlockSpec((tm, d), lambda i: (i, 0)),
            pl.BlockSpec((d, d), lambda i: (0, 0)),
            pl.BlockSpec((1, d), lambda i: (0, 0)),
        ],
        out_specs=pl.BlockSpec((tm, d), lambda i: (i, 0)),
        out_shape=jax.ShapeDtypeStruct((m, d), F32),
        compiler_params=_params("parallel"),
        name="out_proj",
    )(x2, y, w, g)


def _split_w_in(w_in, d):
    sizes = (BRANCH_WIDTH, BRANCH_WIDTH, BRANCH_WIDTH, N_HEADS, BRANCH_WIDTH, BRANCH_WIDTH, BRANCH_WIDTH,
             IDX_HEADS * IDX_DIM, IDX_DIM, IDX_HEADS, Q_LORA, KV_LORA, QK_ROPE,
             N_BRANCH * BRANCH_WIDTH, N_BRANCH * d)
    pts = np.cumsum(sizes)[:-1].tolist()
    (f_q, f_k, f_v, f_gate, d_q, d_k, d_v, i_q, i_k, i_w, c_q, c_kv, k_pe, silu_g, merge_g) = jnp.split(
        w_in, pts, axis=-1)
    big = jnp.concatenate([f_q, f_k, f_v, d_q, d_k, d_v, i_q, c_q, c_kv, merge_g, silu_g], axis=-1).astype(BF16)
    pad = jnp.zeros((w_in.shape[0], LANES - FOX_BIAS_COLS - IDX_HEADS), w_in.dtype)
    small = jnp.concatenate([f_gate, f_gate, f_gate, i_w, pad, k_pe, i_k], axis=-1).astype(BF16)
    return big, small


def _pad_heads(w, take, width_in):
    r = w.shape[0]
    w3 = w.reshape(r, N_HEADS, width_in)[:, :, take]
    w3 = jnp.pad(w3, ((0, 0), (0, 0), (0, MLA_QK_PAD - w3.shape[-1])))
    return w3.reshape(r, N_HEADS * MLA_QK_PAD)


def _tile(s, pref):
    return pref if s % pref == 0 else s


def kernel(x, norm_g, w_in, forget_b, q_norm_g, w_q_up, kv_norm_g, w_kv_up, w_branch, w_out, final_norm_g):
    b, s, d = x.shape
    depth = w_in.shape[0]
    n_sel = min(TOPK_MAX, s // 4)
    m = b * s
    tabs = _rope_tables(s)
    t_att = _tile(s, 512)
    tm_in = _tile(s, 1024)
    x2 = x.reshape(m, d)
    for l in range(depth):
        w_big, w_small = _split_w_in(w_in[l], d)
        zb, zs, vt, norms = _in_proj(x2, norm_g[l].reshape(1, d), w_big, w_small, tabs, tm_in, 512)

        key_bias = _forget_cumsum(zs, forget_b[l].reshape(1, N_HEADS), b, s, _tile(s, 1024))
        o_fox = _flash(zb, COL_FQ // BRANCH_WIDTH, zb, COL_FK // BRANCH_WIDTH, vt, 0,
                       b, s, _tile(s, 1024), HEAD_DIM, "flash_fox", key_bias=key_bias, norms=norms)

        zs3 = zs.reshape(b, s, 2 * LANES)
        wt = jnp.swapaxes(zs3[:, :, FOX_BIAS_COLS:FOX_BIAS_COLS + IDX_HEADS], 1, 2)
        ik = zs3[:, :, LANES + QK_ROPE:].astype(BF16)
        k2 = jnp.concatenate([ik, ik], axis=-1)
        mask = _indexer_mask(zb, wt, k2, b, s, _tile(s, 256), _tile(s, 512), n_sel)
        o_dsa = _flash(zb, COL_DQ // BRANCH_WIDTH, zb, COL_DK // BRANCH_WIDTH, vt, 1,
                       b, s, t_att, HEAD_DIM, "flash_mask", mask=mask)

        wq = _pad_heads(w_q_up[l], slice(0, QK_NOPE + QK_ROPE), QK_NOPE + QK_ROPE).astype(BF16)
        wk = _pad_heads(w_kv_up[l], slice(0, QK_NOPE), QK_NOPE + V_DIM).astype(BF16)
        wv = w_kv_up[l].reshape(KV_LORA, N_HEADS, QK_NOPE + V_DIM)[:, :, QK_NOPE:].reshape(
            KV_LORA, BRANCH_WIDTH).astype(BF16)
        q_mla, k_mla, vt_mla = _mla_up(zb, zs, q_norm_g[l].reshape(1, Q_LORA), kv_norm_g[l].reshape(1, KV_LORA),
                                       wq, wk, wv, tabs, _tile(s, 512))
        o_mla = _flash(q_mla, 0, k_mla, 0, vt_mla, 0, b, s, t_att, MLA_QK_PAD, "flash_plain")

        y = _branch_merge(o_fox, o_dsa, o_mla, zb, w_branch[l].astype(BF16), d, _tile(s, 512), 1024)
        last = l == depth - 1
        x2 = _out_proj(x2, y, w_out[l].astype(BF16), final_norm_g.reshape(1, d), _tile(s, 512), last)
    return x2.reshape(b, s, d)
```

```python
import functools

import jax
import jax.numpy as jnp
import numpy as np
from jax import lax
from jax.experimental import pallas as pl
from jax.experimental.pallas import tpu as pltpu

F32 = jnp.float32
BF16 = jnp.bfloat16

HEAD_DIM = 128
N_HEADS = 8
BRANCH_WIDTH = N_HEADS * HEAD_DIM
N_BRANCH = 3
IDX_HEADS = 16
IDX_DIM = 64
TOPK_MAX = 256
Q_LORA = 512
KV_LORA = 512
QK_NOPE = 128
QK_ROPE = 64
V_DIM = 128
MLA_QK_PAD = 256
FOX_BIAS_COLS = 3 * N_HEADS
NORM_SLACK = 1.02
UNDERFLOW_LOG2 = 160.0
ONES_ROWS = 16
ROPE_THETA = 10000.0
EPS = 1e-6

LANES = 128
VMEM_LIMIT = 56 * 1024 * 1024
NEG_BIG = -1e30
INT_MIN = -(2 ** 31)
LOG2E = 1.4426950408889634
Q_STRIP = 256
FIRST_PROBE_BELOW_MAX = 2 ** 24
COUNT_ACCS = 4
PIPE_LEAD = 2
PIPE_LAG = 2

COL_FQ, COL_FK, COL_FV = 0, 1024, 2048
COL_DQ, COL_DK, COL_DV = 3072, 4096, 5120
COL_IQ = 6144
COL_CQ = 7168
COL_MERGE = 8192


def _params(*sem):
    return pltpu.CompilerParams(dimension_semantics=sem, vmem_limit_bytes=VMEM_LIMIT)


def _rope_tables(s):
    pos = jnp.arange(s, dtype=F32)

    def cs(half):
        inv = ROPE_THETA ** (-jnp.arange(half, dtype=F32) / half)
        ang = pos[:, None] * inv[None, :]
        return jnp.cos(ang), jnp.sin(ang)

    c64, s64 = cs(64)
    c32, s32 = cs(32)
    z32 = jnp.zeros_like(s32)
    cos128 = jnp.concatenate([c64, c64], axis=-1)
    sin128 = jnp.concatenate([-s64, s64], axis=-1)
    cos64 = jnp.concatenate([c32, c32, c32, c32], axis=-1)
    sin64_lo = jnp.concatenate([-s32, z32, -s32, z32], axis=-1)
    sin64_hi = jnp.concatenate([z32, s32, z32, s32], axis=-1)
    return cos128, sin128, cos64, sin64_lo, sin64_hi


def _rope128(x, cos, sin):
    return x * cos + pltpu.roll(x, 64, 1) * sin


def _rope64(x, cos, sin_lo, sin_hi):
    return x * cos + pltpu.roll(x, 96, 1) * sin_lo + pltpu.roll(x, 32, 1) * sin_hi


def _in_proj_kernel(x_ref, g_ref, w_ref, ws_ref, c128_ref, s128_ref, c64_ref, s64l_ref, s64h_ref,
                    zb_ref, zs_ref, vt_ref, nrm_ref, h_scr, *, tn, q_scale):
    j = pl.program_id(1)

    @pl.when(j == 0)
    def _():
        nrm_ref[...] = jnp.zeros_like(nrm_ref)
        xf = x_ref[...]
        y = xf * lax.rsqrt(jnp.mean(xf * xf, axis=-1, keepdims=True) + EPS)
        h = (y * g_ref[...]).astype(BF16)
        h_scr[...] = h
        zs = jnp.dot(h, ws_ref[...], preferred_element_type=F32)
        zs_ref[:, :LANES] = zs[:, :LANES]
        zs_ref[:, LANES:] = _rope64(zs[:, LANES:], c64_ref[...], s64l_ref[...], s64h_ref[...])

    acc = jnp.dot(h_scr[...], w_ref[...], preferred_element_type=F32)
    col = j * tn
    is_q = ((col >= COL_FQ) & (col < COL_FK)) | ((col >= COL_DQ) & (col < COL_DK))
    acc = acc * jnp.where(is_q, q_scale, 1.0).astype(F32)
    is_r128 = (col >= COL_DQ) & (col < COL_DV)
    is_r64 = (col >= COL_IQ) & (col < COL_CQ)

    @pl.when(is_r128)
    def _():
        for g in range(tn // LANES):
            sl = slice(g * LANES, (g + 1) * LANES)
            zb_ref[:, sl] = _rope128(acc[:, sl], c128_ref[...], s128_ref[...]).astype(BF16)

    @pl.when(is_r64)
    def _():
        for g in range(tn // LANES):
            sl = slice(g * LANES, (g + 1) * LANES)
            zb_ref[:, sl] = _rope64(acc[:, sl], c64_ref[...], s64l_ref[...], s64h_ref[...]).astype(BF16)

    @pl.when(jnp.logical_not(is_r128 | is_r64))
    def _():
        zb_ref[...] = acc.astype(BF16)

    is_v = ((col >= COL_FV) & (col < COL_DQ)) | ((col >= COL_DV) & (col < COL_IQ))

    @pl.when(is_v)
    def _():
        vt_ref[...] = acc.T.astype(BF16)

    is_fq = (col >= COL_FQ) & (col < COL_FK)
    is_fk = (col >= COL_FK) & (col < COL_FV)

    @pl.when(is_fq | is_fk)
    def _():
        lane = lax.broadcasted_iota(jnp.int32, (1, LANES), 1)
        first = (col % BRANCH_WIDTH) // HEAD_DIM + jnp.where(is_fq, N_HEADS, 0)
        nrm = nrm_ref[...]
        for g in range(tn // HEAD_DIM):
            a = acc[:, g * HEAD_DIM:(g + 1) * HEAD_DIM]
            nrm = jnp.where(lane == first + g, jnp.sum(a * a, axis=1, keepdims=True), nrm)
        nrm_ref[...] = nrm


def _vt_row_block(j, tn):
    per = BRANCH_WIDTH // tn
    fv0, dv0 = COL_FV // tn, COL_DV // tn
    in_fox = jnp.clip(j - fv0, 0, per - 1)
    in_dsa = per + jnp.clip(j - dv0, 0, per - 1)
    return jnp.where(j < dv0, in_fox, in_dsa)


def _in_proj(x2, g, w_big, w_small, tabs, tm, tn):
    m, d = x2.shape
    n = w_big.shape[1]
    c128, s128, c64, s64l, s64h = tabs
    s = c128.shape[0]
    nsb = s // tm
    tab_spec = pl.BlockSpec((tm, LANES), lambda i, j: (i % nsb, 0))
    return pl.pallas_call(
        functools.partial(_in_proj_kernel, tn=tn, q_scale=HEAD_DIM ** -0.5 * LOG2E),
        grid=(m // tm, n // tn),
        in_specs=[
            pl.BlockSpec((tm, d), lambda i, j: (i, 0)),
            pl.BlockSpec((1, d), lambda i, j: (0, 0)),
            pl.BlockSpec((d, tn), lambda i, j: (0, j)),
            pl.BlockSpec((d, 2 * LANES), lambda i, j: (0, 0)),
            tab_spec, tab_spec, tab_spec, tab_spec, tab_spec,
        ],
        out_specs=[
            pl.BlockSpec((tm, tn), lambda i, j: (i, j)),
            pl.BlockSpec((tm, 2 * LANES), lambda i, j: (i, 0)),
            pl.BlockSpec((None, tn, tm), lambda i, j: (i // nsb, _vt_row_block(j, tn), i % nsb)),
            pl.BlockSpec((tm, LANES), lambda i, j: (i, 0)),
        ],
        out_shape=[
            jax.ShapeDtypeStruct((m, n), BF16),
            jax.ShapeDtypeStruct((m, 2 * LANES), F32),
            jax.ShapeDtypeStruct((m // s, 2 * BRANCH_WIDTH, s), BF16),
            jax.ShapeDtypeStruct((m, LANES), F32),
        ],
        scratch_shapes=[pltpu.VMEM((tm, d), BF16)],
        compiler_params=_params("parallel", "arbitrary"),
        name="in_proj",
    )(x2, g, w_big, w_small, c128, s128, c64, s64l, s64h)


def _bf16_floor(x):
    return pltpu.bitcast(pltpu.bitcast(x, jnp.int32) & jnp.int32(-65536), F32)


def _cum_kernel(zs_ref, fb_ref, kb_ref, carry_scr, *, t):
    @pl.when(pl.program_id(1) == 0)
    def _():
        carry_scr[...] = jnp.zeros_like(carry_scr)

    lane = lax.broadcasted_iota(jnp.int32, (1, LANES), 1)
    x = zs_ref[:, :LANES] + fb_ref[...]
    logf = -(jnp.maximum(-x, 0.0) + jnp.log1p(jnp.exp(-jnp.abs(x))))
    logf = jnp.where(lane < FOX_BIAS_COLS, logf, 0.0)
    r = lax.broadcasted_iota(jnp.int32, (t, t), 0)
    c = lax.broadcasted_iota(jnp.int32, (t, t), 1)
    tri = jnp.where(c <= r, 1.0, 0.0).astype(F32)
    cum = jnp.dot(tri, logf, preferred_element_type=F32, precision=lax.Precision.HIGHEST) + carry_scr[...]
    carry_scr[...] = cum[t - 1:t, :]
    c = cum * (-LOG2E)
    hi = _bf16_floor(c)
    mid = _bf16_floor(c - hi)
    lo = (c - hi) - mid
    kb_ref[...] = jnp.where(lane < N_HEADS, hi, jnp.where(lane < 2 * N_HEADS, mid, lo)).astype(BF16)


def _forget_cumsum(zs, fb, b, s, t):
    fb128 = jnp.pad(jnp.tile(fb, (1, 3)), ((0, 0), (0, LANES - FOX_BIAS_COLS)))
    return pl.pallas_call(
        functools.partial(_cum_kernel, t=t),
        grid=(b, s // t),
        in_specs=[
            pl.BlockSpec((t, 2 * LANES), lambda bi, i: (bi * (s // t) + i, 0)),
            pl.BlockSpec((1, LANES), lambda bi, i: (0, 0)),
        ],
        out_specs=pl.BlockSpec((t, LANES), lambda bi, i: (bi * (s // t) + i, 0)),
        out_shape=jax.ShapeDtypeStruct((b * s, LANES), BF16),
        scratch_shapes=[pltpu.VMEM((1, LANES), F32)],
        compiler_params=_params("parallel", "arbitrary"),
        name="forget_cumsum",
    )(zs, fb128)


def _flash_kernel(qt_ref, kt_ref, q_ref, k_ref, v_ref, *rest, extra, dk, t):
    qn_scr = None
    if extra is None:
        o_ref, m_scr, l_scr, acc_scr = rest
        extra_ref = None
    elif extra == "mask":
        extra_ref, o_ref, m_scr, l_scr, acc_scr = rest
    else:
        extra_ref, qnorm_ref, knorm_ref, o_ref, m_scr, l_scr, acc_scr, qn_scr = rest
    masked = extra == "mask"
    p = pl.program_id(1)
    qi = qt_ref[p]
    ki = kt_ref[p]
    qs = min(t, Q_STRIP)

    @pl.when(ki == qi)
    def _():
        m_scr[...] = jnp.full_like(m_scr, NEG_BIG)
        l_scr[...] = jnp.zeros_like(l_scr)
        acc_scr[...] = jnp.zeros_like(acc_scr)
        if qn_scr is not None:
            qn_scr[...] = pltpu.roll(jnp.max(qnorm_ref[...], axis=0, keepdims=True), LANES - N_HEADS, 1)

    def head_margins():
        kn = jnp.max(knorm_ref[...], axis=0, keepdims=True)
        dot_bound = jnp.sqrt(qn_scr[...] * kn) * NORM_SLACK
        bm = jnp.max(extra_ref[...].astype(F32), axis=0, keepdims=True)
        bias = bm + pltpu.roll(bm, LANES - N_HEADS, 1) + pltpu.roll(bm, LANES - 2 * N_HEADS, 1)
        bound = dot_bound + bias
        m_min = jnp.min(m_scr[...], axis=1, keepdims=True)
        head = lax.broadcasted_iota(jnp.int32, (N_HEADS, LANES), 0)
        lane = lax.broadcasted_iota(jnp.int32, (N_HEADS, LANES), 1)
        margin = jnp.where(head == lane, m_min - bound, jnp.inf)
        return jnp.min(margin, axis=1, keepdims=True)

    all_units = [(c, h) for c in range(t // qs) for h in range(N_HEADS)]

    def step(diag, units):

        def n_keys(c):
            return (c + 1) * qs if diag else t

        def logits(c, h):
            k_h = k_ref[:n_keys(c), h * dk:(h + 1) * dk]
            q_h = q_ref[c * qs:(c + 1) * qs, h * dk:(h + 1) * dk]
            if extra == "key_bias":
                lane = lax.broadcasted_iota(jnp.int32, (qs, LANES), 1)
                pick = jnp.where(lane < FOX_BIAS_COLS, jnp.where(lane % N_HEADS == h, 1.0, 0.0), 0.0)
                k_h = jnp.concatenate([k_h, extra_ref[:n_keys(c), :]], axis=1)
                q_h = jnp.concatenate([q_h, pick.astype(q_h.dtype)], axis=1)
            return lax.dot_general(k_h, q_h, (((1,), (1,)), ((), ())), preferred_element_type=F32)

        def keep_mask(c):
            nk = n_keys(c)
            if masked:
                return extra_ref[:, c * qs:(c + 1) * qs].astype(jnp.int32) != 0
            if diag:
                kr = lax.broadcasted_iota(jnp.int32, (nk, qs), 0)
                qc = lax.broadcasted_iota(jnp.int32, (nk, qs), 1) + c * qs
                return kr <= qc
            return None

        keeps = {}

        def softmax(c, h, st):
            cs = slice(c * qs, (c + 1) * qs)
            if c not in keeps:
                keeps[c] = keep_mask(c)
            if keeps[c] is not None:
                st = jnp.where(keeps[c], st, NEG_BIG)
            m_prev = m_scr[h:h + 1, cs]
            m_new = jnp.maximum(m_prev, jnp.max(st, axis=0, keepdims=True))
            m_scr[h:h + 1, cs] = m_new
            alpha = jnp.exp2(m_prev - m_new)
            return alpha, jnp.exp2(st - m_new).astype(BF16)

        def accumulate(c, h, alpha, pt):
            cs = slice(c * qs, (c + 1) * qs)
            nk = n_keys(c)
            ones = jnp.ones((ONES_ROWS, nk), v_ref.dtype)
            vt_h = jnp.concatenate([v_ref[h * HEAD_DIM:(h + 1) * HEAD_DIM, :nk], ones], axis=0)
            pv = jnp.dot(vt_h, pt, preferred_element_type=F32)
            sl = slice(h * HEAD_DIM, (h + 1) * HEAD_DIM)
            acc_scr[sl, cs] = alpha * acc_scr[sl, cs] + pv[:HEAD_DIM]
            l_scr[h:h + 1, cs] = alpha * l_scr[h:h + 1, cs] + pv[HEAD_DIM:HEAD_DIM + 1]

        n_u = len(units)
        st_q = [logits(*units[i]) for i in range(min(PIPE_LEAD, n_u))]
        p_q = []
        for u in range(n_u + PIPE_LAG):
            if u + PIPE_LEAD < n_u:
                st_q.append(logits(*units[u + PIPE_LEAD]))
            if u < n_u:
                p_q.append(softmax(*units[u], st_q.pop(0)))
            if u >= PIPE_LAG:
                accumulate(*units[u - PIPE_LAG], *p_q.pop(0))

    if masked:
        step(False, all_units)
    else:
        @pl.when(ki == qi)
        def _():
            step(True, all_units)

        if qn_scr is None:
            @pl.when(ki != qi)
            def _():
                step(False, all_units)
        else:
            margins = head_margins()
            off_diag = ki != qi
            every_head = jnp.logical_and(off_diag, jnp.max(margins) <= UNDERFLOW_LOG2)

            @pl.when(every_head)
            def _():
                step(False, all_units)

            some_heads = jnp.logical_and(off_diag, jnp.logical_not(every_head))
            for h in range(N_HEADS):
                @pl.when(jnp.logical_and(some_heads, margins[h, 0] <= UNDERFLOW_LOG2))
                def _(h=h):
                    step(False, [(c, h) for c in range(t // qs)])

    @pl.when(ki == 0)
    def _():
        for h in range(N_HEADS):
            sl = slice(h * HEAD_DIM, (h + 1) * HEAD_DIM)
            o = acc_scr[sl, :] / l_scr[h:h + 1, :]
            o_ref[:, sl] = o.T.astype(BF16)


def _pair_tables(n):
    qt = np.concatenate([np.full(i + 1, i, np.int32) for i in range(n)])
    kt = np.concatenate([np.arange(i, -1, -1, dtype=np.int32) for i in range(n)])
    return jnp.asarray(qt), jnp.asarray(kt)


def _flash(q_arr, q_col, k_arr, k_col, vt_arr, vt_blk, b, s, t, dk, name, mask=None, key_bias=None, norms=None):
    n = s // t
    qt, kt = _pair_tables(n)
    wq = N_HEADS * dk
    wv = N_HEADS * HEAD_DIM
    in_specs = [
        pl.BlockSpec((t, wq), lambda bi, p, qt, kt: (bi * n + qt[p], q_col)),
        pl.BlockSpec((t, wq), lambda bi, p, qt, kt: (bi * n + kt[p], k_col)),
        pl.BlockSpec((None, wv, t), lambda bi, p, qt, kt: (bi, vt_blk, kt[p])),
    ]
    args = [q_arr, k_arr, vt_arr]
    extra = None
    if mask is not None:
        extra = "mask"
        in_specs.append(pl.BlockSpec((None, t, t), lambda bi, p, qt, kt: (bi, kt[p], qt[p])))
        args.append(mask)
    elif key_bias is not None:
        extra = "key_bias"
        in_specs.append(pl.BlockSpec((t, LANES), lambda bi, p, qt, kt: (bi * n + kt[p], 0)))
        in_specs.append(pl.BlockSpec((t, LANES), lambda bi, p, qt, kt: (bi * n + qt[p], 0)))
        in_specs.append(pl.BlockSpec((t, LANES), lambda bi, p, qt, kt: (bi * n + kt[p], 0)))
        args += [key_bias, norms, norms]
    scratch = [pltpu.VMEM((N_HEADS, t), F32), pltpu.VMEM((N_HEADS, t), F32), pltpu.VMEM((wv, t), F32)]
    if extra == "key_bias":
        scratch.append(pltpu.VMEM((1, LANES), F32))
    grid_spec = pltpu.PrefetchScalarGridSpec(
        num_scalar_prefetch=2,
        grid=(b, n * (n + 1) // 2),
        in_specs=in_specs,
        out_specs=pl.BlockSpec((t, wv), lambda bi, p, qt, kt: (bi * n + qt[p], 0)),
        scratch_shapes=scratch,
    )
    return pl.pallas_call(
        functools.partial(_flash_kernel, extra=extra, dk=dk, t=t),
        grid_spec=grid_spec,
        out_shape=jax.ShapeDtypeStruct((b * s, wv), BF16),
        compiler_params=_params("parallel", "arbitrary"),
        name=name,
    )(qt, kt, *args)


def _count(sc_ref, nvalid, cand, tk, tq, ind_fn):
    group = _count_group(sc_ref.shape[0])

    lanes = COUNT_ACCS * 8

    def body(g, a):
        for e in range(group):
            c = g * group + e
            ind = ind_fn(sc_ref[c], cand, c)
            a = a + jnp.sum(ind.reshape(tk // lanes, lanes, tq), axis=0)
        return a

    part = lax.fori_loop(0, (nvalid + group - 1) // group, body, jnp.zeros((lanes, tq), jnp.int32))
    return jnp.sum(part, axis=0, keepdims=True)


def _count_group(nchunks):
    return 2 if nchunks % 2 == 0 else 1


def _indexer_kernel(q_ref, wt_ref, k2_ref, mask_ref, sc_ref, qm_scr, j_scr, *, tq, tk, n_sel, idx_bits):
    qi = pl.program_id(1)
    one, zero = jnp.int32(1), jnp.int32(0)
    nchunks = sc_ref.shape[0]
    nvalid = ((qi + 1) * tq + tk - 1) // tk
    qcol = qi * tq + lax.broadcasted_iota(jnp.int32, (1, tq), 1)
    krow0 = lax.broadcasted_iota(jnp.int32, (tk, 1), 0)

    half = lax.broadcasted_iota(jnp.int32, (tq, LANES), 1) // IDX_DIM
    for g in range(IDX_HEADS // 2):
        qp = q_ref[:, g * LANES:(g + 1) * LANES]
        for e in range(2):
            qm_scr[2 * g + e] = jnp.where(half == e, qp, jnp.zeros_like(qp))

    def score_chunk(c, carry):
        kmax, n_pos = carry
        kc = k2_ref[pl.ds(pl.multiple_of(c * tk, tk), tk), :]
        acc = jnp.zeros((tk, tq), F32)
        for h in range(IDX_HEADS):
            x = lax.dot_general(kc, qm_scr[h], (((1,), (1,)), ((), ())), preferred_element_type=F32)
            acc = acc + wt_ref[h:h + 1, :] * jnp.maximum(x, 0.0)
        score = acc * (IDX_HEADS * IDX_DIM) ** -0.5
        bits = pltpu.bitcast(score, jnp.int32)
        key = bits ^ ((bits >> 31) & 0x7FFFFFFF)
        causal = (c * tk + krow0) <= qcol
        key = jnp.where(causal, key, INT_MIN)
        sc_ref[c] = key
        nonneg = jnp.where(key >= 0, one, zero)
        return (jnp.maximum(kmax, jnp.max(key.reshape(tk // 8, 8, tq), axis=0)),
                n_pos + jnp.sum(nonneg.reshape(tk // 8, 8, tq), axis=0))

    kmax, n_pos = lax.fori_loop(0, nvalid, score_chunk,
                                (jnp.full((8, tq), INT_MIN, jnp.int32), jnp.zeros((8, tq), jnp.int32)))
    kmax = jnp.max(kmax, axis=0, keepdims=True)
    f0 = jnp.sum(n_pos, axis=0, keepdims=True)

    group = _count_group(nchunks)
    if group > 1:
        def pad_chunk(c, carry):
            sc_ref[c] = jnp.full((tk, tq), INT_MIN, jnp.int32)
            return carry

        lax.fori_loop(nvalid, (nvalid + group - 1) // group * group, pad_chunk, 0)

    def count_ge(cand):
        return _count(sc_ref, nvalid, cand, tk, tq, lambda k, cd, c: jnp.where(k >= cd, one, zero))

    n_keys = qcol + 1
    pos = f0 >= n_sel
    lo0 = jnp.where(pos, 0, INT_MIN + 1)
    hi0 = jnp.where(pos, kmax, jnp.minimum(kmax, -1)) + 1
    flo0 = jnp.where(pos, f0, n_keys)

    def searching(flo, w):
        return jnp.where(flo > n_sel, jnp.where(w > 1, 1.0, 0.0), 0.0)

    def bisect_cond(carry):
        it, lo, w, flo = carry
        return (jnp.max(searching(flo, w)) > 0.0) & (it < 32)

    def bisect_body(carry):
        it, lo, w, flo = carry
        active = searching(flo, w) > 0
        off = w >> 1
        near_top = w - FIRST_PROBE_BELOW_MAX
        off = jnp.where(it == 0, jnp.where(near_top >= 1, near_top, off), off)
        cand = lo + off
        cnt = count_ge(cand)
        ok = cnt >= n_sel
        lo = jnp.where(active, jnp.where(ok, cand, lo), lo)
        flo = jnp.where(active, jnp.where(ok, cnt, flo), flo)
        w = jnp.where(active, jnp.where(ok, w - off, off), w)
        return it + 1, lo, w, flo

    _, lo, _, flo = lax.while_loop(bisect_cond, bisect_body, (jnp.int32(0), lo0, hi0 - lo0, flo0))
    found = n_keys >= n_sel
    tie = jnp.where(found, jnp.where(flo > n_sel, one, zero), zero) > 0
    gt_thr = jnp.where(found, jnp.where(tie, lo, lo - 1), INT_MIN)
    thr = jnp.where(tie, lo, INT_MIN + 1)

    j_scr[...] = jnp.full_like(j_scr, 2 ** idx_bits - 1)
    has_tie = jnp.max(jnp.where(tie, 1.0, 0.0)) > 0.0

    @pl.when(has_tie)
    def _():
        cgt = _count(sc_ref, nvalid, thr, tk, tq, lambda k, cd, c: jnp.where(k > cd, one, zero))
        need = n_sel - cgt

        def jbit(i, jv):
            cand = jv | lax.shift_left(jnp.int32(1), idx_bits - 1 - i)
            cnt = _count(sc_ref, nvalid, cand, tk, tq,
                         lambda k, cd, c: jnp.where(k == thr, jnp.where((c * tk + krow0) < cd, one, zero), zero))
            return jnp.where(cnt < need, cand, jv)

        j_scr[...] = lax.fori_loop(0, idx_bits, jbit, jnp.zeros((1, tq), jnp.int32))

    jlast = j_scr[...]

    def write_chunk(c, carry):
        key = sc_ref[c]
        sel = jnp.where(key > gt_thr, one,
                        jnp.where(key == thr, jnp.where((c * tk + krow0) <= jlast, one, zero), zero))
        mask_ref[pl.ds(pl.multiple_of(c * tk, tk), tk), :] = sel.astype(jnp.int8)
        return carry

    lax.fori_loop(0, nvalid, write_chunk, 0)

    def zero_chunk(c, carry):
        mask_ref[pl.ds(pl.multiple_of(c * tk, tk), tk), :] = jnp.zeros((tk, tq), jnp.int8)
        return carry

    lax.fori_loop(nvalid, nchunks, zero_chunk, 0)


def _indexer_mask(zb, wt, k2, b, s, tq, tk, n_sel):
    nq = s // tq
    idx_bits = max(1, int(np.ceil(np.log2(s))))
    return pl.pallas_call(
        functools.partial(_indexer_kernel, tq=tq, tk=tk, n_sel=n_sel, idx_bits=idx_bits),
        grid=(b, nq),
        in_specs=[
            pl.BlockSpec((tq, IDX_HEADS * IDX_DIM), lambda bi, i: (bi * nq + i, COL_IQ // (IDX_HEADS * IDX_DIM))),
            pl.BlockSpec((None, IDX_HEADS, tq), lambda bi, i: (bi, 0, i)),
            pl.BlockSpec((None, s, LANES), lambda bi, i: (bi, 0, 0)),
        ],
        out_specs=pl.BlockSpec((None, s, tq), lambda bi, i: (bi, 0, i)),
        out_shape=jax.ShapeDtypeStruct((b, s, s), jnp.int8),
        scratch_shapes=[
            pltpu.VMEM((s // tk, tk, tq), jnp.int32),
            pltpu.VMEM((IDX_HEADS, tq, LANES), BF16),
            pltpu.VMEM((1, tq), jnp.int32),
        ],
        compiler_params=_params("parallel", "arbitrary"),
        name="indexer_mask",
    )(zb, wt, k2)


def _mla_up_kernel(c_ref, zs_ref, qg_ref, kvg_ref, wq_ref, wk_ref, wv_ref, c64_ref, s64l_ref, s64h_ref,
                   q_ref, k_ref, v_ref, *, q_scale):
    def norm(xf, g):
        y = xf * lax.rsqrt(jnp.mean(xf * xf, axis=-1, keepdims=True) + EPS)
        return (y * g).astype(BF16)

    cq = norm(c_ref[:, :Q_LORA].astype(F32), qg_ref[...])
    ckv = norm(c_ref[:, Q_LORA:].astype(F32), kvg_ref[...])
    q = jnp.dot(cq, wq_ref[...], preferred_element_type=F32) * q_scale
    k = jnp.dot(ckv, wk_ref[...], preferred_element_type=F32)
    v_ref[...] = jnp.dot(ckv, wv_ref[...], preferred_element_type=F32).T.astype(BF16)
    lane = lax.broadcasted_iota(jnp.int32, (1, LANES), 1)
    k_rot = jnp.where(lane < QK_ROPE, zs_ref[:, LANES:], 0.0)
    cos, sin_lo, sin_hi = c64_ref[...], s64l_ref[...], s64h_ref[...]
    for h in range(N_HEADS):
        nope = slice(h * MLA_QK_PAD, h * MLA_QK_PAD + QK_NOPE)
        rope = slice(h * MLA_QK_PAD + QK_NOPE, (h + 1) * MLA_QK_PAD)
        q_ref[:, nope] = q[:, nope].astype(BF16)
        q_ref[:, rope] = _rope64(q[:, rope], cos, sin_lo, sin_hi).astype(BF16)
        k_ref[:, nope] = k[:, nope].astype(BF16)
        k_ref[:, rope] = k_rot.astype(BF16)


def _mla_up(zb, zs, qg, kvg, wq, wk, wv, tabs, tm):
    m = zb.shape[0]
    _, _, c64, s64l, s64h = tabs
    nsb = c64.shape[0] // tm
    tab_spec = pl.BlockSpec((tm, LANES), lambda i: (i % nsb, 0))
    wqk = N_HEADS * MLA_QK_PAD
    full = lambda a: pl.BlockSpec(a.shape, lambda i: (0,) * a.ndim)
    return pl.pallas_call(
        functools.partial(_mla_up_kernel, q_scale=(QK_NOPE + QK_ROPE) ** -0.5 * LOG2E),
        grid=(m // tm,),
        in_specs=[
            pl.BlockSpec((tm, Q_LORA + KV_LORA), lambda i: (i, COL_CQ // (Q_LORA + KV_LORA))),
            pl.BlockSpec((tm, 2 * LANES), lambda i: (i, 0)),
            full(qg), full(kvg), full(wq), full(wk), full(wv),
            tab_spec, tab_spec, tab_spec,
        ],
        out_specs=[
            pl.BlockSpec((tm, wqk), lambda i: (i, 0)),
            pl.BlockSpec((tm, wqk), lambda i: (i, 0)),
            pl.BlockSpec((None, BRANCH_WIDTH, tm), lambda i: (i // nsb, 0, i % nsb)),
        ],
        out_shape=[
            jax.ShapeDtypeStruct((m, wqk), BF16),
            jax.ShapeDtypeStruct((m, wqk), BF16),
            jax.ShapeDtypeStruct((m // (nsb * tm), BRANCH_WIDTH, nsb * tm), BF16),
        ],
        compiler_params=_params("parallel"),
        name="mla_up",
    )(zb, zs, qg, kvg, wq, wk, wv, c64, s64l, s64h)


def _branch_kernel(o0_ref, o1_ref, o2_ref, g0_ref, g1_ref, g2_ref, m0_ref, m1_ref, m2_ref, w_ref, y_ref):
    y = None
    for i, (o_ref, g_ref, m_ref) in enumerate(((o0_ref, g0_ref, m0_ref), (o1_ref, g1_ref, m1_ref),
                                               (o2_ref, g2_ref, m2_ref))):
        g = g_ref[...].astype(F32)
        a = (o_ref[...].astype(F32) * (g * jax.nn.sigmoid(g))).astype(BF16)
        term = jax.nn.sigmoid(m_ref[...].astype(F32)) * jnp.dot(a, w_ref[i], preferred_element_type=F32)
        y = term if y is None else y + term
    y_ref[...] = y.astype(BF16)


def _branch_merge(o_fox, o_dsa, o_mla, zb, wb, d, tm, tn):
    m = zb.shape[0]
    o_spec = pl.BlockSpec((tm, BRANCH_WIDTH), lambda i, j: (i, 0))
    silu0 = (COL_MERGE + N_BRANCH * d) // BRANCH_WIDTH

    def g_spec(i_br):
        return pl.BlockSpec((tm, BRANCH_WIDTH), lambda i, j: (i, silu0 + i_br))

    def m_spec(i_br):
        base = (COL_MERGE + i_br * d) // tn
        return pl.BlockSpec((tm, tn), lambda i, j: (i, base + j))

    return pl.pallas_call(
        _branch_kernel,
        grid=(m // tm, d // tn),
        in_specs=[o_spec, o_spec, o_spec, g_spec(0), g_spec(1), g_spec(2), m_spec(0), m_spec(1), m_spec(2),
                  pl.BlockSpec((N_BRANCH, BRANCH_WIDTH, tn), lambda i, j: (0, 0, j))],
        out_specs=pl.BlockSpec((tm, tn), lambda i, j: (i, j)),
        out_shape=jax.ShapeDtypeStruct((m, d), BF16),
        compiler_params=_params("parallel", "arbitrary"),
        name="branch_merge",
    )(o_fox, o_dsa, o_mla, zb, zb, zb, zb, zb, zb, wb)


def _out_kernel(x_ref, y_ref, w_ref, g_ref, o_ref, *, final_norm):
    r = x_ref[...] + jnp.dot(y_ref[...], w_ref[...], preferred_element_type=F32)
    if final_norm:
        r = r * lax.rsqrt(jnp.mean(r * r, axis=-1, keepdims=True) + EPS) * g_ref[...]
    o_ref[...] = r


def _out_proj(x2, y, w, g, tm, final_norm):
    m, d = x2.shape
    return pl.pallas_call(
        functools.partial(_out_kernel, final_norm=final_norm),
        grid=(m // tm,),
        in_specs=[
            pl.BlockSpec((tm, d), lambda i: (i, 0)),
            pl.BlockSpec((tm, d), lambda i: (i, 0)),
            pl.BlockSpec((d, d), lambda i: (0, 0)),
            pl.BlockSpec((1, d), lambda i: (0, 0)),
        ],
        out_specs=pl.BlockSpec((tm, d), lambda i: (i, 0)),
        out_shape=jax.ShapeDtypeStruct((m, d), F32),
        compiler_params=_params("parallel"),
        name="out_proj",
    )(x2, y, w, g)


def _split_w_in(w_in, d):
    sizes = (BRANCH_WIDTH, BRANCH_WIDTH, BRANCH_WIDTH, N_HEADS, BRANCH_WIDTH, BRANCH_WIDTH, BRANCH_WIDTH,
             IDX_HEADS * IDX_DIM, IDX_DIM, IDX_HEADS, Q_LORA, KV_LORA, QK_ROPE,
             N_BRANCH * BRANCH_WIDTH, N_BRANCH * d)
    pts = np.cumsum(sizes)[:-1].tolist()
    (f_q, f_k, f_v, f_gate, d_q, d_k, d_v, i_q, i_k, i_w, c_q, c_kv, k_pe, silu_g, merge_g) = jnp.split(
        w_in, pts, axis=-1)
    big = jnp.concatenate([f_q, f_k, f_v, d_q, d_k, d_v, i_q, c_q, c_kv, merge_g, silu_g], axis=-1).astype(BF16)
    pad = jnp.zeros((w_in.shape[0], LANES - FOX_BIAS_COLS - IDX_HEADS), w_in.dtype)
    small = jnp.concatenate([f_gate, f_gate, f_gate, i_w, pad, k_pe, i_k], axis=-1).astype(BF16)
    return big, small


def _pad_heads(w, take, width_in):
    r = w.shape[0]
    w3 = w.reshape(r, N_HEADS, width_in)[:, :, take]
    w3 = jnp.pad(w3, ((0, 0), (0, 0), (0, MLA_QK_PAD - w3.shape[-1])))
    return w3.reshape(r, N_HEADS * MLA_QK_PAD)


def _tile(s, pref):
    return pref if s % pref == 0 else s


def kernel(x, norm_g, w_in, forget_b, q_norm_g, w_q_up, kv_norm_g, w_kv_up, w_branch, w_out, final_norm_g):
    b, s, d = x.shape
    depth = w_in.shape[0]
    n_sel = min(TOPK_MAX, s // 4)
    m = b * s
    tabs = _rope_tables(s)
    t_att = _tile(s, 512)
    tm_in = _tile(s, 1024)
    x2 = x.reshape(m, d)
    for l in range(depth):
        w_big, w_small = _split_w_in(w_in[l], d)
        zb, zs, vt, norms = _in_proj(x2, norm_g[l].reshape(1, d), w_big, w_small, tabs, tm_in, 512)

        key_bias = _forget_cumsum(zs, forget_b[l].reshape(1, N_HEADS), b, s, _tile(s, 256))
        o_fox = _flash(zb, COL_FQ // BRANCH_WIDTH, zb, COL_FK // BRANCH_WIDTH, vt, 0,
                       b, s, _tile(s, 1024), HEAD_DIM, "flash_fox", key_bias=key_bias, norms=norms)

        zs3 = zs.reshape(b, s, 2 * LANES)
        wt = jnp.swapaxes(zs3[:, :, FOX_BIAS_COLS:FOX_BIAS_COLS + IDX_HEADS], 1, 2)
        ik = zs3[:, :, LANES + QK_ROPE:].astype(BF16)
        k2 = jnp.concatenate([ik, ik], axis=-1)
        mask = _indexer_mask(zb, wt, k2, b, s, _tile(s, 256), _tile(s, 512), n_sel)
        o_dsa = _flash(zb, COL_DQ // BRANCH_WIDTH, zb, COL_DK // BRANCH_WIDTH, vt, 1,
                       b, s, t_att, HEAD_DIM, "flash_mask", mask=mask)

        wq = _pad_heads(w_q_up[l], slice(0, QK_NOPE + QK_ROPE), QK_NOPE + QK_ROPE).astype(BF16)
        wk = _pad_heads(w_kv_up[l], slice(0, QK_NOPE), QK_NOPE + V_DIM).astype(BF16)
        wv = w_kv_up[l].reshape(KV_LORA, N_HEADS, QK_NOPE + V_DIM)[:, :, QK_NOPE:].reshape(
            KV_LORA, BRANCH_WIDTH).astype(BF16)
        q_mla, k_mla, vt_mla = _mla_up(zb, zs, q_norm_g[l].reshape(1, Q_LORA), kv_norm_g[l].reshape(1, KV_LORA),
                                       wq, wk, wv, tabs, _tile(s, 512))
        o_mla = _flash(q_mla, 0, k_mla, 0, vt_mla, 0, b, s, t_att, MLA_QK_PAD, "flash_plain")

        y = _branch_merge(o_fox, o_dsa, o_mla, zb, w_branch[l].astype(BF16), d, _tile(s, 512), 1024)
        last = l == depth - 1
        x2 = _out_proj(x2, y, w_out[l].astype(BF16), final_norm_g.reshape(1, d), _tile(s, 512), last)
    return x2.reshape(b, s, d)
```

```python
import functools

import jax
import jax.numpy as jnp
import numpy as np
from jax import lax
from jax.experimental import pallas as pl
from jax.experimental.pallas import tpu as pltpu

F32 = jnp.float32
BF16 = jnp.bfloat16

HEAD_DIM = 128
N_HEADS = 8
BRANCH_WIDTH = N_HEADS * HEAD_DIM
N_BRANCH = 3
IDX_HEADS = 16
IDX_DIM = 64
TOPK_MAX = 256
Q_LORA = 512
KV_LORA = 512
QK_NOPE = 128
QK_ROPE = 64
V_DIM = 128
MLA_QK_PAD = 256
FOX_BIAS_COLS = 3 * N_HEADS
NORM_SLACK = 1.02
UNDERFLOW_LOG2 = 160.0
ONES_ROWS = 16
ROPE_THETA = 10000.0
EPS = 1e-6

LANES = 128
VMEM_LIMIT = 56 * 1024 * 1024
NEG_BIG = -1e30
INT_MIN = -(2 ** 31)
LOG2E = 1.4426950408889634
Q_STRIP = 256
FIRST_PROBE_BELOW_MAX = 2 ** 24
COUNT_ACCS = 4
PIPE_LEAD = 2
PIPE_LAG = 2

COL_FQ, COL_FK, COL_FV = 0, 1024, 2048
COL_DQ, COL_DK, COL_DV = 3072, 4096, 5120
COL_IQ = 6144
COL_CQ = 7168
COL_MERGE = 8192


def _params(*sem):
    return pltpu.CompilerParams(dimension_semantics=sem, vmem_limit_bytes=VMEM_LIMIT)


def _rope_tables(s):
    pos = jnp.arange(s, dtype=F32)

    def cs(half):
        inv = ROPE_THETA ** (-jnp.arange(half, dtype=F32) / half)
        ang = pos[:, None] * inv[None, :]
        return jnp.cos(ang), jnp.sin(ang)

    c64, s64 = cs(64)
    c32, s32 = cs(32)
    z32 = jnp.zeros_like(s32)
    cos128 = jnp.concatenate([c64, c64], axis=-1)
    sin128 = jnp.concatenate([-s64, s64], axis=-1)
    cos64 = jnp.concatenate([c32, c32, c32, c32], axis=-1)
    sin64_lo = jnp.concatenate([-s32, z32, -s32, z32], axis=-1)
    sin64_hi = jnp.concatenate([z32, s32, z32, s32], axis=-1)
    return cos128, sin128, cos64, sin64_lo, sin64_hi


def _rope128(x, cos, sin):
    return x * cos + pltpu.roll(x, 64, 1) * sin


def _rope64(x, cos, sin_lo, sin_hi):
    return x * cos + pltpu.roll(x, 96, 1) * sin_lo + pltpu.roll(x, 32, 1) * sin_hi


def _in_proj_kernel(x_ref, g_ref, w_ref, ws_ref, c128_ref, s128_ref, c64_ref, s64l_ref, s64h_ref,
                    zb_ref, zs_ref, vt_ref, nrm_ref, h_scr, q_scr, *, tn, q_scale):
    j = pl.program_id(1)

    @pl.when(j == 0)
    def _():
        nrm_ref[...] = jnp.zeros_like(nrm_ref)
        xf = x_ref[...]
        y = xf * lax.rsqrt(jnp.mean(xf * xf, axis=-1, keepdims=True) + EPS)
        h = (y * g_ref[...]).astype(BF16)
        h_scr[...] = h
        zs = jnp.dot(h, ws_ref[...], preferred_element_type=F32)
        zs_ref[:, :LANES] = zs[:, :LANES]
        zs_ref[:, LANES:] = _rope64(zs[:, LANES:], c64_ref[...], s64l_ref[...], s64h_ref[...])

    acc = jnp.dot(h_scr[...], w_ref[...], preferred_element_type=F32)
    col = j * tn
    is_q = ((col >= COL_FQ) & (col < COL_FK)) | ((col >= COL_DQ) & (col < COL_DK))
    acc = acc * jnp.where(is_q, q_scale, 1.0).astype(F32)
    is_r128 = (col >= COL_DQ) & (col < COL_DV)
    is_r64 = (col >= COL_IQ) & (col < COL_CQ)

    @pl.when(is_r128)
    def _():
        for g in range(tn // LANES):
            sl = slice(g * LANES, (g + 1) * LANES)
            zb_ref[:, sl] = _rope128(acc[:, sl], c128_ref[...], s128_ref[...]).astype(BF16)

    @pl.when(is_r64)
    def _():
        for g in range(tn // LANES):
            sl = slice(g * LANES, (g + 1) * LANES)
            zb_ref[:, sl] = _rope64(acc[:, sl], c64_ref[...], s64l_ref[...], s64h_ref[...]).astype(BF16)

    @pl.when(jnp.logical_not(is_r128 | is_r64))
    def _():
        zb_ref[...] = acc.astype(BF16)

    is_v = ((col >= COL_FV) & (col < COL_DQ)) | ((col >= COL_DV) & (col < COL_IQ))

    @pl.when(is_v)
    def _():
        vt_ref[...] = acc.T.astype(BF16)

    is_fq = (col >= COL_FQ) & (col < COL_FK)
    is_fk = (col >= COL_FK) & (col < COL_FV)

    @pl.when(is_fq | is_fk)
    def _():
        lane = lax.broadcasted_iota(jnp.int32, (1, LANES), 1)
        first = (col % BRANCH_WIDTH) // HEAD_DIM + jnp.where(is_fq, N_HEADS, 0)
        nrm = nrm_ref[...]
        for g in range(tn // HEAD_DIM):
            a = acc[:, g * HEAD_DIM:(g + 1) * HEAD_DIM]
            nrm = jnp.where(lane == first + g, jnp.sum(a * a, axis=1, keepdims=True), nrm)
        nrm_ref[...] = nrm

    heads_per_tile = tn // HEAD_DIM
    for jj in range(BRANCH_WIDTH // tn):
        @pl.when(j == COL_FQ // tn + jj)
        def _(jj=jj):
            q_scr[:, jj * tn:(jj + 1) * tn] = acc.astype(BF16)

        @pl.when(j == COL_FK // tn + jj)
        def _(jj=jj):
            lane = lax.broadcasted_iota(jnp.int32, (1, LANES), 1)
            nrm = nrm_ref[...]
            for g in range(heads_per_tile):
                q_h = q_scr[:, jj * tn + g * HEAD_DIM:jj * tn + (g + 1) * HEAD_DIM].astype(F32)
                k_h = acc[:, g * HEAD_DIM:(g + 1) * HEAD_DIM]
                own = jnp.sum(q_h * k_h, axis=1, keepdims=True)
                nrm = jnp.where(lane == 2 * N_HEADS + jj * heads_per_tile + g, own, nrm)
            nrm_ref[...] = nrm


def _vt_row_block(j, tn):
    per = BRANCH_WIDTH // tn
    fv0, dv0 = COL_FV // tn, COL_DV // tn
    in_fox = jnp.clip(j - fv0, 0, per - 1)
    in_dsa = per + jnp.clip(j - dv0, 0, per - 1)
    return jnp.where(j < dv0, in_fox, in_dsa)


def _in_proj(x2, g, w_big, w_small, tabs, tm, tn):
    m, d = x2.shape
    n = w_big.shape[1]
    c128, s128, c64, s64l, s64h = tabs
    s = c128.shape[0]
    nsb = s // tm
    tab_spec = pl.BlockSpec((tm, LANES), lambda i, j: (i % nsb, 0))
    return pl.pallas_call(
        functools.partial(_in_proj_kernel, tn=tn, q_scale=HEAD_DIM ** -0.5 * LOG2E),
        grid=(m // tm, n // tn),
        in_specs=[
            pl.BlockSpec((tm, d), lambda i, j: (i, 0)),
            pl.BlockSpec((1, d), lambda i, j: (0, 0)),
            pl.BlockSpec((d, tn), lambda i, j: (0, j)),
            pl.BlockSpec((d, 2 * LANES), lambda i, j: (0, 0)),
            tab_spec, tab_spec, tab_spec, tab_spec, tab_spec,
        ],
        out_specs=[
            pl.BlockSpec((tm, tn), lambda i, j: (i, j)),
            pl.BlockSpec((tm, 2 * LANES), lambda i, j: (i, 0)),
            pl.BlockSpec((None, tn, tm), lambda i, j: (i // nsb, _vt_row_block(j, tn), i % nsb)),
            pl.BlockSpec((tm, LANES), lambda i, j: (i, 0)),
        ],
        out_shape=[
            jax.ShapeDtypeStruct((m, n), BF16),
            jax.ShapeDtypeStruct((m, 2 * LANES), F32),
            jax.ShapeDtypeStruct((m // s, 2 * BRANCH_WIDTH, s), BF16),
            jax.ShapeDtypeStruct((m, LANES), F32),
        ],
        scratch_shapes=[pltpu.VMEM((tm, d), BF16), pltpu.VMEM((tm, BRANCH_WIDTH), BF16)],
        compiler_params=_params("parallel", "arbitrary"),
        name="in_proj",
    )(x2, g, w_big, w_small, c128, s128, c64, s64l, s64h)


def _bf16_floor(x):
    return pltpu.bitcast(pltpu.bitcast(x, jnp.int32) & jnp.int32(-65536), F32)


def _cum_kernel(zs_ref, fb_ref, kb_ref, carry_scr, *, t):
    @pl.when(pl.program_id(1) == 0)
    def _():
        carry_scr[...] = jnp.zeros_like(carry_scr)

    lane = lax.broadcasted_iota(jnp.int32, (1, LANES), 1)
    x = zs_ref[:, :LANES] + fb_ref[...]
    logf = -(jnp.maximum(-x, 0.0) + jnp.log1p(jnp.exp(-jnp.abs(x))))
    logf = jnp.where(lane < FOX_BIAS_COLS, logf, 0.0)
    r = lax.broadcasted_iota(jnp.int32, (t, t), 0)
    c = lax.broadcasted_iota(jnp.int32, (t, t), 1)
    tri = jnp.where(c <= r, 1.0, 0.0).astype(F32)
    cum = jnp.dot(tri, logf, preferred_element_type=F32, precision=lax.Precision.HIGHEST) + carry_scr[...]
    carry_scr[...] = cum[t - 1:t, :]
    c = cum * (-LOG2E)
    hi = _bf16_floor(c)
    mid = _bf16_floor(c - hi)
    lo = (c - hi) - mid
    kb_ref[...] = jnp.where(lane < N_HEADS, hi, jnp.where(lane < 2 * N_HEADS, mid, lo)).astype(BF16)


def _forget_cumsum(zs, fb, b, s, t):
    fb128 = jnp.pad(jnp.tile(fb, (1, 3)), ((0, 0), (0, LANES - FOX_BIAS_COLS)))
    return pl.pallas_call(
        functools.partial(_cum_kernel, t=t),
        grid=(b, s // t),
        in_specs=[
            pl.BlockSpec((t, 2 * LANES), lambda bi, i: (bi * (s // t) + i, 0)),
            pl.BlockSpec((1, LANES), lambda bi, i: (0, 0)),
        ],
        out_specs=pl.BlockSpec((t, LANES), lambda bi, i: (bi * (s // t) + i, 0)),
        out_shape=jax.ShapeDtypeStruct((b * s, LANES), BF16),
        scratch_shapes=[pltpu.VMEM((1, LANES), F32)],
        compiler_params=_params("parallel", "arbitrary"),
        name="forget_cumsum",
    )(zs, fb128)


def _flash_kernel(qt_ref, kt_ref, old_ref, q_ref, k_ref, v_ref, *rest, extra, dk, t, n_tiles):
    qn_scr = None
    if extra is None:
        o_ref, m_scr, l_scr, acc_scr = rest
        extra_ref = None
    elif extra == "mask":
        extra_ref, o_ref, m_scr, l_scr, acc_scr = rest
    else:
        extra_ref, qnorm_ref, knorm_ref, o_ref, m_scr, l_scr, acc_scr, qn_scr = rest
    masked = extra == "mask"
    p = pl.program_id(1)
    qi = qt_ref[p]
    ki = kt_ref[p]
    qs = min(t, Q_STRIP)

    @pl.when(ki == qi)
    def _():
        m_scr[...] = jnp.full_like(m_scr, NEG_BIG)
        l_scr[...] = jnp.zeros_like(l_scr)
        acc_scr[...] = jnp.zeros_like(acc_scr)
        if qn_scr is not None:
            qn_scr[...] = pltpu.roll(jnp.max(qnorm_ref[...], axis=0, keepdims=True), LANES - N_HEADS, 1)

    def head_margins():
        kn = jnp.max(knorm_ref[...], axis=0, keepdims=True)
        dot_bound = jnp.sqrt(qn_scr[...] * kn) * NORM_SLACK
        bm = jnp.max(extra_ref[...].astype(F32), axis=0, keepdims=True)
        bias = bm + pltpu.roll(bm, LANES - N_HEADS, 1) + pltpu.roll(bm, LANES - 2 * N_HEADS, 1)
        bound = dot_bound + bias
        m_min = jnp.min(m_scr[...], axis=1, keepdims=True)
        head = lax.broadcasted_iota(jnp.int32, (N_HEADS, LANES), 0)
        lane = lax.broadcasted_iota(jnp.int32, (N_HEADS, LANES), 1)
        margin = jnp.where(head == lane, m_min - bound, jnp.inf)
        return jnp.min(margin, axis=1, keepdims=True)

    all_units = [(c, h) for c in range(t // qs) for h in range(N_HEADS)]

    def step(diag, units):

        def n_keys(c):
            return (c + 1) * qs if diag else t

        def logits(c, h):
            k_h = k_ref[:n_keys(c), h * dk:(h + 1) * dk]
            q_h = q_ref[c * qs:(c + 1) * qs, h * dk:(h + 1) * dk]
            if extra == "key_bias":
                lane = lax.broadcasted_iota(jnp.int32, (qs, LANES), 1)
                pick = jnp.where(lane < FOX_BIAS_COLS, jnp.where(lane % N_HEADS == h, 1.0, 0.0), 0.0)
                k_h = jnp.concatenate([k_h, extra_ref[:n_keys(c), :]], axis=1)
                q_h = jnp.concatenate([q_h, pick.astype(q_h.dtype)], axis=1)
            return lax.dot_general(k_h, q_h, (((1,), (1,)), ((), ())), preferred_element_type=F32)

        def keep_mask(c):
            nk = n_keys(c)
            if masked:
                return extra_ref[:, c * qs:(c + 1) * qs].astype(jnp.int32) != 0
            if diag:
                kr = lax.broadcasted_iota(jnp.int32, (nk, qs), 0)
                qc = lax.broadcasted_iota(jnp.int32, (nk, qs), 1) + c * qs
                return kr <= qc
            return None

        keeps = {}

        def softmax(c, h, st):
            cs = slice(c * qs, (c + 1) * qs)
            if c not in keeps:
                keeps[c] = keep_mask(c)
            if keeps[c] is not None:
                st = jnp.where(keeps[c], st, NEG_BIG)
            m_prev = m_scr[h:h + 1, cs]
            m_new = jnp.maximum(m_prev, jnp.max(st, axis=0, keepdims=True))
            m_scr[h:h + 1, cs] = m_new
            alpha = jnp.exp2(m_prev - m_new)
            return alpha, jnp.exp2(st - m_new).astype(BF16)

        def accumulate(c, h, alpha, pt):
            cs = slice(c * qs, (c + 1) * qs)
            nk = n_keys(c)
            ones = jnp.ones((ONES_ROWS, nk), v_ref.dtype)
            vt_h = jnp.concatenate([v_ref[h * HEAD_DIM:(h + 1) * HEAD_DIM, :nk], ones], axis=0)
            pv = jnp.dot(vt_h, pt, preferred_element_type=F32)
            sl = slice(h * HEAD_DIM, (h + 1) * HEAD_DIM)
            acc_scr[sl, cs] = alpha * acc_scr[sl, cs] + pv[:HEAD_DIM]
            l_scr[h:h + 1, cs] = alpha * l_scr[h:h + 1, cs] + pv[HEAD_DIM:HEAD_DIM + 1]

        n_u = len(units)
        st_q = [logits(*units[i]) for i in range(min(PIPE_LEAD, n_u))]
        p_q = []
        for u in range(n_u + PIPE_LAG):
            if u + PIPE_LEAD < n_u:
                st_q.append(logits(*units[u + PIPE_LEAD]))
            if u < n_u:
                p_q.append(softmax(*units[u], st_q.pop(0)))
            if u >= PIPE_LAG:
                accumulate(*units[u - PIPE_LAG], *p_q.pop(0))

    if masked:
        step(False, all_units)
    else:
        @pl.when(ki == qi)
        def _():
            step(True, all_units)

        if qn_scr is None:
            @pl.when(ki != qi)
            def _():
                step(False, all_units)
        else:
            margins = head_margins()
            fetched = ki >= old_ref[pl.program_id(0) * n_tiles + qi]
            off_diag = jnp.logical_and(ki != qi, fetched)
            every_head = jnp.logical_and(off_diag, jnp.max(margins) <= UNDERFLOW_LOG2)

            @pl.when(every_head)
            def _():
                step(False, all_units)

            some_heads = jnp.logical_and(off_diag, jnp.logical_not(every_head))
            for h in range(N_HEADS):
                @pl.when(jnp.logical_and(some_heads, margins[h, 0] <= UNDERFLOW_LOG2))
                def _(h=h):
                    step(False, [(c, h) for c in range(t // qs)])

    @pl.when(ki == 0)
    def _():
        for h in range(N_HEADS):
            sl = slice(h * HEAD_DIM, (h + 1) * HEAD_DIM)
            o = acc_scr[sl, :] / l_scr[h:h + 1, :]
            o_ref[:, sl] = o.T.astype(BF16)


def _pair_tables(n):
    qt = np.concatenate([np.full(i + 1, i, np.int32) for i in range(n)])
    kt = np.concatenate([np.arange(i, -1, -1, dtype=np.int32) for i in range(n)])
    return jnp.asarray(qt), jnp.asarray(kt)


def _flash(q_arr, q_col, k_arr, k_col, vt_arr, vt_blk, b, s, t, dk, name, mask=None, key_bias=None, norms=None,
           oldest=None):
    n = s // t
    qt, kt = _pair_tables(n)
    if oldest is None:
        oldest = jnp.zeros((b * n,), jnp.int32)
    wq = N_HEADS * dk
    wv = N_HEADS * HEAD_DIM

    def key_tile(bi, p, qt, kt, old):
        return jnp.maximum(kt[p], old[bi * n + qt[p]])

    in_specs = [
        pl.BlockSpec((t, wq), lambda bi, p, qt, kt, old: (bi * n + qt[p], q_col)),
        pl.BlockSpec((t, wq), lambda bi, p, qt, kt, old: (bi * n + key_tile(bi, p, qt, kt, old), k_col)),
        pl.BlockSpec((None, wv, t), lambda bi, p, qt, kt, old: (bi, vt_blk, key_tile(bi, p, qt, kt, old))),
    ]
    args = [q_arr, k_arr, vt_arr]
    extra = None
    if mask is not None:
        extra = "mask"
        in_specs.append(pl.BlockSpec((None, t, t), lambda bi, p, qt, kt, old: (bi, kt[p], qt[p])))
        args.append(mask)
    elif key_bias is not None:
        extra = "key_bias"
        in_specs.append(pl.BlockSpec((t, LANES), lambda bi, p, qt, kt, old: (bi * n + key_tile(bi, p, qt, kt, old), 0)))
        in_specs.append(pl.BlockSpec((t, LANES), lambda bi, p, qt, kt, old: (bi * n + qt[p], 0)))
        in_specs.append(pl.BlockSpec((t, LANES), lambda bi, p, qt, kt, old: (bi * n + key_tile(bi, p, qt, kt, old), 0)))
        args += [key_bias, norms, norms]
    scratch = [pltpu.VMEM((N_HEADS, t), F32), pltpu.VMEM((N_HEADS, t), F32), pltpu.VMEM((wv, t), F32)]
    if extra == "key_bias":
        scratch.append(pltpu.VMEM((1, LANES), F32))
    grid_spec = pltpu.PrefetchScalarGridSpec(
        num_scalar_prefetch=3,
        grid=(b, n * (n + 1) // 2),
        in_specs=in_specs,
        out_specs=pl.BlockSpec((t, wv), lambda bi, p, qt, kt, old: (bi * n + qt[p], 0)),
        scratch_shapes=scratch,
    )
    return pl.pallas_call(
        functools.partial(_flash_kernel, extra=extra, dk=dk, t=t, n_tiles=n),
        grid_spec=grid_spec,
        out_shape=jax.ShapeDtypeStruct((b * s, wv), BF16),
        compiler_params=_params("parallel", "arbitrary"),
        name=name,
    )(qt, kt, oldest, *args)


def _oldest_useful_tile(norms, key_bias, b, s, t):
    n = s // t
    nr = norms.reshape(b, n, t, LANES)
    kn2, qn2, own = nr[..., :N_HEADS], nr[..., N_HEADS:2 * N_HEADS], nr[..., 2 * N_HEADS:3 * N_HEADS]
    kb = key_bias.astype(F32).reshape(b, n, t, LANES)
    bias = kb[..., :N_HEADS] + kb[..., N_HEADS:2 * N_HEADS] + kb[..., 2 * N_HEADS:3 * N_HEADS]
    rounding = (NORM_SLACK - 1.0) * jnp.sqrt(qn2 * kn2)
    m_low = jnp.min(own + bias - rounding, axis=2)
    dot_bound = jnp.sqrt(jnp.max(qn2, axis=2)[:, :, None, :] * jnp.max(kn2, axis=2)[:, None, :, :]) * NORM_SLACK
    upper = dot_bound + jnp.max(bias, axis=2)[:, None, :, :]
    useful = jnp.any(upper > m_low[:, :, None, :] - UNDERFLOW_LOG2, axis=-1)
    qi = jnp.arange(n, dtype=jnp.int32)[None, :, None]
    kj = jnp.arange(n, dtype=jnp.int32)[None, None, :]
    first = jnp.min(jnp.where(useful & (kj <= qi), kj, qi), axis=2)
    return first.reshape(b * n).astype(jnp.int32)


def _count(sc_ref, nvalid, cand, tk, tq, ind_fn):
    group = _count_group(sc_ref.shape[0])

    lanes = COUNT_ACCS * 8

    def body(g, a):
        for e in range(group):
            c = g * group + e
            ind = ind_fn(sc_ref[c], cand, c)
            a = a + jnp.sum(ind.reshape(tk // lanes, lanes, tq), axis=0)
        return a

    part = lax.fori_loop(0, (nvalid + group - 1) // group, body, jnp.zeros((lanes, tq), jnp.int32))
    return jnp.sum(part, axis=0, keepdims=True)


def _count_group(nchunks):
    return 2 if nchunks % 2 == 0 else 1


def _indexer_kernel(q_ref, wt_ref, k2_ref, mask_ref, sc_ref, qm_scr, j_scr, *, tq, tk, n_sel, idx_bits):
    qi = pl.program_id(1)
    one, zero = jnp.int32(1), jnp.int32(0)
    nchunks = sc_ref.shape[0]
    nvalid = ((qi + 1) * tq + tk - 1) // tk
    qcol = qi * tq + lax.broadcasted_iota(jnp.int32, (1, tq), 1)
    krow0 = lax.broadcasted_iota(jnp.int32, (tk, 1), 0)

    half = lax.broadcasted_iota(jnp.int32, (tq, LANES), 1) // IDX_DIM
    for g in range(IDX_HEADS // 2):
        qp = q_ref[:, g * LANES:(g + 1) * LANES]
        for e in range(2):
            qm_scr[2 * g + e] = jnp.where(half == e, qp, jnp.zeros_like(qp))

    def score_chunk(c, carry):
        kmax, n_pos = carry
        kc = k2_ref[pl.ds(pl.multiple_of(c * tk, tk), tk), :]
        acc = jnp.zeros((tk, tq), F32)
        for h in range(IDX_HEADS):
            x = lax.dot_general(kc, qm_scr[h], (((1,), (1,)), ((), ())), preferred_element_type=F32)
            acc = acc + wt_ref[h:h + 1, :] * jnp.maximum(x, 0.0)
        score = acc * (IDX_HEADS * IDX_DIM) ** -0.5
        bits = pltpu.bitcast(score, jnp.int32)
        key = bits ^ ((bits >> 31) & 0x7FFFFFFF)
        causal = (c * tk + krow0) <= qcol
        key = jnp.where(causal, key, INT_MIN)
        sc_ref[c] = key
        nonneg = jnp.where(key >= 0, one, zero)
        return (jnp.maximum(kmax, jnp.max(key.reshape(tk // 8, 8, tq), axis=0)),
                n_pos + jnp.sum(nonneg.reshape(tk // 8, 8, tq), axis=0))

    kmax, n_pos = lax.fori_loop(0, nvalid, score_chunk,
                                (jnp.full((8, tq), INT_MIN, jnp.int32), jnp.zeros((8, tq), jnp.int32)))
    kmax = jnp.max(kmax, axis=0, keepdims=True)
    f0 = jnp.sum(n_pos, axis=0, keepdims=True)

    group = _count_group(nchunks)
    if group > 1:
        def pad_chunk(c, carry):
            sc_ref[c] = jnp.full((tk, tq), INT_MIN, jnp.int32)
            return carry

        lax.fori_loop(nvalid, (nvalid + group - 1) // group * group, pad_chunk, 0)

    def count_ge(cand):
        return _count(sc_ref, nvalid, cand, tk, tq, lambda k, cd, c: jnp.where(k >= cd, one, zero))

    n_keys = qcol + 1
    pos = f0 >= n_sel
    lo0 = jnp.where(pos, 0, INT_MIN + 1)
    hi0 = jnp.where(pos, kmax, jnp.minimum(kmax, -1)) + 1
    flo0 = jnp.where(pos, f0, n_keys)

    def searching(flo, w):
        return jnp.where(flo > n_sel, jnp.where(w > 1, 1.0, 0.0), 0.0)

    def bisect_cond(carry):
        it, lo, w, flo = carry
        return (jnp.max(searching(flo, w)) > 0.0) & (it < 32)

    def bisect_body(carry):
        it, lo, w, flo = carry
        active = searching(flo, w) > 0
        off = w >> 1
        near_top = w - FIRST_PROBE_BELOW_MAX
        off = jnp.where(it == 0, jnp.where(near_top >= 1, near_top, off), off)
        cand = lo + off
        cnt = count_ge(cand)
        ok = cnt >= n_sel
        lo = jnp.where(active, jnp.where(ok, cand, lo), lo)
        flo = jnp.where(active, jnp.where(ok, cnt, flo), flo)
        w = jnp.where(active, jnp.where(ok, w - off, off), w)
        return it + 1, lo, w, flo

    _, lo, _, flo = lax.while_loop(bisect_cond, bisect_body, (jnp.int32(0), lo0, hi0 - lo0, flo0))
    found = n_keys >= n_sel
    tie = jnp.where(found, jnp.where(flo > n_sel, one, zero), zero) > 0
    gt_thr = jnp.where(found, jnp.where(tie, lo, lo - 1), INT_MIN)
    thr = jnp.where(tie, lo, INT_MIN + 1)

    j_scr[...] = jnp.full_like(j_scr, 2 ** idx_bits - 1)
    has_tie = jnp.max(jnp.where(tie, 1.0, 0.0)) > 0.0

    @pl.when(has_tie)
    def _():
        cgt = _count(sc_ref, nvalid, thr, tk, tq, lambda k, cd, c: jnp.where(k > cd, one, zero))
        need = n_sel - cgt

        def jbit(i, jv):
            cand = jv | lax.shift_left(jnp.int32(1), idx_bits - 1 - i)
            cnt = _count(sc_ref, nvalid, cand, tk, tq,
                         lambda k, cd, c: jnp.where(k == thr, jnp.where((c * tk + krow0) < cd, one, zero), zero))
            return jnp.where(cnt < need, cand, jv)

        j_scr[...] = lax.fori_loop(0, idx_bits, jbit, jnp.zeros((1, tq), jnp.int32))

    jlast = j_scr[...]

    def write_chunk(c, carry):
        key = sc_ref[c]
        sel = jnp.where(key > gt_thr, one,
                        jnp.where(key == thr, jnp.where((c * tk + krow0) <= jlast, one, zero), zero))
        mask_ref[pl.ds(pl.multiple_of(c * tk, tk), tk), :] = sel.astype(jnp.int8)
        return carry

    lax.fori_loop(0, nvalid, write_chunk, 0)

    def zero_chunk(c, carry):
        mask_ref[pl.ds(pl.multiple_of(c * tk, tk), tk), :] = jnp.zeros((tk, tq), jnp.int8)
        return carry

    lax.fori_loop(nvalid, nchunks, zero_chunk, 0)


def _indexer_mask(zb, wt, k2, b, s, tq, tk, n_sel):
    nq = s // tq
    idx_bits = max(1, int(np.ceil(np.log2(s))))
    return pl.pallas_call(
        functools.partial(_indexer_kernel, tq=tq, tk=tk, n_sel=n_sel, idx_bits=idx_bits),
        grid=(b, nq),
        in_specs=[
            pl.BlockSpec((tq, IDX_HEADS * IDX_DIM), lambda bi, i: (bi * nq + i, COL_IQ // (IDX_HEADS * IDX_DIM))),
            pl.BlockSpec((None, IDX_HEADS, tq), lambda bi, i: (bi, 0, i)),
            pl.BlockSpec((None, s, LANES), lambda bi, i: (bi, 0, 0)),
        ],
        out_specs=pl.BlockSpec((None, s, tq), lambda bi, i: (bi, 0, i)),
        out_shape=jax.ShapeDtypeStruct((b, s, s), jnp.int8),
        scratch_shapes=[
            pltpu.VMEM((s // tk, tk, tq), jnp.int32),
            pltpu.VMEM((IDX_HEADS, tq, LANES), BF16),
            pltpu.VMEM((1, tq), jnp.int32),
        ],
        compiler_params=_params("parallel", "arbitrary"),
        name="indexer_mask",
    )(zb, wt, k2)


def _mla_up_kernel(c_ref, zs_ref, qg_ref, kvg_ref, wq_ref, wk_ref, wv_ref, c64_ref, s64l_ref, s64h_ref,
                   q_ref, k_ref, v_ref, *, q_scale):
    def norm(xf, g):
        y = xf * lax.rsqrt(jnp.mean(xf * xf, axis=-1, keepdims=True) + EPS)
        return (y * g).astype(BF16)

    cq = norm(c_ref[:, :Q_LORA].astype(F32), qg_ref[...])
    ckv = norm(c_ref[:, Q_LORA:].astype(F32), kvg_ref[...])
    q = jnp.dot(cq, wq_ref[...], preferred_element_type=F32) * q_scale
    k = jnp.dot(ckv, wk_ref[...], preferred_element_type=F32)
    v_ref[...] = jnp.dot(ckv, wv_ref[...], preferred_element_type=F32).T.astype(BF16)
    lane = lax.broadcasted_iota(jnp.int32, (1, LANES), 1)
    k_rot = jnp.where(lane < QK_ROPE, zs_ref[:, LANES:], 0.0)
    cos, sin_lo, sin_hi = c64_ref[...], s64l_ref[...], s64h_ref[...]
    for h in range(N_HEADS):
        nope = slice(h * MLA_QK_PAD, h * MLA_QK_PAD + QK_NOPE)
        rope = slice(h * MLA_QK_PAD + QK_NOPE, (h + 1) * MLA_QK_PAD)
        q_ref[:, nope] = q[:, nope].astype(BF16)
        q_ref[:, rope] = _rope64(q[:, rope], cos, sin_lo, sin_hi).astype(BF16)
        k_ref[:, nope] = k[:, nope].astype(BF16)
        k_ref[:, rope] = k_rot.astype(BF16)


def _mla_up(zb, zs, qg, kvg, wq, wk, wv, tabs, tm):
    m = zb.shape[0]
    _, _, c64, s64l, s64h = tabs
    nsb = c64.shape[0] // tm
    tab_spec = pl.BlockSpec((tm, LANES), lambda i: (i % nsb, 0))
    wqk = N_HEADS * MLA_QK_PAD
    full = lambda a: pl.BlockSpec(a.shape, lambda i: (0,) * a.ndim)
    return pl.pallas_call(
        functools.partial(_mla_up_kernel, q_scale=(QK_NOPE + QK_ROPE) ** -0.5 * LOG2E),
        grid=(m // tm,),
        in_specs=[
            pl.BlockSpec((tm, Q_LORA + KV_LORA), lambda i: (i, COL_CQ // (Q_LORA + KV_LORA))),
            pl.BlockSpec((tm, 2 * LANES), lambda i: (i, 0)),
            full(qg), full(kvg), full(wq), full(wk), full(wv),
            tab_spec, tab_spec, tab_spec,
        ],
        out_specs=[
            pl.BlockSpec((tm, wqk), lambda i: (i, 0)),
            pl.BlockSpec((tm, wqk), lambda i: (i, 0)),
            pl.BlockSpec((None, BRANCH_WIDTH, tm), lambda i: (i // nsb, 0, i % nsb)),
        ],
        out_shape=[
            jax.ShapeDtypeStruct((m, wqk), BF16),
            jax.ShapeDtypeStruct((m, wqk), BF16),
            jax.ShapeDtypeStruct((m // (nsb * tm), BRANCH_WIDTH, nsb * tm), BF16),
        ],
        compiler_params=_params("parallel"),
        name="mla_up",
    )(zb, zs, qg, kvg, wq, wk, wv, c64, s64l, s64h)


def _branch_kernel(o0_ref, o1_ref, o2_ref, g0_ref, g1_ref, g2_ref, m0_ref, m1_ref, m2_ref, w_ref, y_ref):
    y = None
    for i, (o_ref, g_ref, m_ref) in enumerate(((o0_ref, g0_ref, m0_ref), (o1_ref, g1_ref, m1_ref),
                                               (o2_ref, g2_ref, m2_ref))):
        g = g_ref[...].astype(F32)
        a = (o_ref[...].astype(F32) * (g * jax.nn.sigmoid(g))).astype(BF16)
        term = jax.nn.sigmoid(m_ref[...].astype(F32)) * jnp.dot(a, w_ref[i], preferred_element_type=F32)
        y = term if y is None else y + term
    y_ref[...] = y.astype(BF16)


def _branch_merge(o_fox, o_dsa, o_mla, zb, wb, d, tm, tn):
    m = zb.shape[0]
    o_spec = pl.BlockSpec((tm, BRANCH_WIDTH), lambda i, j: (i, 0))
    silu0 = (COL_MERGE + N_BRANCH * d) // BRANCH_WIDTH

    def g_spec(i_br):
        return pl.BlockSpec((tm, BRANCH_WIDTH), lambda i, j: (i, silu0 + i_br))

    def m_spec(i_br):
        base = (COL_MERGE + i_br * d) // tn
        return pl.BlockSpec((tm, tn), lambda i, j: (i, base + j))

    return pl.pallas_call(
        _branch_kernel,
        grid=(m // tm, d // tn),
        in_specs=[o_spec, o_spec, o_spec, g_spec(0), g_spec(1), g_spec(2), m_spec(0), m_spec(1), m_spec(2),
                  pl.BlockSpec((N_BRANCH, BRANCH_WIDTH, tn), lambda i, j: (0, 0, j))],
        out_specs=pl.BlockSpec((tm, tn), lambda i, j: (i, j)),
        out_shape=jax.ShapeDtypeStruct((m, d), BF16),
        compiler_params=_params("parallel", "arbitrary"),
        name="branch_merge",
    )(o_fox, o_dsa, o_mla, zb, zb, zb, zb, zb, zb, wb)


def _out_kernel(x_ref, y_ref, w_ref, g_ref, o_ref, *, final_norm):
    r = x_ref[...] + jnp.dot(y_ref[...], w_ref[...], preferred_element_type=F32)
    if final_norm:
        r = r * lax.rsqrt(jnp.mean(r * r, axis=-1, keepdims=True) + EPS) * g_ref[...]
    o_ref[...] = r


def _out_proj(x2, y, w, g, tm, final_norm):
    m, d = x2.shape
    return pl.pallas_call(
        functools.partial(_out_kernel, final_norm=final_norm),
        grid=(m // tm,),
        in_specs=[
            pl.BlockSpec((tm, d), lambda i: (i, 0)),
            pl.BlockSpec((tm, d), lambda i: (i, 0)),
            pl.BlockSpec((d, d), lambda i: (0, 0)),
            pl.BlockSpec((1, d), lambda i: (0, 0)),
        ],
        out_specs=pl.BlockSpec((tm, d), lambda i: (i, 0)),
        out_shape=jax.ShapeDtypeStruct((m, d), F32),
        compiler_params=_params("parallel"),
        name="out_proj",
    )(x2, y, w, g)


def _split_w_in(w_in, d):
    sizes = (BRANCH_WIDTH, BRANCH_WIDTH, BRANCH_WIDTH, N_HEADS, BRANCH_WIDTH, BRANCH_WIDTH, BRANCH_WIDTH,
             IDX_HEADS * IDX_DIM, IDX_DIM, IDX_HEADS, Q_LORA, KV_LORA, QK_ROPE,
             N_BRANCH * BRANCH_WIDTH, N_BRANCH * d)
    pts = np.cumsum(sizes)[:-1].tolist()
    (f_q, f_k, f_v, f_gate, d_q, d_k, d_v, i_q, i_k, i_w, c_q, c_kv, k_pe, silu_g, merge_g) = jnp.split(
        w_in, pts, axis=-1)
    big = jnp.concatenate([f_q, f_k, f_v, d_q, d_k, d_v, i_q, c_q, c_kv, merge_g, silu_g], axis=-1).astype(BF16)
    pad = jnp.zeros((w_in.shape[0], LANES - FOX_BIAS_COLS - IDX_HEADS), w_in.dtype)
    small = jnp.concatenate([f_gate, f_gate, f_gate, i_w, pad, k_pe, i_k], axis=-1).astype(BF16)
    return big, small


def _pad_heads(w, take, width_in):
    r = w.shape[0]
    w3 = w.reshape(r, N_HEADS, width_in)[:, :, take]
    w3 = jnp.pad(w3, ((0, 0), (0, 0), (0, MLA_QK_PAD - w3.shape[-1])))
    return w3.reshape(r, N_HEADS * MLA_QK_PAD)


def _tile(s, pref):
    return pref if s % pref == 0 else s


def kernel(x, norm_g, w_in, forget_b, q_norm_g, w_q_up, kv_norm_g, w_kv_up, w_branch, w_out, final_norm_g):
    b, s, d = x.shape
    depth = w_in.shape[0]
    n_sel = min(TOPK_MAX, s // 4)
    m = b * s
    tabs = _rope_tables(s)
    t_att = _tile(s, 512)
    tm_in = _tile(s, 1024)
    x2 = x.reshape(m, d)
    for l in range(depth):
        w_big, w_small = _split_w_in(w_in[l], d)
        zb, zs, vt, norms = _in_proj(x2, norm_g[l].reshape(1, d), w_big, w_small, tabs, tm_in, 512)

        key_bias = _forget_cumsum(zs, forget_b[l].reshape(1, N_HEADS), b, s, _tile(s, 256))
        t_fox = _tile(s, 1024)
        o_fox = _flash(zb, COL_FQ // BRANCH_WIDTH, zb, COL_FK // BRANCH_WIDTH, vt, 0,
                       b, s, t_fox, HEAD_DIM, "flash_fox", key_bias=key_bias, norms=norms,
                       oldest=_oldest_useful_tile(norms, key_bias, b, s, t_fox))

        zs3 = zs.reshape(b, s, 2 * LANES)
        wt = jnp.swapaxes(zs3[:, :, FOX_BIAS_COLS:FOX_BIAS_COLS + IDX_HEADS], 1, 2)
        ik = zs3[:, :, LANES + QK_ROPE:].astype(BF16)
        k2 = jnp.concatenate([ik, ik], axis=-1)
        mask = _indexer_mask(zb, wt, k2, b, s, _tile(s, 256), _tile(s, 512), n_sel)
        o_dsa = _flash(zb, COL_DQ // BRANCH_WIDTH, zb, COL_DK // BRANCH_WIDTH, vt, 1,
                       b, s, t_att, HEAD_DIM, "flash_mask", mask=mask)

        wq = _pad_heads(w_q_up[l], slice(0, QK_NOPE + QK_ROPE), QK_NOPE + QK_ROPE).astype(BF16)
        wk = _pad_heads(w_kv_up[l], slice(0, QK_NOPE), QK_NOPE + V_DIM).astype(BF16)
        wv = w_kv_up[l].reshape(KV_LORA, N_HEADS, QK_NOPE + V_DIM)[:, :, QK_NOPE:].reshape(
            KV_LORA, BRANCH_WIDTH).astype(BF16)
        q_mla, k_mla, vt_mla = _mla_up(zb, zs, q_norm_g[l].reshape(1, Q_LORA), kv_norm_g[l].reshape(1, KV_LORA),
                                       wq, wk, wv, tabs, _tile(s, 512))
        o_mla = _flash(q_mla, 0, k_mla, 0, vt_mla, 0, b, s, t_att, MLA_QK_PAD, "flash_plain")

        y = _branch_merge(o_fox, o_dsa, o_mla, zb, w_branch[l].astype(BF16), d, _tile(s, 512), 1024)
        last = l == depth - 1
        x2 = _out_proj(x2, y, w_out[l].astype(BF16), final_norm_g.reshape(1, d), _tile(s, 512), last)
    return x2.reshape(b, s, d)
```

```python
import functools

import jax
import jax.numpy as jnp
import numpy as np
from jax import lax
from jax.experimental import pallas as pl
from jax.experimental.pallas import tpu as pltpu

F32 = jnp.float32
BF16 = jnp.bfloat16

HEAD_DIM = 128
N_HEADS = 8
BRANCH_WIDTH = N_HEADS * HEAD_DIM
N_BRANCH = 3
IDX_HEADS = 16
IDX_DIM = 64
TOPK_MAX = 256
Q_LORA = 512
KV_LORA = 512
QK_NOPE = 128
QK_ROPE = 64
V_DIM = 128
MLA_QK_PAD = 256
FOX_BIAS_COLS = 3 * N_HEADS
NORM_SLACK = 1.02
UNDERFLOW_LOG2 = 160.0
ONES_ROWS = 16
ROPE_THETA = 10000.0
EPS = 1e-6

LANES = 128
VMEM_LIMIT = 56 * 1024 * 1024
NEG_BIG = -1e30
INT_MIN = -(2 ** 31)
LOG2E = 1.4426950408889634
Q_STRIP = 256
FIRST_PROBE_BELOW_MAX = 2 ** 24
COUNT_ACCS = 4
PIPE_LEAD = 2
PIPE_LAG = 2

COL_FQ, COL_FK, COL_FV = 0, 1024, 2048
COL_DQ, COL_DK, COL_DV = 3072, 4096, 5120
COL_IQ = 6144
COL_CQ = 7168
COL_MERGE = 8192


def _params(*sem):
    return pltpu.CompilerParams(dimension_semantics=sem, vmem_limit_bytes=VMEM_LIMIT)


def _rope_tables(s):
    pos = jnp.arange(s, dtype=F32)

    def cs(half):
        inv = ROPE_THETA ** (-jnp.arange(half, dtype=F32) / half)
        ang = pos[:, None] * inv[None, :]
        return jnp.cos(ang), jnp.sin(ang)

    c64, s64 = cs(64)
    c32, s32 = cs(32)
    z32 = jnp.zeros_like(s32)
    cos128 = jnp.concatenate([c64, c64], axis=-1)
    sin128 = jnp.concatenate([-s64, s64], axis=-1)
    cos64 = jnp.concatenate([c32, c32, c32, c32], axis=-1)
    sin64_lo = jnp.concatenate([-s32, z32, -s32, z32], axis=-1)
    sin64_hi = jnp.concatenate([z32, s32, z32, s32], axis=-1)
    return cos128, sin128, cos64, sin64_lo, sin64_hi


def _rope128(x, cos, sin):
    return x * cos + pltpu.roll(x, 64, 1) * sin


def _rope64(x, cos, sin_lo, sin_hi):
    return x * cos + pltpu.roll(x, 96, 1) * sin_lo + pltpu.roll(x, 32, 1) * sin_hi


def _in_proj_kernel(x_ref, g_ref, w_ref, ws_ref, c128_ref, s128_ref, c64_ref, s64l_ref, s64h_ref,
                    zb_ref, zs_ref, vt_ref, nrm_ref, h_scr, *, tn, q_scale):
    j = pl.program_id(1)

    @pl.when(j == 0)
    def _():
        nrm_ref[...] = jnp.zeros_like(nrm_ref)
        xf = x_ref[...]
        y = xf * lax.rsqrt(jnp.mean(xf * xf, axis=-1, keepdims=True) + EPS)
        h = (y * g_ref[...]).astype(BF16)
        h_scr[...] = h
        zs = jnp.dot(h, ws_ref[...], preferred_element_type=F32)
        zs_ref[:, :LANES] = zs[:, :LANES]
        zs_ref[:, LANES:] = _rope64(zs[:, LANES:], c64_ref[...], s64l_ref[...], s64h_ref[...])

    acc = jnp.dot(h_scr[...], w_ref[...], preferred_element_type=F32)
    col = j * tn
    is_q = ((col >= COL_FQ) & (col < COL_FK)) | ((col >= COL_DQ) & (col < COL_DK))
    acc = acc * jnp.where(is_q, q_scale, 1.0).astype(F32)
    is_r128 = (col >= COL_DQ) & (col < COL_DV)
    is_r64 = (col >= COL_IQ) & (col < COL_CQ)

    @pl.when(is_r128)
    def _():
        for g in range(tn // LANES):
            sl = slice(g * LANES, (g + 1) * LANES)
            zb_ref[:, sl] = _rope128(acc[:, sl], c128_ref[...], s128_ref[...]).astype(BF16)

    @pl.when(is_r64)
    def _():
        for g in range(tn // LANES):
            sl = slice(g * LANES, (g + 1) * LANES)
            zb_ref[:, sl] = _rope64(acc[:, sl], c64_ref[...], s64l_ref[...], s64h_ref[...]).astype(BF16)

    @pl.when(jnp.logical_not(is_r128 | is_r64))
    def _():
        zb_ref[...] = acc.astype(BF16)

    is_v = ((col >= COL_FV) & (col < COL_DQ)) | ((col >= COL_DV) & (col < COL_IQ))

    @pl.when(is_v)
    def _():
        vt_ref[...] = acc.T.astype(BF16)

    is_fq = (col >= COL_FQ) & (col < COL_FK)
    is_fk = (col >= COL_FK) & (col < COL_FV)

    @pl.when(is_fq | is_fk)
    def _():
        lane = lax.broadcasted_iota(jnp.int32, (1, LANES), 1)
        first = (col % BRANCH_WIDTH) // HEAD_DIM + jnp.where(is_fq, N_HEADS, 0)
        nrm = nrm_ref[...]
        for g in range(tn // HEAD_DIM):
            a = acc[:, g * HEAD_DIM:(g + 1) * HEAD_DIM]
            nrm = jnp.where(lane == first + g, jnp.sum(a * a, axis=1, keepdims=True), nrm)
        nrm_ref[...] = nrm


def _vt_row_block(j, tn):
    per = BRANCH_WIDTH // tn
    fv0, dv0 = COL_FV // tn, COL_DV // tn
    in_fox = jnp.clip(j - fv0, 0, per - 1)
    in_dsa = per + jnp.clip(j - dv0, 0, per - 1)
    return jnp.where(j < dv0, in_fox, in_dsa)


def _in_proj(x2, g, w_big, w_small, tabs, tm, tn):
    m, d = x2.shape
    n = w_big.shape[1]
    c128, s128, c64, s64l, s64h = tabs
    s = c128.shape[0]
    nsb = s // tm
    tab_spec = pl.BlockSpec((tm, LANES), lambda i, j: (i % nsb, 0))
    return pl.pallas_call(
        functools.partial(_in_proj_kernel, tn=tn, q_scale=HEAD_DIM ** -0.5 * LOG2E),
        grid=(m // tm, n // tn),
        in_specs=[
            pl.BlockSpec((tm, d), lambda i, j: (i, 0)),
            pl.BlockSpec((1, d), lambda i, j: (0, 0)),
            pl.BlockSpec((d, tn), lambda i, j: (0, j)),
            pl.BlockSpec((d, 2 * LANES), lambda i, j: (0, 0)),
            tab_spec, tab_spec, tab_spec, tab_spec, tab_spec,
        ],
        out_specs=[
            pl.BlockSpec((tm, tn), lambda i, j: (i, j)),
            pl.BlockSpec((tm, 2 * LANES), lambda i, j: (i, 0)),
            pl.BlockSpec((None, tn, tm), lambda i, j: (i // nsb, _vt_row_block(j, tn), i % nsb)),
            pl.BlockSpec((tm, LANES), lambda i, j: (i, 0)),
        ],
        out_shape=[
            jax.ShapeDtypeStruct((m, n), BF16),
            jax.ShapeDtypeStruct((m, 2 * LANES), F32),
            jax.ShapeDtypeStruct((m // s, 2 * BRANCH_WIDTH, s), BF16),
            jax.ShapeDtypeStruct((m, LANES), F32),
        ],
        scratch_shapes=[pltpu.VMEM((tm, d), BF16)],
        compiler_params=_params("parallel", "arbitrary"),
        name="in_proj",
    )(x2, g, w_big, w_small, c128, s128, c64, s64l, s64h)


def _bf16_floor(x):
    return pltpu.bitcast(pltpu.bitcast(x, jnp.int32) & jnp.int32(-65536), F32)


def _cum_kernel(zs_ref, fb_ref, kb_ref, carry_scr, *, t):
    @pl.when(pl.program_id(1) == 0)
    def _():
        carry_scr[...] = jnp.zeros_like(carry_scr)

    lane = lax.broadcasted_iota(jnp.int32, (1, LANES), 1)
    x = zs_ref[:, :LANES] + fb_ref[...]
    logf = -(jnp.maximum(-x, 0.0) + jnp.log1p(jnp.exp(-jnp.abs(x))))
    logf = jnp.where(lane < FOX_BIAS_COLS, logf, 0.0)
    r = lax.broadcasted_iota(jnp.int32, (t, t), 0)
    c = lax.broadcasted_iota(jnp.int32, (t, t), 1)
    tri = jnp.where(c <= r, 1.0, 0.0).astype(F32)
    cum = jnp.dot(tri, logf, preferred_element_type=F32, precision=lax.Precision.HIGHEST) + carry_scr[...]
    carry_scr[...] = cum[t - 1:t, :]
    c = cum * (-LOG2E)
    hi = _bf16_floor(c)
    mid = _bf16_floor(c - hi)
    lo = (c - hi) - mid
    kb_ref[...] = jnp.where(lane < N_HEADS, hi, jnp.where(lane < 2 * N_HEADS, mid, lo)).astype(BF16)


def _forget_cumsum(zs, fb, b, s, t):
    fb128 = jnp.pad(jnp.tile(fb, (1, 3)), ((0, 0), (0, LANES - FOX_BIAS_COLS)))
    return pl.pallas_call(
        functools.partial(_cum_kernel, t=t),
        grid=(b, s // t),
        in_specs=[
            pl.BlockSpec((t, 2 * LANES), lambda bi, i: (bi * (s // t) + i, 0)),
            pl.BlockSpec((1, LANES), lambda bi, i: (0, 0)),
        ],
        out_specs=pl.BlockSpec((t, LANES), lambda bi, i: (bi * (s // t) + i, 0)),
        out_shape=jax.ShapeDtypeStruct((b * s, LANES), BF16),
        scratch_shapes=[pltpu.VMEM((1, LANES), F32)],
        compiler_params=_params("parallel", "arbitrary"),
        name="forget_cumsum",
    )(zs, fb128)


def _flash_kernel(qt_ref, kt_ref, q_ref, k_ref, v_ref, *rest, extra, dk, t):
    qn_scr = None
    if extra is None:
        o_ref, m_scr, l_scr, acc_scr = rest
        extra_ref = None
    elif extra == "mask":
        extra_ref, o_ref, m_scr, l_scr, acc_scr = rest
    else:
        extra_ref, qnorm_ref, knorm_ref, o_ref, m_scr, l_scr, acc_scr, qn_scr = rest
    masked = extra == "mask"
    p = pl.program_id(1)
    qi = qt_ref[p]
    ki = kt_ref[p]
    qs = min(t, Q_STRIP)

    @pl.when(ki == qi)
    def _():
        m_scr[...] = jnp.full_like(m_scr, NEG_BIG)
        l_scr[...] = jnp.zeros_like(l_scr)
        acc_scr[...] = jnp.zeros_like(acc_scr)
        if qn_scr is not None:
            qn_scr[...] = pltpu.roll(jnp.max(qnorm_ref[...], axis=0, keepdims=True), LANES - N_HEADS, 1)

    def head_margins():
        kn = jnp.max(knorm_ref[...], axis=0, keepdims=True)
        dot_bound = jnp.sqrt(qn_scr[...] * kn) * NORM_SLACK
        bm = jnp.max(extra_ref[...].astype(F32), axis=0, keepdims=True)
        bias = bm + pltpu.roll(bm, LANES - N_HEADS, 1) + pltpu.roll(bm, LANES - 2 * N_HEADS, 1)
        bound = dot_bound + bias
        m_min = jnp.min(m_scr[...], axis=1, keepdims=True)
        head = lax.broadcasted_iota(jnp.int32, (N_HEADS, LANES), 0)
        lane = lax.broadcasted_iota(jnp.int32, (N_HEADS, LANES), 1)
        margin = jnp.where(head == lane, m_min - bound, jnp.inf)
        return jnp.min(margin, axis=1, keepdims=True)

    all_units = [(c, h) for c in range(t // qs) for h in range(N_HEADS)]

    def step(diag, units):

        def n_keys(c):
            return (c + 1) * qs if diag else t

        def logits(c, h):
            k_h = k_ref[:n_keys(c), h * dk:(h + 1) * dk]
            q_h = q_ref[c * qs:(c + 1) * qs, h * dk:(h + 1) * dk]
            if extra == "key_bias":
                lane = lax.broadcasted_iota(jnp.int32, (qs, LANES), 1)
                pick = jnp.where(lane < FOX_BIAS_COLS, jnp.where(lane % N_HEADS == h, 1.0, 0.0), 0.0)
                k_h = jnp.concatenate([k_h, extra_ref[:n_keys(c), :]], axis=1)
                q_h = jnp.concatenate([q_h, pick.astype(q_h.dtype)], axis=1)
            return lax.dot_general(k_h, q_h, (((1,), (1,)), ((), ())), preferred_element_type=F32)

        def keep_mask(c):
            nk = n_keys(c)
            if masked:
                return extra_ref[:nk, c * qs:(c + 1) * qs].astype(jnp.int32) != 0
            if diag:
                kr = lax.broadcasted_iota(jnp.int32, (nk, qs), 0)
                qc = lax.broadcasted_iota(jnp.int32, (nk, qs), 1) + c * qs
                return kr <= qc
            return None

        keeps = {}

        def softmax(c, h, st):
            cs = slice(c * qs, (c + 1) * qs)
            if c not in keeps:
                keeps[c] = keep_mask(c)
            if keeps[c] is not None:
                st = jnp.where(keeps[c], st, NEG_BIG)
            m_prev = m_scr[h:h + 1, cs]
            m_new = jnp.maximum(m_prev, jnp.max(st, axis=0, keepdims=True))
            m_scr[h:h + 1, cs] = m_new
            alpha = jnp.exp2(m_prev - m_new)
            return alpha, jnp.exp2(st - m_new).astype(BF16)

        def accumulate(c, h, alpha, pt):
            cs = slice(c * qs, (c + 1) * qs)
            nk = n_keys(c)
            ones = jnp.ones((ONES_ROWS, nk), v_ref.dtype)
            vt_h = jnp.concatenate([v_ref[h * HEAD_DIM:(h + 1) * HEAD_DIM, :nk], ones], axis=0)
            pv = jnp.dot(vt_h, pt, preferred_element_type=F32)
            sl = slice(h * HEAD_DIM, (h + 1) * HEAD_DIM)
            acc_scr[sl, cs] = alpha * acc_scr[sl, cs] + pv[:HEAD_DIM]
            l_scr[h:h + 1, cs] = alpha * l_scr[h:h + 1, cs] + pv[HEAD_DIM:HEAD_DIM + 1]

        n_u = len(units)
        st_q = [logits(*units[i]) for i in range(min(PIPE_LEAD, n_u))]
        p_q = []
        for u in range(n_u + PIPE_LAG):
            if u + PIPE_LEAD < n_u:
                st_q.append(logits(*units[u + PIPE_LEAD]))
            if u < n_u:
                p_q.append(softmax(*units[u], st_q.pop(0)))
            if u >= PIPE_LAG:
                accumulate(*units[u - PIPE_LAG], *p_q.pop(0))

    @pl.when(ki == qi)
    def _():
        step(True, all_units)

    if qn_scr is None:
        @pl.when(ki != qi)
        def _():
            step(False, all_units)
    else:
        margins = head_margins()
        off_diag = ki != qi
        every_head = jnp.logical_and(off_diag, jnp.max(margins) <= UNDERFLOW_LOG2)

        @pl.when(every_head)
        def _():
            step(False, all_units)

        some_heads = jnp.logical_and(off_diag, jnp.logical_not(every_head))
        for h in range(N_HEADS):
            @pl.when(jnp.logical_and(some_heads, margins[h, 0] <= UNDERFLOW_LOG2))
            def _(h=h):
                step(False, [(c, h) for c in range(t // qs)])

    @pl.when(ki == 0)
    def _():
        for h in range(N_HEADS):
            sl = slice(h * HEAD_DIM, (h + 1) * HEAD_DIM)
            o = acc_scr[sl, :] / l_scr[h:h + 1, :]
            o_ref[:, sl] = o.T.astype(BF16)


def _pair_tables(n):
    qt = np.concatenate([np.full(i + 1, i, np.int32) for i in range(n)])
    kt = np.concatenate([np.arange(i, -1, -1, dtype=np.int32) for i in range(n)])
    return jnp.asarray(qt), jnp.asarray(kt)


def _flash(q_arr, q_col, k_arr, k_col, vt_arr, vt_blk, b, s, t, dk, name, mask=None, key_bias=None, norms=None):
    n = s // t
    qt, kt = _pair_tables(n)
    wq = N_HEADS * dk
    wv = N_HEADS * HEAD_DIM
    in_specs = [
        pl.BlockSpec((t, wq), lambda bi, p, qt, kt: (bi * n + qt[p], q_col)),
        pl.BlockSpec((t, wq), lambda bi, p, qt, kt: (bi * n + kt[p], k_col)),
        pl.BlockSpec((None, wv, t), lambda bi, p, qt, kt: (bi, vt_blk, kt[p])),
    ]
    args = [q_arr, k_arr, vt_arr]
    extra = None
    if mask is not None:
        extra = "mask"
        in_specs.append(pl.BlockSpec((None, t, t), lambda bi, p, qt, kt: (bi, kt[p], qt[p])))
        args.append(mask)
    elif key_bias is not None:
        extra = "key_bias"
        in_specs.append(pl.BlockSpec((t, LANES), lambda bi, p, qt, kt: (bi * n + kt[p], 0)))
        in_specs.append(pl.BlockSpec((t, LANES), lambda bi, p, qt, kt: (bi * n + qt[p], 0)))
        in_specs.append(pl.BlockSpec((t, LANES), lambda bi, p, qt, kt: (bi * n + kt[p], 0)))
        args += [key_bias, norms, norms]
    scratch = [pltpu.VMEM((N_HEADS, t), F32), pltpu.VMEM((N_HEADS, t), F32), pltpu.VMEM((wv, t), F32)]
    if extra == "key_bias":
        scratch.append(pltpu.VMEM((1, LANES), F32))
    grid_spec = pltpu.PrefetchScalarGridSpec(
        num_scalar_prefetch=2,
        grid=(b, n * (n + 1) // 2),
        in_specs=in_specs,
        out_specs=pl.BlockSpec((t, wv), lambda bi, p, qt, kt: (bi * n + qt[p], 0)),
        scratch_shapes=scratch,
    )
    return pl.pallas_call(
        functools.partial(_flash_kernel, extra=extra, dk=dk, t=t),
        grid_spec=grid_spec,
        out_shape=jax.ShapeDtypeStruct((b * s, wv), BF16),
        compiler_params=_params("parallel", "arbitrary"),
        name=name,
    )(qt, kt, *args)


def _count(sc_ref, nvalid, cand, tk, tq, ind_fn):
    group = _count_group(sc_ref.shape[0])

    lanes = COUNT_ACCS * 8

    def body(g, a):
        for e in range(group):
            c = g * group + e
            ind = ind_fn(sc_ref[c], cand, c)
            a = a + jnp.sum(ind.reshape(tk // lanes, lanes, tq), axis=0)
        return a

    part = lax.fori_loop(0, (nvalid + group - 1) // group, body, jnp.zeros((lanes, tq), jnp.int32))
    return jnp.sum(part, axis=0, keepdims=True)


def _count_group(nchunks):
    return 2 if nchunks % 2 == 0 else 1


def _indexer_kernel(q_ref, wt_ref, k2_ref, mask_ref, sc_ref, qm_scr, j_scr, *, tq, tk, n_sel, idx_bits):
    qi = pl.program_id(1)
    one, zero = jnp.int32(1), jnp.int32(0)
    nchunks = sc_ref.shape[0]
    nvalid = ((qi + 1) * tq + tk - 1) // tk
    qcol = qi * tq + lax.broadcasted_iota(jnp.int32, (1, tq), 1)
    krow0 = lax.broadcasted_iota(jnp.int32, (tk, 1), 0)

    half = lax.broadcasted_iota(jnp.int32, (tq, LANES), 1) // IDX_DIM
    for g in range(IDX_HEADS // 2):
        qp = q_ref[:, g * LANES:(g + 1) * LANES]
        for e in range(2):
            qm_scr[2 * g + e] = jnp.where(half == e, qp, jnp.zeros_like(qp))

    def score_chunk(c, carry):
        kmax, n_pos = carry
        kc = k2_ref[pl.ds(pl.multiple_of(c * tk, tk), tk), :]
        acc = jnp.zeros((tk, tq), F32)
        for h in range(IDX_HEADS):
            x = lax.dot_general(kc, qm_scr[h], (((1,), (1,)), ((), ())), preferred_element_type=F32)
            acc = acc + wt_ref[h:h + 1, :] * jnp.maximum(x, 0.0)
        score = acc * (IDX_HEADS * IDX_DIM) ** -0.5
        bits = pltpu.bitcast(score, jnp.int32)
        key = bits ^ ((bits >> 31) & 0x7FFFFFFF)
        causal = (c * tk + krow0) <= qcol
        key = jnp.where(causal, key, INT_MIN)
        sc_ref[c] = key
        nonneg = jnp.where(key >= 0, one, zero)
        return (jnp.maximum(kmax, jnp.max(key.reshape(tk // 8, 8, tq), axis=0)),
                n_pos + jnp.sum(nonneg.reshape(tk // 8, 8, tq), axis=0))

    kmax, n_pos = lax.fori_loop(0, nvalid, score_chunk,
                                (jnp.full((8, tq), INT_MIN, jnp.int32), jnp.zeros((8, tq), jnp.int32)))
    kmax = jnp.max(kmax, axis=0, keepdims=True)
    f0 = jnp.sum(n_pos, axis=0, keepdims=True)

    group = _count_group(nchunks)
    if group > 1:
        def pad_chunk(c, carry):
            sc_ref[c] = jnp.full((tk, tq), INT_MIN, jnp.int32)
            return carry

        lax.fori_loop(nvalid, (nvalid + group - 1) // group * group, pad_chunk, 0)

    def count_ge(cand):
        return _count(sc_ref, nvalid, cand, tk, tq, lambda k, cd, c: jnp.where(k >= cd, one, zero))

    n_keys = qcol + 1
    pos = f0 >= n_sel
    lo0 = jnp.where(pos, 0, INT_MIN + 1)
    hi0 = jnp.where(pos, kmax, jnp.minimum(kmax, -1)) + 1
    flo0 = jnp.where(pos, f0, n_keys)

    def searching(flo, w):
        return jnp.where(flo > n_sel, jnp.where(w > 1, 1.0, 0.0), 0.0)

    def bisect_cond(carry):
        it, lo, w, flo = carry
        return (jnp.max(searching(flo, w)) > 0.0) & (it < 32)

    def bisect_body(carry):
        it, lo, w, flo = carry
        active = searching(flo, w) > 0
        off = w >> 1
        near_top = w - FIRST_PROBE_BELOW_MAX
        off = jnp.where(it == 0, jnp.where(near_top >= 1, near_top, off), off)
        cand = lo + off
        cnt = count_ge(cand)
        ok = cnt >= n_sel
        lo = jnp.where(active, jnp.where(ok, cand, lo), lo)
        flo = jnp.where(active, jnp.where(ok, cnt, flo), flo)
        w = jnp.where(active, jnp.where(ok, w - off, off), w)
        return it + 1, lo, w, flo

    _, lo, _, flo = lax.while_loop(bisect_cond, bisect_body, (jnp.int32(0), lo0, hi0 - lo0, flo0))
    found = n_keys >= n_sel
    tie = jnp.where(found, jnp.where(flo > n_sel, one, zero), zero) > 0
    gt_thr = jnp.where(found, jnp.where(tie, lo, lo - 1), INT_MIN)
    thr = jnp.where(tie, lo, INT_MIN + 1)

    j_scr[...] = jnp.full_like(j_scr, 2 ** idx_bits - 1)
    has_tie = jnp.max(jnp.where(tie, 1.0, 0.0)) > 0.0

    @pl.when(has_tie)
    def _():
        cgt = _count(sc_ref, nvalid, thr, tk, tq, lambda k, cd, c: jnp.where(k > cd, one, zero))
        need = n_sel - cgt

        def jbit(i, jv):
            cand = jv | lax.shift_left(jnp.int32(1), idx_bits - 1 - i)
            cnt = _count(sc_ref, nvalid, cand, tk, tq,
                         lambda k, cd, c: jnp.where(k == thr, jnp.where((c * tk + krow0) < cd, one, zero), zero))
            return jnp.where(cnt < need, cand, jv)

        j_scr[...] = lax.fori_loop(0, idx_bits, jbit, jnp.zeros((1, tq), jnp.int32))

    jlast = j_scr[...]

    def write_chunk(c, carry):
        key = sc_ref[c]
        sel = jnp.where(key > gt_thr, one,
                        jnp.where(key == thr, jnp.where((c * tk + krow0) <= jlast, one, zero), zero))
        mask_ref[pl.ds(pl.multiple_of(c * tk, tk), tk), :] = sel.astype(jnp.int8)
        return carry

    lax.fori_loop(0, nvalid, write_chunk, 0)

    def zero_chunk(c, carry):
        mask_ref[pl.ds(pl.multiple_of(c * tk, tk), tk), :] = jnp.zeros((tk, tq), jnp.int8)
        return carry

    lax.fori_loop(nvalid, nchunks, zero_chunk, 0)


def _indexer_mask(zb, wt, k2, b, s, tq, tk, n_sel):
    nq = s // tq
    idx_bits = max(1, int(np.ceil(np.log2(s))))
    return pl.pallas_call(
        functools.partial(_indexer_kernel, tq=tq, tk=tk, n_sel=n_sel, idx_bits=idx_bits),
        grid=(b, nq),
        in_specs=[
            pl.BlockSpec((tq, IDX_HEADS * IDX_DIM), lambda bi, i: (bi * nq + i, COL_IQ // (IDX_HEADS * IDX_DIM))),
            pl.BlockSpec((None, IDX_HEADS, tq), lambda bi, i: (bi, 0, i)),
            pl.BlockSpec((None, s, LANES), lambda bi, i: (bi, 0, 0)),
        ],
        out_specs=pl.BlockSpec((None, s, tq), lambda bi, i: (bi, 0, i)),
        out_shape=jax.ShapeDtypeStruct((b, s, s), jnp.int8),
        scratch_shapes=[
            pltpu.VMEM((s // tk, tk, tq), jnp.int32),
            pltpu.VMEM((IDX_HEADS, tq, LANES), BF16),
            pltpu.VMEM((1, tq), jnp.int32),
        ],
        compiler_params=_params("parallel", "arbitrary"),
        name="indexer_mask",
    )(zb, wt, k2)


def _mla_up_kernel(c_ref, zs_ref, qg_ref, kvg_ref, wq_ref, wk_ref, wv_ref, c64_ref, s64l_ref, s64h_ref,
                   q_ref, k_ref, v_ref, *, q_scale):
    def norm(xf, g):
        y = xf * lax.rsqrt(jnp.mean(xf * xf, axis=-1, keepdims=True) + EPS)
        return (y * g).astype(BF16)

    cq = norm(c_ref[:, :Q_LORA].astype(F32), qg_ref[...])
    ckv = norm(c_ref[:, Q_LORA:].astype(F32), kvg_ref[...])
    q = jnp.dot(cq, wq_ref[...], preferred_element_type=F32) * q_scale
    k = jnp.dot(ckv, wk_ref[...], preferred_element_type=F32)
    v_ref[...] = jnp.dot(ckv, wv_ref[...], preferred_element_type=F32).T.astype(BF16)
    lane = lax.broadcasted_iota(jnp.int32, (1, LANES), 1)
    k_rot = jnp.where(lane < QK_ROPE, zs_ref[:, LANES:], 0.0)
    cos, sin_lo, sin_hi = c64_ref[...], s64l_ref[...], s64h_ref[...]
    for h in range(N_HEADS):
        nope = slice(h * MLA_QK_PAD, h * MLA_QK_PAD + QK_NOPE)
        rope = slice(h * MLA_QK_PAD + QK_NOPE, (h + 1) * MLA_QK_PAD)
        q_ref[:, nope] = q[:, nope].astype(BF16)
        q_ref[:, rope] = _rope64(q[:, rope], cos, sin_lo, sin_hi).astype(BF16)
        k_ref[:, nope] = k[:, nope].astype(BF16)
        k_ref[:, rope] = k_rot.astype(BF16)


def _mla_up(zb, zs, qg, kvg, wq, wk, wv, tabs, tm):
    m = zb.shape[0]
    _, _, c64, s64l, s64h = tabs
    nsb = c64.shape[0] // tm
    tab_spec = pl.BlockSpec((tm, LANES), lambda i: (i % nsb, 0))
    wqk = N_HEADS * MLA_QK_PAD
    full = lambda a: pl.BlockSpec(a.shape, lambda i: (0,) * a.ndim)
    return pl.pallas_call(
        functools.partial(_mla_up_kernel, q_scale=(QK_NOPE + QK_ROPE) ** -0.5 * LOG2E),
        grid=(m // tm,),
        in_specs=[
            pl.BlockSpec((tm, Q_LORA + KV_LORA), lambda i: (i, COL_CQ // (Q_LORA + KV_LORA))),
            pl.BlockSpec((tm, 2 * LANES), lambda i: (i, 0)),
            full(qg), full(kvg), full(wq), full(wk), full(wv),
            tab_spec, tab_spec, tab_spec,
        ],
        out_specs=[
            pl.BlockSpec((tm, wqk), lambda i: (i, 0)),
            pl.BlockSpec((tm, wqk), lambda i: (i, 0)),
            pl.BlockSpec((None, BRANCH_WIDTH, tm), lambda i: (i // nsb, 0, i % nsb)),
        ],
        out_shape=[
            jax.ShapeDtypeStruct((m, wqk), BF16),
            jax.ShapeDtypeStruct((m, wqk), BF16),
            jax.ShapeDtypeStruct((m // (nsb * tm), BRANCH_WIDTH, nsb * tm), BF16),
        ],
        compiler_params=_params("parallel"),
        name="mla_up",
    )(zb, zs, qg, kvg, wq, wk, wv, c64, s64l, s64h)


def _branch_kernel(o0_ref, o1_ref, o2_ref, g0_ref, g1_ref, g2_ref, m0_ref, m1_ref, m2_ref, w_ref, y_ref):
    y = None
    for i, (o_ref, g_ref, m_ref) in enumerate(((o0_ref, g0_ref, m0_ref), (o1_ref, g1_ref, m1_ref),
                                               (o2_ref, g2_ref, m2_ref))):
        g = g_ref[...].astype(F32)
        a = (o_ref[...].astype(F32) * (g * jax.nn.sigmoid(g))).astype(BF16)
        term = jax.nn.sigmoid(m_ref[...].astype(F32)) * jnp.dot(a, w_ref[i], preferred_element_type=F32)
        y = term if y is None else y + term
    y_ref[...] = y.astype(BF16)


def _branch_merge(o_fox, o_dsa, o_mla, zb, wb, d, tm, tn):
    m = zb.shape[0]
    o_spec = pl.BlockSpec((tm, BRANCH_WIDTH), lambda i, j: (i, 0))
    silu0 = (COL_MERGE + N_BRANCH * d) // BRANCH_WIDTH

    def g_spec(i_br):
        return pl.BlockSpec((tm, BRANCH_WIDTH), lambda i, j: (i, silu0 + i_br))

    def m_spec(i_br):
        base = (COL_MERGE + i_br * d) // tn
        return pl.BlockSpec((tm, tn), lambda i, j: (i, base + j))

    return pl.pallas_call(
        _branch_kernel,
        grid=(m // tm, d // tn),
        in_specs=[o_spec, o_spec, o_spec, g_spec(0), g_spec(1), g_spec(2), m_spec(0), m_spec(1), m_spec(2),
                  pl.BlockSpec((N_BRANCH, BRANCH_WIDTH, tn), lambda i, j: (0, 0, j))],
        out_specs=pl.BlockSpec((tm, tn), lambda i, j: (i, j)),
        out_shape=jax.ShapeDtypeStruct((m, d), BF16),
        compiler_params=_params("parallel", "arbitrary"),
        name="branch_merge",
    )(o_fox, o_dsa, o_mla, zb, zb, zb, zb, zb, zb, wb)


def _out_kernel(x_ref, y_ref, w_ref, g_ref, o_ref, *, final_norm):
    r = x_ref[...] + jnp.dot(y_ref[...], w_ref[...], preferred_element_type=F32)
    if final_norm:
        r = r * lax.rsqrt(jnp.mean(r * r, axis=-1, keepdims=True) + EPS) * g_ref[...]
    o_ref[...] = r


def _out_proj(x2, y, w, g, tm, final_norm):
    m, d = x2.shape
    return pl.pallas_call(
        functools.partial(_out_kernel, final_norm=final_norm),
        grid=(m // tm,),
        in_specs=[
            pl.BlockSpec((tm, d), lambda i: (i, 0)),
            pl.BlockSpec((tm, d), lambda i: (i, 0)),
            pl.BlockSpec((d, d), lambda i: (0, 0)),
            pl.BlockSpec((1, d), lambda i: (0, 0)),
        ],
        out_specs=pl.BlockSpec((tm, d), lambda i: (i, 0)),
        out_shape=jax.ShapeDtypeStruct((m, d), F32),
        compiler_params=_params("parallel"),
        name="out_proj",
    )(x2, y, w, g)


def _split_w_in(w_in, d):
    sizes = (BRANCH_WIDTH, BRANCH_WIDTH, BRANCH_WIDTH, N_HEADS, BRANCH_WIDTH, BRANCH_WIDTH, BRANCH_WIDTH,
             IDX_HEADS * IDX_DIM, IDX_DIM, IDX_HEADS, Q_LORA, KV_LORA, QK_ROPE,
             N_BRANCH * BRANCH_WIDTH, N_BRANCH * d)
    pts = np.cumsum(sizes)[:-1].tolist()
    (f_q, f_k, f_v, f_gate, d_q, d_k, d_v, i_q, i_k, i_w, c_q, c_kv, k_pe, silu_g, merge_g) = jnp.split(
        w_in, pts, axis=-1)
    big = jnp.concatenate([f_q, f_k, f_v, d_q, d_k, d_v, i_q, c_q, c_kv, merge_g, silu_g], axis=-1).astype(BF16)
    pad = jnp.zeros((w_in.shape[0], LANES - FOX_BIAS_COLS - IDX_HEADS), w_in.dtype)
    small = jnp.concatenate([f_gate, f_gate, f_gate, i_w, pad, k_pe, i_k], axis=-1).astype(BF16)
    return big, small


def _pad_heads(w, take, width_in):
    r = w.shape[0]
    w3 = w.reshape(r, N_HEADS, width_in)[:, :, take]
    w3 = jnp.pad(w3, ((0, 0), (0, 0), (0, MLA_QK_PAD - w3.shape[-1])))
    return w3.reshape(r, N_HEADS * MLA_QK_PAD)


def _tile(s, pref):
    return pref if s % pref == 0 else s


def kernel(x, norm_g, w_in, forget_b, q_norm_g, w_q_up, kv_norm_g, w_kv_up, w_branch, w_out, final_norm_g):
    b, s, d = x.shape
    depth = w_in.shape[0]
    n_sel = min(TOPK_MAX, s // 4)
    m = b * s
    tabs = _rope_tables(s)
    t_att = _tile(s, 512)
    tm_in = _tile(s, 1024)
    x2 = x.reshape(m, d)
    for l in range(depth):
        w_big, w_small = _split_w_in(w_in[l], d)
        zb, zs, vt, norms = _in_proj(x2, norm_g[l].reshape(1, d), w_big, w_small, tabs, tm_in, 512)

        key_bias = _forget_cumsum(zs, forget_b[l].reshape(1, N_HEADS), b, s, _tile(s, 256))
        o_fox = _flash(zb, COL_FQ // BRANCH_WIDTH, zb, COL_FK // BRANCH_WIDTH, vt, 0,
                       b, s, _tile(s, 1024), HEAD_DIM, "flash_fox", key_bias=key_bias, norms=norms)

        zs3 = zs.reshape(b, s, 2 * LANES)
        wt = jnp.swapaxes(zs3[:, :, FOX_BIAS_COLS:FOX_BIAS_COLS + IDX_HEADS], 1, 2)
        ik = zs3[:, :, LANES + QK_ROPE:].astype(BF16)
        k2 = jnp.concatenate([ik, ik], axis=-1)
        mask = _indexer_mask(zb, wt, k2, b, s, _tile(s, 256), _tile(s, 512), n_sel)
        o_dsa = _flash(zb, COL_DQ // BRANCH_WIDTH, zb, COL_DK // BRANCH_WIDTH, vt, 1,
                       b, s, t_att, HEAD_DIM, "flash_mask", mask=mask)

        wq = _pad_heads(w_q_up[l], slice(0, QK_NOPE + QK_ROPE), QK_NOPE + QK_ROPE).astype(BF16)
        wk = _pad_heads(w_kv_up[l], slice(0, QK_NOPE), QK_NOPE + V_DIM).astype(BF16)
        wv = w_kv_up[l].reshape(KV_LORA, N_HEADS, QK_NOPE + V_DIM)[:, :, QK_NOPE:].reshape(
            KV_LORA, BRANCH_WIDTH).astype(BF16)
        q_mla, k_mla, vt_mla = _mla_up(zb, zs, q_norm_g[l].reshape(1, Q_LORA), kv_norm_g[l].reshape(1, KV_LORA),
                                       wq, wk, wv, tabs, _tile(s, 512))
        o_mla = _flash(q_mla, 0, k_mla, 0, vt_mla, 0, b, s, t_att, MLA_QK_PAD, "flash_plain")

        y = _branch_merge(o_fox, o_dsa, o_mla, zb, w_branch[l].astype(BF16), d, _tile(s, 512), 1024)
        last = l == depth - 1
        x2 = _out_proj(x2, y, w_out[l].astype(BF16), final_norm_g.reshape(1, d), _tile(s, 512), last)
    return x2.reshape(b, s, d)
```

```python
import functools

import jax
import jax.numpy as jnp
import numpy as np
from jax import lax
from jax.experimental import pallas as pl
from jax.experimental.pallas import tpu as pltpu

F32 = jnp.float32
BF16 = jnp.bfloat16

HEAD_DIM = 128
N_HEADS = 8
BRANCH_WIDTH = N_HEADS * HEAD_DIM
N_BRANCH = 3
IDX_HEADS = 16
IDX_DIM = 64
TOPK_MAX = 256
Q_LORA = 512
KV_LORA = 512
QK_NOPE = 128
QK_ROPE = 64
V_DIM = 128
MLA_QK_PAD = 256
FOX_BIAS_COLS = 3 * N_HEADS
NORM_SLACK = 1.02
UNDERFLOW_LOG2 = 160.0
ONES_ROWS = 16
ROPE_THETA = 10000.0
EPS = 1e-6

LANES = 128
VMEM_LIMIT = 56 * 1024 * 1024
NEG_BIG = -1e30
INT_MIN = -(2 ** 31)
LOG2E = 1.4426950408889634
Q_STRIP = 256
FIRST_PROBE_BELOW_MAX = 2 ** 24
SCORE_SPLIT = 4
COUNT_ACCS = 4
PIPE_LEAD = 2
PIPE_LAG = 2

COL_FQ, COL_FK, COL_FV = 0, 1024, 2048
COL_DQ, COL_DK, COL_DV = 3072, 4096, 5120
COL_IQ = 6144
COL_CQ = 7168
COL_MERGE = 8192


def _params(*sem):
    return pltpu.CompilerParams(dimension_semantics=sem, vmem_limit_bytes=VMEM_LIMIT)


def _rope_tables(s):
    pos = jnp.arange(s, dtype=F32)

    def cs(half):
        inv = ROPE_THETA ** (-jnp.arange(half, dtype=F32) / half)
        ang = pos[:, None] * inv[None, :]
        return jnp.cos(ang), jnp.sin(ang)

    c64, s64 = cs(64)
    c32, s32 = cs(32)
    z32 = jnp.zeros_like(s32)
    cos128 = jnp.concatenate([c64, c64], axis=-1)
    sin128 = jnp.concatenate([-s64, s64], axis=-1)
    cos64 = jnp.concatenate([c32, c32, c32, c32], axis=-1)
    sin64_lo = jnp.concatenate([-s32, z32, -s32, z32], axis=-1)
    sin64_hi = jnp.concatenate([z32, s32, z32, s32], axis=-1)
    return cos128, sin128, cos64, sin64_lo, sin64_hi


def _rope128(x, cos, sin):
    return x * cos + pltpu.roll(x, 64, 1) * sin


def _rope64(x, cos, sin_lo, sin_hi):
    return x * cos + pltpu.roll(x, 96, 1) * sin_lo + pltpu.roll(x, 32, 1) * sin_hi


def _in_proj_kernel(x_ref, g_ref, w_ref, ws_ref, c128_ref, s128_ref, c64_ref, s64l_ref, s64h_ref,
                    zb_ref, zs_ref, vt_ref, nrm_ref, h_scr, *, tn, q_scale):
    j = pl.program_id(1)

    @pl.when(j == 0)
    def _():
        nrm_ref[...] = jnp.zeros_like(nrm_ref)
        xf = x_ref[...]
        y = xf * lax.rsqrt(jnp.mean(xf * xf, axis=-1, keepdims=True) + EPS)
        h = (y * g_ref[...]).astype(BF16)
        h_scr[...] = h
        zs = jnp.dot(h, ws_ref[...], preferred_element_type=F32)
        zs_ref[:, :LANES] = zs[:, :LANES]
        zs_ref[:, LANES:] = _rope64(zs[:, LANES:], c64_ref[...], s64l_ref[...], s64h_ref[...])

    acc = jnp.dot(h_scr[...], w_ref[...], preferred_element_type=F32)
    col = j * tn
    is_q = ((col >= COL_FQ) & (col < COL_FK)) | ((col >= COL_DQ) & (col < COL_DK))
    acc = acc * jnp.where(is_q, q_scale, 1.0).astype(F32)
    is_r128 = (col >= COL_DQ) & (col < COL_DV)
    is_r64 = (col >= COL_IQ) & (col < COL_CQ)

    @pl.when(is_r128)
    def _():
        for g in range(tn // LANES):
            sl = slice(g * LANES, (g + 1) * LANES)
            zb_ref[:, sl] = _rope128(acc[:, sl], c128_ref[...], s128_ref[...]).astype(BF16)

    @pl.when(is_r64)
    def _():
        for g in range(tn // LANES):
            sl = slice(g * LANES, (g + 1) * LANES)
            zb_ref[:, sl] = _rope64(acc[:, sl], c64_ref[...], s64l_ref[...], s64h_ref[...]).astype(BF16)

    @pl.when(jnp.logical_not(is_r128 | is_r64))
    def _():
        zb_ref[...] = acc.astype(BF16)

    is_v = ((col >= COL_FV) & (col < COL_DQ)) | ((col >= COL_DV) & (col < COL_IQ))

    @pl.when(is_v)
    def _():
        vt_ref[...] = acc.T.astype(BF16)

    is_fq = (col >= COL_FQ) & (col < COL_FK)
    is_fk = (col >= COL_FK) & (col < COL_FV)

    @pl.when(is_fq | is_fk)
    def _():
        lane = lax.broadcasted_iota(jnp.int32, (1, LANES), 1)
        first = (col % BRANCH_WIDTH) // HEAD_DIM + jnp.where(is_fq, N_HEADS, 0)
        nrm = nrm_ref[...]
        for g in range(tn // HEAD_DIM):
            a = acc[:, g * HEAD_DIM:(g + 1) * HEAD_DIM]
            nrm = jnp.where(lane == first + g, jnp.sum(a * a, axis=1, keepdims=True), nrm)
        nrm_ref[...] = nrm


def _vt_row_block(j, tn):
    per = BRANCH_WIDTH // tn
    fv0, dv0 = COL_FV // tn, COL_DV // tn
    in_fox = jnp.clip(j - fv0, 0, per - 1)
    in_dsa = per + jnp.clip(j - dv0, 0, per - 1)
    return jnp.where(j < dv0, in_fox, in_dsa)


def _in_proj(x2, g, w_big, w_small, tabs, tm, tn):
    m, d = x2.shape
    n = w_big.shape[1]
    c128, s128, c64, s64l, s64h = tabs
    s = c128.shape[0]
    nsb = s // tm
    tab_spec = pl.BlockSpec((tm, LANES), lambda i, j: (i % nsb, 0))
    return pl.pallas_call(
        functools.partial(_in_proj_kernel, tn=tn, q_scale=HEAD_DIM ** -0.5 * LOG2E),
        grid=(m // tm, n // tn),
        in_specs=[
            pl.BlockSpec((tm, d), lambda i, j: (i, 0)),
            pl.BlockSpec((1, d), lambda i, j: (0, 0)),
            pl.BlockSpec((d, tn), lambda i, j: (0, j)),
            pl.BlockSpec((d, 2 * LANES), lambda i, j: (0, 0)),
            tab_spec, tab_spec, tab_spec, tab_spec, tab_spec,
        ],
        out_specs=[
            pl.BlockSpec((tm, tn), lambda i, j: (i, j)),
            pl.BlockSpec((tm, 2 * LANES), lambda i, j: (i, 0)),
            pl.BlockSpec((None, tn, tm), lambda i, j: (i // nsb, _vt_row_block(j, tn), i % nsb)),
            pl.BlockSpec((tm, LANES), lambda i, j: (i, 0)),
        ],
        out_shape=[
            jax.ShapeDtypeStruct((m, n), BF16),
            jax.ShapeDtypeStruct((m, 2 * LANES), F32),
            jax.ShapeDtypeStruct((m // s, 2 * BRANCH_WIDTH, s), BF16),
            jax.ShapeDtypeStruct((m, LANES), F32),
        ],
        scratch_shapes=[pltpu.VMEM((tm, d), BF16)],
        compiler_params=_params("parallel", "arbitrary"),
        name="in_proj",
    )(x2, g, w_big, w_small, c128, s128, c64, s64l, s64h)


def _bf16_floor(x):
    return pltpu.bitcast(pltpu.bitcast(x, jnp.int32) & jnp.int32(-65536), F32)


def _cum_kernel(zs_ref, fb_ref, kb_ref, carry_scr, *, t):
    @pl.when(pl.program_id(1) == 0)
    def _():
        carry_scr[...] = jnp.zeros_like(carry_scr)

    lane = lax.broadcasted_iota(jnp.int32, (1, LANES), 1)
    x = zs_ref[:, :LANES] + fb_ref[...]
    logf = -(jnp.maximum(-x, 0.0) + jnp.log1p(jnp.exp(-jnp.abs(x))))
    logf = jnp.where(lane < FOX_BIAS_COLS, logf, 0.0)
    r = lax.broadcasted_iota(jnp.int32, (t, t), 0)
    c = lax.broadcasted_iota(jnp.int32, (t, t), 1)
    tri = jnp.where(c <= r, 1.0, 0.0).astype(F32)
    cum = jnp.dot(tri, logf, preferred_element_type=F32, precision=lax.Precision.HIGHEST) + carry_scr[...]
    carry_scr[...] = cum[t - 1:t, :]
    c = cum * (-LOG2E)
    hi = _bf16_floor(c)
    mid = _bf16_floor(c - hi)
    lo = (c - hi) - mid
    kb_ref[...] = jnp.where(lane < N_HEADS, hi, jnp.where(lane < 2 * N_HEADS, mid, lo)).astype(BF16)


def _forget_cumsum(zs, fb, b, s, t):
    fb128 = jnp.pad(jnp.tile(fb, (1, 3)), ((0, 0), (0, LANES - FOX_BIAS_COLS)))
    return pl.pallas_call(
        functools.partial(_cum_kernel, t=t),
        grid=(b, s // t),
        in_specs=[
            pl.BlockSpec((t, 2 * LANES), lambda bi, i: (bi * (s // t) + i, 0)),
            pl.BlockSpec((1, LANES), lambda bi, i: (0, 0)),
        ],
        out_specs=pl.BlockSpec((t, LANES), lambda bi, i: (bi * (s // t) + i, 0)),
        out_shape=jax.ShapeDtypeStruct((b * s, LANES), BF16),
        scratch_shapes=[pltpu.VMEM((1, LANES), F32)],
        compiler_params=_params("parallel", "arbitrary"),
        name="forget_cumsum",
    )(zs, fb128)


def _flash_kernel(qt_ref, kt_ref, q_ref, k_ref, v_ref, *rest, extra, dk, t):
    qn_scr = None
    if extra is None:
        o_ref, m_scr, l_scr, acc_scr = rest
        extra_ref = None
    elif extra == "mask":
        extra_ref, o_ref, m_scr, l_scr, acc_scr = rest
    else:
        extra_ref, qnorm_ref, knorm_ref, o_ref, m_scr, l_scr, acc_scr, qn_scr = rest
    masked = extra == "mask"
    p = pl.program_id(1)
    qi = qt_ref[p]
    ki = kt_ref[p]
    qs = min(t, Q_STRIP)

    @pl.when(ki == qi)
    def _():
        m_scr[...] = jnp.full_like(m_scr, NEG_BIG)
        l_scr[...] = jnp.zeros_like(l_scr)
        acc_scr[...] = jnp.zeros_like(acc_scr)
        if qn_scr is not None:
            qn_scr[...] = pltpu.roll(jnp.max(qnorm_ref[...], axis=0, keepdims=True), LANES - N_HEADS, 1)

    def head_margins():
        kn = jnp.max(knorm_ref[...], axis=0, keepdims=True)
        dot_bound = jnp.sqrt(qn_scr[...] * kn) * NORM_SLACK
        bm = jnp.max(extra_ref[...].astype(F32), axis=0, keepdims=True)
        bias = bm + pltpu.roll(bm, LANES - N_HEADS, 1) + pltpu.roll(bm, LANES - 2 * N_HEADS, 1)
        bound = dot_bound + bias
        m_min = jnp.min(m_scr[...], axis=1, keepdims=True)
        head = lax.broadcasted_iota(jnp.int32, (N_HEADS, LANES), 0)
        lane = lax.broadcasted_iota(jnp.int32, (N_HEADS, LANES), 1)
        margin = jnp.where(head == lane, m_min - bound, jnp.inf)
        return jnp.min(margin, axis=1, keepdims=True)

    all_units = [(c, h) for c in range(t // qs) for h in range(N_HEADS)]

    def step(diag, units):

        def n_keys(c):
            return (c + 1) * qs if diag else t

        def logits(c, h):
            k_h = k_ref[:n_keys(c), h * dk:(h + 1) * dk]
            q_h = q_ref[c * qs:(c + 1) * qs, h * dk:(h + 1) * dk]
            if extra == "key_bias":
                lane = lax.broadcasted_iota(jnp.int32, (qs, LANES), 1)
                pick = jnp.where(lane < FOX_BIAS_COLS, jnp.where(lane % N_HEADS == h, 1.0, 0.0), 0.0)
                k_h = jnp.concatenate([k_h, extra_ref[:n_keys(c), :]], axis=1)
                q_h = jnp.concatenate([q_h, pick.astype(q_h.dtype)], axis=1)
            return lax.dot_general(k_h, q_h, (((1,), (1,)), ((), ())), preferred_element_type=F32)

        def keep_mask(c):
            nk = n_keys(c)
            if masked:
                return extra_ref[:nk, c * qs:(c + 1) * qs].astype(jnp.int32) != 0
            if diag:
                kr = lax.broadcasted_iota(jnp.int32, (nk, qs), 0)
                qc = lax.broadcasted_iota(jnp.int32, (nk, qs), 1) + c * qs
                return kr <= qc
            return None

        keeps = {}

        def softmax(c, h, st):
            cs = slice(c * qs, (c + 1) * qs)
            if c not in keeps:
                keeps[c] = keep_mask(c)
            if keeps[c] is not None:
                st = jnp.where(keeps[c], st, NEG_BIG)
            m_prev = m_scr[h:h + 1, cs]
            m_new = jnp.maximum(m_prev, jnp.max(st, axis=0, keepdims=True))
            m_scr[h:h + 1, cs] = m_new
            alpha = jnp.exp2(m_prev - m_new)
            return alpha, jnp.exp2(st - m_new).astype(BF16)

        def accumulate(c, h, alpha, pt):
            cs = slice(c * qs, (c + 1) * qs)
            nk = n_keys(c)
            ones = jnp.ones((ONES_ROWS, nk), v_ref.dtype)
            vt_h = jnp.concatenate([v_ref[h * HEAD_DIM:(h + 1) * HEAD_DIM, :nk], ones], axis=0)
            pv = jnp.dot(vt_h, pt, preferred_element_type=F32)
            sl = slice(h * HEAD_DIM, (h + 1) * HEAD_DIM)
            acc_scr[sl, cs] = alpha * acc_scr[sl, cs] + pv[:HEAD_DIM]
            l_scr[h:h + 1, cs] = alpha * l_scr[h:h + 1, cs] + pv[HEAD_DIM:HEAD_DIM + 1]

        n_u = len(units)
        st_q = [logits(*units[i]) for i in range(min(PIPE_LEAD, n_u))]
        p_q = []
        for u in range(n_u + PIPE_LAG):
            if u + PIPE_LEAD < n_u:
                st_q.append(logits(*units[u + PIPE_LEAD]))
            if u < n_u:
                p_q.append(softmax(*units[u], st_q.pop(0)))
            if u >= PIPE_LAG:
                accumulate(*units[u - PIPE_LAG], *p_q.pop(0))

    @pl.when(ki == qi)
    def _():
        step(True, all_units)

    if qn_scr is None:
        @pl.when(ki != qi)
        def _():
            step(False, all_units)
    else:
        margins = head_margins()
        off_diag = ki != qi
        every_head = jnp.logical_and(off_diag, jnp.max(margins) <= UNDERFLOW_LOG2)

        @pl.when(every_head)
        def _():
            step(False, all_units)

        some_heads = jnp.logical_and(off_diag, jnp.logical_not(every_head))
        for h in range(N_HEADS):
            @pl.when(jnp.logical_and(some_heads, margins[h, 0] <= UNDERFLOW_LOG2))
            def _(h=h):
                step(False, [(c, h) for c in range(t // qs)])

    @pl.when(ki == 0)
    def _():
        for h in range(N_HEADS):
            sl = slice(h * HEAD_DIM, (h + 1) * HEAD_DIM)
            o = acc_scr[sl, :] / l_scr[h:h + 1, :]
            o_ref[:, sl] = o.T.astype(BF16)


def _pair_tables(n):
    qt = np.concatenate([np.full(i + 1, i, np.int32) for i in range(n)])
    kt = np.concatenate([np.arange(i, -1, -1, dtype=np.int32) for i in range(n)])
    return jnp.asarray(qt), jnp.asarray(kt)


def _flash(q_arr, q_col, k_arr, k_col, vt_arr, vt_blk, b, s, t, dk, name, mask=None, key_bias=None, norms=None):
    n = s // t
    qt, kt = _pair_tables(n)
    wq = N_HEADS * dk
    wv = N_HEADS * HEAD_DIM
    in_specs = [
        pl.BlockSpec((t, wq), lambda bi, p, qt, kt: (bi * n + qt[p], q_col)),
        pl.BlockSpec((t, wq), lambda bi, p, qt, kt: (bi * n + kt[p], k_col)),
        pl.BlockSpec((None, wv, t), lambda bi, p, qt, kt: (bi, vt_blk, kt[p])),
    ]
    args = [q_arr, k_arr, vt_arr]
    extra = None
    if mask is not None:
        extra = "mask"
        in_specs.append(pl.BlockSpec((None, t, t), lambda bi, p, qt, kt: (bi, kt[p], qt[p])))
        args.append(mask)
    elif key_bias is not None:
        extra = "key_bias"
        in_specs.append(pl.BlockSpec((t, LANES), lambda bi, p, qt, kt: (bi * n + kt[p], 0)))
        in_specs.append(pl.BlockSpec((t, LANES), lambda bi, p, qt, kt: (bi * n + qt[p], 0)))
        in_specs.append(pl.BlockSpec((t, LANES), lambda bi, p, qt, kt: (bi * n + kt[p], 0)))
        args += [key_bias, norms, norms]
    scratch = [pltpu.VMEM((N_HEADS, t), F32), pltpu.VMEM((N_HEADS, t), F32), pltpu.VMEM((wv, t), F32)]
    if extra == "key_bias":
        scratch.append(pltpu.VMEM((1, LANES), F32))
    grid_spec = pltpu.PrefetchScalarGridSpec(
        num_scalar_prefetch=2,
        grid=(b, n * (n + 1) // 2),
        in_specs=in_specs,
        out_specs=pl.BlockSpec((t, wv), lambda bi, p, qt, kt: (bi * n + qt[p], 0)),
        scratch_shapes=scratch,
    )
    return pl.pallas_call(
        functools.partial(_flash_kernel, extra=extra, dk=dk, t=t),
        grid_spec=grid_spec,
        out_shape=jax.ShapeDtypeStruct((b * s, wv), BF16),
        compiler_params=_params("parallel", "arbitrary"),
        name=name,
    )(qt, kt, *args)


def _count(sc_ref, nvalid, cand, tk, tq, ind_fn):
    group = _count_group(sc_ref.shape[0])

    lanes = COUNT_ACCS * 8

    def body(g, a):
        for e in range(group):
            c = g * group + e
            ind = ind_fn(sc_ref[c], cand, c)
            a = a + jnp.sum(ind.reshape(tk // lanes, lanes, tq), axis=0)
        return a

    part = lax.fori_loop(0, (nvalid + group - 1) // group, body, jnp.zeros((lanes, tq), jnp.int32))
    return jnp.sum(part, axis=0, keepdims=True)


def _count_group(nchunks):
    return 2 if nchunks % 2 == 0 else 1


def _indexer_kernel(q_ref, wt_ref, k2_ref, mask_ref, sc_ref, qm_scr, j_scr, *, tq, tk, n_sel, idx_bits):
    qi = pl.program_id(1)
    one, zero = jnp.int32(1), jnp.int32(0)
    nchunks = sc_ref.shape[0]
    nvalid = ((qi + 1) * tq + tk - 1) // tk
    qcol = qi * tq + lax.broadcasted_iota(jnp.int32, (1, tq), 1)
    krow0 = lax.broadcasted_iota(jnp.int32, (tk, 1), 0)

    half = lax.broadcasted_iota(jnp.int32, (tq, LANES), 1) // IDX_DIM
    for g in range(IDX_HEADS // 2):
        qp = q_ref[:, g * LANES:(g + 1) * LANES]
        for e in range(2):
            qm_scr[2 * g + e] = jnp.where(half == e, qp, jnp.zeros_like(qp))

    def score_chunk(c, carry):
        kmax, n_pos = carry
        th = tk // SCORE_SPLIT
        for part in range(SCORE_SPLIT):
            kc = k2_ref[pl.ds(pl.multiple_of(c * tk + part * th, th), th), :]
            acc = jnp.zeros((th, tq), F32)
            for h in range(IDX_HEADS):
                x = lax.dot_general(kc, qm_scr[h], (((1,), (1,)), ((), ())), preferred_element_type=F32)
                acc = acc + wt_ref[h:h + 1, :] * jnp.maximum(x, 0.0)
            score = acc * (IDX_HEADS * IDX_DIM) ** -0.5
            bits = pltpu.bitcast(score, jnp.int32)
            key = bits ^ ((bits >> 31) & 0x7FFFFFFF)
            causal = (c * tk + part * th + krow0[:th]) <= qcol
            key = jnp.where(causal, key, INT_MIN)
            sc_ref[c, part * th:(part + 1) * th, :] = key
            nonneg = jnp.where(key >= 0, one, zero)
            kmax = jnp.maximum(kmax, jnp.max(key.reshape(th // 8, 8, tq), axis=0))
            n_pos = n_pos + jnp.sum(nonneg.reshape(th // 8, 8, tq), axis=0)
        return kmax, n_pos

    kmax, n_pos = lax.fori_loop(0, nvalid, score_chunk,
                                (jnp.full((8, tq), INT_MIN, jnp.int32), jnp.zeros((8, tq), jnp.int32)))
    kmax = jnp.max(kmax, axis=0, keepdims=True)
    f0 = jnp.sum(n_pos, axis=0, keepdims=True)

    group = _count_group(nchunks)
    if group > 1:
        def pad_chunk(c, carry):
            sc_ref[c] = jnp.full((tk, tq), INT_MIN, jnp.int32)
            return carry

        lax.fori_loop(nvalid, (nvalid + group - 1) // group * group, pad_chunk, 0)

    def count_ge(cand):
        return _count(sc_ref, nvalid, cand, tk, tq, lambda k, cd, c: jnp.where(k >= cd, one, zero))

    n_keys = qcol + 1
    pos = f0 >= n_sel
    lo0 = jnp.where(pos, 0, INT_MIN + 1)
    hi0 = jnp.where(pos, kmax, jnp.minimum(kmax, -1)) + 1
    flo0 = jnp.where(pos, f0, n_keys)

    def searching(flo, w):
        return jnp.where(flo > n_sel, jnp.where(w > 1, 1.0, 0.0), 0.0)

    def bisect_cond(carry):
        it, lo, w, flo = carry
        return (jnp.max(searching(flo, w)) > 0.0) & (it < 32)

    def bisect_body(carry):
        it, lo, w, flo = carry
        active = searching(flo, w) > 0
        off = w >> 1
        near_top = w - FIRST_PROBE_BELOW_MAX
        off = jnp.where(it == 0, jnp.where(near_top >= 1, near_top, off), off)
        cand = lo + off
        cnt = count_ge(cand)
        ok = cnt >= n_sel
        lo = jnp.where(active, jnp.where(ok, cand, lo), lo)
        flo = jnp.where(active, jnp.where(ok, cnt, flo), flo)
        w = jnp.where(active, jnp.where(ok, w - off, off), w)
        return it + 1, lo, w, flo

    _, lo, _, flo = lax.while_loop(bisect_cond, bisect_body, (jnp.int32(0), lo0, hi0 - lo0, flo0))
    found = n_keys >= n_sel
    tie = jnp.where(found, jnp.where(flo > n_sel, one, zero), zero) > 0
    gt_thr = jnp.where(found, jnp.where(tie, lo, lo - 1), INT_MIN)
    thr = jnp.where(tie, lo, INT_MIN + 1)

    j_scr[...] = jnp.full_like(j_scr, 2 ** idx_bits - 1)
    has_tie = jnp.max(jnp.where(tie, 1.0, 0.0)) > 0.0

    @pl.when(has_tie)
    def _():
        cgt = _count(sc_ref, nvalid, thr, tk, tq, lambda k, cd, c: jnp.where(k > cd, one, zero))
        need = n_sel - cgt

        def jbit(i, jv):
            cand = jv | lax.shift_left(jnp.int32(1), idx_bits - 1 - i)
            cnt = _count(sc_ref, nvalid, cand, tk, tq,
                         lambda k, cd, c: jnp.where(k == thr, jnp.where((c * tk + krow0) < cd, one, zero), zero))
            return jnp.where(cnt < need, cand, jv)

        j_scr[...] = lax.fori_loop(0, idx_bits, jbit, jnp.zeros((1, tq), jnp.int32))

    jlast = j_scr[...]

    def write_chunk(c, carry):
        key = sc_ref[c]
        sel = jnp.where(key > gt_thr, one,
                        jnp.where(key == thr, jnp.where((c * tk + krow0) <= jlast, one, zero), zero))
        mask_ref[pl.ds(pl.multiple_of(c * tk, tk), tk), :] = sel.astype(jnp.int8)
        return carry

    lax.fori_loop(0, nvalid, write_chunk, 0)

    def zero_chunk(c, carry):
        mask_ref[pl.ds(pl.multiple_of(c * tk, tk), tk), :] = jnp.zeros((tk, tq), jnp.int8)
        return carry

    lax.fori_loop(nvalid, nchunks, zero_chunk, 0)


def _indexer_mask(zb, wt, k2, b, s, tq, tk, n_sel):
    nq = s // tq
    idx_bits = max(1, int(np.ceil(np.log2(s))))
    return pl.pallas_call(
        functools.partial(_indexer_kernel, tq=tq, tk=tk, n_sel=n_sel, idx_bits=idx_bits),
        grid=(b, nq),
        in_specs=[
            pl.BlockSpec((tq, IDX_HEADS * IDX_DIM), lambda bi, i: (bi * nq + i, COL_IQ // (IDX_HEADS * IDX_DIM))),
            pl.BlockSpec((None, IDX_HEADS, tq), lambda bi, i: (bi, 0, i)),
            pl.BlockSpec((None, s, LANES), lambda bi, i: (bi, 0, 0)),
        ],
        out_specs=pl.BlockSpec((None, s, tq), lambda bi, i: (bi, 0, i)),
        out_shape=jax.ShapeDtypeStruct((b, s, s), jnp.int8),
        scratch_shapes=[
            pltpu.VMEM((s // tk, tk, tq), jnp.int32),
            pltpu.VMEM((IDX_HEADS, tq, LANES), BF16),
            pltpu.VMEM((1, tq), jnp.int32),
        ],
        compiler_params=_params("parallel", "arbitrary"),
        name="indexer_mask",
    )(zb, wt, k2)


def _mla_up_kernel(c_ref, zs_ref, qg_ref, kvg_ref, wq_ref, wk_ref, wv_ref, c64_ref, s64l_ref, s64h_ref,
                   q_ref, k_ref, v_ref, *, q_scale):
    def norm(xf, g):
        y = xf * lax.rsqrt(jnp.mean(xf * xf, axis=-1, keepdims=True) + EPS)
        return (y * g).astype(BF16)

    cq = norm(c_ref[:, :Q_LORA].astype(F32), qg_ref[...])
    ckv = norm(c_ref[:, Q_LORA:].astype(F32), kvg_ref[...])
    q = jnp.dot(cq, wq_ref[...], preferred_element_type=F32) * q_scale
    k = jnp.dot(ckv, wk_ref[...], preferred_element_type=F32)
    v_ref[...] = jnp.dot(ckv, wv_ref[...], preferred_element_type=F32).T.astype(BF16)
    lane = lax.broadcasted_iota(jnp.int32, (1, LANES), 1)
    k_rot = jnp.where(lane < QK_ROPE, zs_ref[:, LANES:], 0.0)
    cos, sin_lo, sin_hi = c64_ref[...], s64l_ref[...], s64h_ref[...]
    for h in range(N_HEADS):
        nope = slice(h * MLA_QK_PAD, h * MLA_QK_PAD + QK_NOPE)
        rope = slice(h * MLA_QK_PAD + QK_NOPE, (h + 1) * MLA_QK_PAD)
        q_ref[:, nope] = q[:, nope].astype(BF16)
        q_ref[:, rope] = _rope64(q[:, rope], cos, sin_lo, sin_hi).astype(BF16)
        k_ref[:, nope] = k[:, nope].astype(BF16)
        k_ref[:, rope] = k_rot.astype(BF16)


def _mla_up(zb, zs, qg, kvg, wq, wk, wv, tabs, tm):
    m = zb.shape[0]
    _, _, c64, s64l, s64h = tabs
    nsb = c64.shape[0] // tm
    tab_spec = pl.BlockSpec((tm, LANES), lambda i: (i % nsb, 0))
    wqk = N_HEADS * MLA_QK_PAD
    full = lambda a: pl.BlockSpec(a.shape, lambda i: (0,) * a.ndim)
    return pl.pallas_call(
        functools.partial(_mla_up_kernel, q_scale=(QK_NOPE + QK_ROPE) ** -0.5 * LOG2E),
        grid=(m // tm,),
        in_specs=[
            pl.BlockSpec((tm, Q_LORA + KV_LORA), lambda i: (i, COL_CQ // (Q_LORA + KV_LORA))),
            pl.BlockSpec((tm, 2 * LANES), lambda i: (i, 0)),
            full(qg), full(kvg), full(wq), full(wk), full(wv),
            tab_spec, tab_spec, tab_spec,
        ],
        out_specs=[
            pl.BlockSpec((tm, wqk), lambda i: (i, 0)),
            pl.BlockSpec((tm, wqk), lambda i: (i, 0)),
            pl.BlockSpec((None, BRANCH_WIDTH, tm), lambda i: (i // nsb, 0, i % nsb)),
        ],
        out_shape=[
            jax.ShapeDtypeStruct((m, wqk), BF16),
            jax.ShapeDtypeStruct((m, wqk), BF16),
            jax.ShapeDtypeStruct((m // (nsb * tm), BRANCH_WIDTH, nsb * tm), BF16),
        ],
        compiler_params=_params("parallel"),
        name="mla_up",
    )(zb, zs, qg, kvg, wq, wk, wv, c64, s64l, s64h)


def _branch_kernel(o0_ref, o1_ref, o2_ref, g0_ref, g1_ref, g2_ref, m0_ref, m1_ref, m2_ref, w_ref, y_ref):
    y = None
    for i, (o_ref, g_ref, m_ref) in enumerate(((o0_ref, g0_ref, m0_ref), (o1_ref, g1_ref, m1_ref),
                                               (o2_ref, g2_ref, m2_ref))):
        g = g_ref[...].astype(F32)
        a = (o_ref[...].astype(F32) * (g * jax.nn.sigmoid(g))).astype(BF16)
        term = jax.nn.sigmoid(m_ref[...].astype(F32)) * jnp.dot(a, w_ref[i], preferred_element_type=F32)
        y = term if y is None else y + term
    y_ref[...] = y.astype(BF16)


def _branch_merge(o_fox, o_dsa, o_mla, zb, wb, d, tm, tn):
    m = zb.shape[0]
    o_spec = pl.BlockSpec((tm, BRANCH_WIDTH), lambda i, j: (i, 0))
    silu0 = (COL_MERGE + N_BRANCH * d) // BRANCH_WIDTH

    def g_spec(i_br):
        return pl.BlockSpec((tm, BRANCH_WIDTH), lambda i, j: (i, silu0 + i_br))

    def m_spec(i_br):
        base = (COL_MERGE + i_br * d) // tn
        return pl.BlockSpec((tm, tn), lambda i, j: (i, base + j))

    return pl.pallas_call(
        _branch_kernel,
        grid=(m // tm, d // tn),
        in_specs=[o_spec, o_spec, o_spec, g_spec(0), g_spec(1), g_spec(2), m_spec(0), m_spec(1), m_spec(2),
                  pl.BlockSpec((N_BRANCH, BRANCH_WIDTH, tn), lambda i, j: (0, 0, j))],
        out_specs=pl.BlockSpec((tm, tn), lambda i, j: (i, j)),
        out_shape=jax.ShapeDtypeStruct((m, d), BF16),
        compiler_params=_params("parallel", "arbitrary"),
        name="branch_merge",
    )(o_fox, o_dsa, o_mla, zb, zb, zb, zb, zb, zb, wb)


def _out_kernel(x_ref, y_ref, w_ref, g_ref, o_ref, *, final_norm):
    r = x_ref[...] + jnp.dot(y_ref[...], w_ref[...], preferred_element_type=F32)
    if final_norm:
        r = r * lax.rsqrt(jnp.mean(r * r, axis=-1, keepdims=True) + EPS) * g_ref[...]
    o_ref[...] = r


def _out_proj(x2, y, w, g, tm, final_norm):
    m, d = x2.shape
    return pl.pallas_call(
        functools.partial(_out_kernel, final_norm=final_norm),
        grid=(m // tm,),
        in_specs=[
            pl.BlockSpec((tm, d), lambda i: (i, 0)),
            pl.BlockSpec((tm, d), lambda i: (i, 0)),
            pl.BlockSpec((d, d), lambda i: (0, 0)),
            pl.BlockSpec((1, d), lambda i: (0, 0)),
        ],
        out_specs=pl.BlockSpec((tm, d), lambda i: (i, 0)),
        out_shape=jax.ShapeDtypeStruct((m, d), F32),
        compiler_params=_params("parallel"),
        name="out_proj",
    )(x2, y, w, g)


def _split_w_in(w_in, d):
    sizes = (BRANCH_WIDTH, BRANCH_WIDTH, BRANCH_WIDTH, N_HEADS, BRANCH_WIDTH, BRANCH_WIDTH, BRANCH_WIDTH,
             IDX_HEADS * IDX_DIM, IDX_DIM, IDX_HEADS, Q_LORA, KV_LORA, QK_ROPE,
             N_BRANCH * BRANCH_WIDTH, N_BRANCH * d)
    pts = np.cumsum(sizes)[:-1].tolist()
    (f_q, f_k, f_v, f_gate, d_q, d_k, d_v, i_q, i_k, i_w, c_q, c_kv, k_pe, silu_g, merge_g) = jnp.split(
        w_in, pts, axis=-1)
    big = jnp.concatenate([f_q, f_k, f_v, d_q, d_k, d_v, i_q, c_q, c_kv, merge_g, silu_g], axis=-1).astype(BF16)
    pad = jnp.zeros((w_in.shape[0], LANES - FOX_BIAS_COLS - IDX_HEADS), w_in.dtype)
    small = jnp.concatenate([f_gate, f_gate, f_gate, i_w, pad, k_pe, i_k], axis=-1).astype(BF16)
    return big, small


def _pad_heads(w, take, width_in):
    r = w.shape[0]
    w3 = w.reshape(r, N_HEADS, width_in)[:, :, take]
    w3 = jnp.pad(w3, ((0, 0), (0, 0), (0, MLA_QK_PAD - w3.shape[-1])))
    return w3.reshape(r, N_HEADS * MLA_QK_PAD)


def _tile(s, pref):
    return pref if s % pref == 0 else s


def kernel(x, norm_g, w_in, forget_b, q_norm_g, w_q_up, kv_norm_g, w_kv_up, w_branch, w_out, final_norm_g):
    b, s, d = x.shape
    depth = w_in.shape[0]
    n_sel = min(TOPK_MAX, s // 4)
    m = b * s
    tabs = _rope_tables(s)
    t_att = _tile(s, 512)
    tm_in = _tile(s, 1024)
    x2 = x.reshape(m, d)
    for l in range(depth):
        w_big, w_small = _split_w_in(w_in[l], d)
        zb, zs, vt, norms = _in_proj(x2, norm_g[l].reshape(1, d), w_big, w_small, tabs, tm_in, 512)

        key_bias = _forget_cumsum(zs, forget_b[l].reshape(1, N_HEADS), b, s, _tile(s, 256))
        o_fox = _flash(zb, COL_FQ // BRANCH_WIDTH, zb, COL_FK // BRANCH_WIDTH, vt, 0,
                       b, s, _tile(s, 1024), HEAD_DIM, "flash_fox", key_bias=key_bias, norms=norms)

        zs3 = zs.reshape(b, s, 2 * LANES)
        wt = jnp.swapaxes(zs3[:, :, FOX_BIAS_COLS:FOX_BIAS_COLS + IDX_HEADS], 1, 2)
        ik = zs3[:, :, LANES + QK_ROPE:].astype(BF16)
        k2 = jnp.concatenate([ik, ik], axis=-1)
        mask = _indexer_mask(zb, wt, k2, b, s, _tile(s, 256), _tile(s, 512), n_sel)
        o_dsa = _flash(zb, COL_DQ // BRANCH_WIDTH, zb, COL_DK // BRANCH_WIDTH, vt, 1,
                       b, s, t_att, HEAD_DIM, "flash_mask", mask=mask)

        wq = _pad_heads(w_q_up[l], slice(0, QK_NOPE + QK_ROPE), QK_NOPE + QK_ROPE).astype(BF16)
        wk = _pad_heads(w_kv_up[l], slice(0, QK_NOPE), QK_NOPE + V_DIM).astype(BF16)
        wv = w_kv_up[l].reshape(KV_LORA, N_HEADS, QK_NOPE + V_DIM)[:, :, QK_NOPE:].reshape(
            KV_LORA, BRANCH_WIDTH).astype(BF16)
        q_mla, k_mla, vt_mla = _mla_up(zb, zs, q_norm_g[l].reshape(1, Q_LORA), kv_norm_g[l].reshape(1, KV_LORA),
                                       wq, wk, wv, tabs, _tile(s, 512))
        o_mla = _flash(q_mla, 0, k_mla, 0, vt_mla, 0, b, s, t_att, MLA_QK_PAD, "flash_plain")

        y = _branch_merge(o_fox, o_dsa, o_mla, zb, w_branch[l].astype(BF16), d, _tile(s, 512), 1024)
        last = l == depth - 1
        x2 = _out_proj(x2, y, w_out[l].astype(BF16), final_norm_g.reshape(1, d), _tile(s, 512), last)
    return x2.reshape(b, s, d)
```
